```python
import functools
import math
import jax
import jax.numpy as jnp
from jax import lax
import numpy as np


D_MODEL = 1024
BATCH = 16
SEQ = 2048
DEPTH = 1
DEC_BATCH = 32
DEC_SEQ = 1
PAST_LEN = 16384
PAGE_SIZE = 128

D_ATTN = D_MODEL // 2
D_CONV = D_MODEL // 2
N_HEADS = 8
HEAD_DIM = D_ATTN // N_HEADS
N_KV_HEADS = N_HEADS
N_IDX_HEADS = 8
IDX_DIM = 64
TOPK_MAX = 256
CONV_W = 31
N_BUCKETS = 32
MAX_DISTANCE = 128
Q_BLOCK = 128
EPS = 1e-6
D_IN_PROJ = 2 * D_ATTN + 2 * N_KV_HEADS * HEAD_DIM + N_IDX_HEADS * IDX_DIM + IDX_DIM + N_IDX_HEADS + 3 * D_CONV

kernel_name = "hymba_dsa_conformer_decode_step"


def _rms_norm(x, g):
    xf = x.astype(jnp.float32)
    y = xf * lax.rsqrt(jnp.mean(xf * xf, axis=-1, keepdims=True) + EPS)
    return (y * g.astype(jnp.float32)).astype(x.dtype)


def _layer_norm(x, g, b):
    xf = x.astype(jnp.float32)
    mu = jnp.mean(xf, axis=-1, keepdims=True)
    var = jnp.mean(jnp.square(xf - mu), axis=-1, keepdims=True)
    y = (xf - mu) * lax.rsqrt(var + EPS) * g.astype(jnp.float32) + b.astype(jnp.float32)
    return y.astype(x.dtype)


def _rel_bucket(dist):
    n = jnp.maximum(dist, 0)
    max_exact = N_BUCKETS // 2
    nf = jnp.maximum(n, 1).astype(jnp.float32)
    large = max_exact + (jnp.log(nf / max_exact) / math.log(MAX_DISTANCE / max_exact)
                         * (N_BUCKETS - max_exact)).astype(jnp.int32)
    large = jnp.minimum(large, N_BUCKETS - 1)
    return jnp.where(n < max_exact, n, large)


def _mixer_inputs(x, norm_g, w_in, q_g, k_g):
    B, T, _ = x.shape
    z = jnp.einsum('btd,de->bte', _rms_norm(x, norm_g), w_in)
    sizes = (D_ATTN, N_KV_HEADS * HEAD_DIM, N_KV_HEADS * HEAD_DIM, D_ATTN,
             N_IDX_HEADS * IDX_DIM, IDX_DIM, N_IDX_HEADS, D_CONV, D_CONV, D_CONV)
    cuts = []
    acc = 0
    for s in sizes[:-1]:
        acc += s
        cuts.append(acc)
    zq, zk, zv, gate_a, zqi, ki, wi, glu_a, glu_b, gate_c = jnp.split(z, cuts, axis=-1)
    q = _rms_norm(zq.reshape(B, T, N_HEADS, HEAD_DIM), q_g)
    k = _rms_norm(zk.reshape(B, T, N_KV_HEADS, HEAD_DIM), k_g)
    v = zv.reshape(B, T, N_KV_HEADS, HEAD_DIM)
    qi = zqi.reshape(B, T, N_IDX_HEADS, IDX_DIM)
    u = glu_a * jax.nn.sigmoid(glu_b)
    return q, k, v, gate_a, qi, ki, wi, u, gate_c


def _index_select(qi, wi, ki, q_pos, k_pos, k_sel):
    s = jnp.einsum('thd,ld->thl', qi, ki).astype(jnp.float32) * (IDX_DIM ** -0.5)
    score = jnp.einsum('th,thl->tl', wi.astype(jnp.float32) * (N_IDX_HEADS ** -0.5), jax.nn.relu(s))
    score = jnp.where(k_pos[None, :] <= q_pos[:, None], score, -jnp.inf)
    _, idx = lax.top_k(score, k_sel)
    return idx.astype(jnp.int32)


def _attend_selected(q, k_sel, v_sel, q_pos, sel_pos, rel_bias):
    logits = jnp.einsum('thd,tkhd->thk', q, k_sel).astype(jnp.float32) * (HEAD_DIM ** -0.5)
    dist = q_pos[:, None] - sel_pos
    bias = rel_bias[_rel_bucket(dist)].astype(jnp.float32)
    logits = logits + jnp.transpose(bias, (0, 2, 1))
    logits = jnp.where((dist >= 0)[:, None, :], logits, -jnp.inf)
    p = jax.nn.softmax(logits, axis=-1).astype(v_sel.dtype)
    return jnp.einsum('thk,tkhd->thd', p, v_sel)


def _prompt_attention(q, k, v, qi, ki, wi, rel_bias):
    B, S = q.shape[0], q.shape[1]
    n_blk = S // Q_BLOCK
    k_sel = min(TOPK_MAX, S // 4)
    k_pos = jnp.arange(S, dtype=jnp.int32)

    def one(item):
        b = item // n_blk
        start = (item % n_blk) * Q_BLOCK
        qb = lax.dynamic_slice_in_dim(q[b], start, Q_BLOCK, 0)
        qib = lax.dynamic_slice_in_dim(qi[b], start, Q_BLOCK, 0)
        wib = lax.dynamic_slice_in_dim(wi[b], start, Q_BLOCK, 0)
        q_pos = start + jnp.arange(Q_BLOCK, dtype=jnp.int32)
        idx = _index_select(qib, wib, ki[b], q_pos, k_pos, k_sel)
        kb = k[b][idx]
        vb = v[b][idx]
        return _attend_selected(qb, kb, vb, q_pos, idx, rel_bias)

    out = lax.map(one, jnp.arange(B * n_blk, dtype=jnp.int32))
    return out.reshape(B, S, N_HEADS, HEAD_DIM)


def _sample_attention(q, k_new, v_new, qi, ki_new, wi, cache_k, cache_v, cache_ki, page_table, rel_bias):
    Bd, T = q.shape[0], q.shape[1]
    past = page_table.shape[1] * PAGE_SIZE
    L = past + T
    k_sel = min(TOPK_MAX, L // 4)
    ki_past = cache_ki[page_table].reshape(Bd, past, IDX_DIM)
    ki_all = jnp.concatenate([ki_past, ki_new.astype(ki_past.dtype)], axis=1)
    q_pos = past + jnp.arange(T, dtype=jnp.int32)
    k_pos = jnp.arange(L, dtype=jnp.int32)
    sel_fn = functools.partial(_index_select, k_sel=k_sel)
    idx = jax.vmap(sel_fn, in_axes=(0, 0, 0, None, None))(qi, wi, ki_all, q_pos, k_pos)
    b_ix = jnp.arange(Bd, dtype=jnp.int32)[:, None, None]
    idx_past = jnp.minimum(idx, past - 1)
    phys = page_table[b_ix, idx_past // PAGE_SIZE]
    off = idx_past % PAGE_SIZE
    idx_new = jnp.clip(idx - past, 0, T - 1)
    is_new = (idx >= past)[..., None, None]
    k_s = jnp.where(is_new, k_new[b_ix, idx_new].astype(cache_k.dtype), cache_k[phys, off])
    v_s = jnp.where(is_new, v_new[b_ix, idx_new].astype(cache_v.dtype), cache_v[phys, off])
    att = jax.vmap(_attend_selected, in_axes=(0, 0, 0, None, 0, None))
    return att(q, k_s, v_s, q_pos, idx, rel_bias)


def _conv_tail(u_ext, dw_w, dw_b, ln_g, ln_b):
    c = lax.conv_general_dilated(u_ext, dw_w[:, None, :], window_strides=(1,), padding='VALID',
                                 dimension_numbers=('NWC', 'WIO', 'NWC'),
                                 feature_group_count=D_CONV) + dw_b
    return jax.nn.silu(_layer_norm(c, ln_g, ln_b))


def _merge(x, attn, gate_a, conv, gate_c, w_out):
    B, T, _ = x.shape
    mix = jnp.concatenate([attn.reshape(B, T, D_ATTN) * jax.nn.silu(gate_a),
                           conv * jax.nn.silu(gate_c)], axis=-1)
    return x + jnp.einsum('bte,ed->btd', mix, w_out)


def setup_inputs(seed: int = 0) -> dict:
    key = jax.random.key(seed)
    ks = jax.random.split(key, 20)
    n_pages = PAST_LEN // PAGE_SIZE
    n_phys = (DEC_BATCH * n_pages * 5) // 4
    nrm = jax.random.normal
    f32 = jnp.float32
    x_prompt = nrm(ks[0], (BATCH, SEQ, D_MODEL), f32)
    x_sample = nrm(ks[1], (DEC_BATCH, DEC_SEQ, D_MODEL), f32)
    cache_k = nrm(ks[2], (DEPTH, n_phys, PAGE_SIZE, N_KV_HEADS, HEAD_DIM), f32)
    cache_v = nrm(ks[3], (DEPTH, n_phys, PAGE_SIZE, N_KV_HEADS, HEAD_DIM), f32)
    cache_k_idx = nrm(ks[4], (DEPTH, n_phys, PAGE_SIZE, IDX_DIM), f32)
    state_conv = 0.5 * nrm(ks[5], (DEPTH, DEC_BATCH, CONV_W - 1, D_CONV), f32)
    page_table = jax.random.permutation(ks[6], n_phys)[:DEC_BATCH * n_pages].reshape(DEC_BATCH, n_pages).astype(jnp.int32)
    rel_bias = 0.5 * nrm(ks[7], (N_BUCKETS, N_HEADS), f32)
    norm_g = 1.0 + 0.05 * nrm(ks[8], (DEPTH, D_MODEL), f32)
    w_in = nrm(ks[9], (DEPTH, D_MODEL, D_IN_PROJ), f32) * (D_MODEL ** -0.5)
    q_norm_g = 1.0 + 0.05 * nrm(ks[10], (DEPTH, HEAD_DIM), f32)
    k_norm_g = 1.0 + 0.05 * nrm(ks[11], (DEPTH, HEAD_DIM), f32)
    dw_w = nrm(ks[12], (DEPTH, CONV_W, D_CONV), f32) * (CONV_W ** -0.5)
    dw_b = 0.02 * nrm(ks[13], (DEPTH, D_CONV), f32)
    ln_g = 1.0 + 0.05 * nrm(ks[14], (DEPTH, D_CONV), f32)
    ln_b = 0.02 * nrm(ks[15], (DEPTH, D_CONV), f32)
    w_out = nrm(ks[16], (DEPTH, D_ATTN + D_CONV, D_MODEL), f32) * ((D_ATTN + D_CONV) ** -0.5)
    return {'x_prompt': x_prompt, 'x_sample': x_sample, 'cache_k': cache_k, 'cache_v': cache_v,
            'cache_k_idx': cache_k_idx, 'state_conv': state_conv, 'page_table': page_table,
            'rel_bias': rel_bias, 'norm_g': norm_g, 'w_in': w_in, 'q_norm_g': q_norm_g,
            'k_norm_g': k_norm_g, 'dw_w': dw_w, 'dw_b': dw_b, 'ln_g': ln_g, 'ln_b': ln_b,
            'w_out': w_out}


def reference(x_prompt, x_sample, cache_k, cache_v, cache_k_idx, state_conv, page_table,
              rel_bias, norm_g, w_in, q_norm_g, k_norm_g, dw_w, dw_b, ln_g, ln_b, w_out):
    hp = x_prompt
    hs = x_sample
    kp_l, vp_l, kip_l, cp_l = [], [], [], []
    ks_l, vs_l, kis_l, cs_l = [], [], [], []
    for l in range(DEPTH):
        q, k, v, gate_a, qi, ki, wi, u, gate_c = _mixer_inputs(hp, norm_g[l], w_in[l], q_norm_g[l], k_norm_g[l])
        attn = _prompt_attention(q, k, v, qi, ki, wi, rel_bias)
        u_ext = jnp.pad(u, ((0, 0), (CONV_W - 1, 0), (0, 0)))
        conv = _conv_tail(u_ext, dw_w[l], dw_b[l], ln_g[l], ln_b[l])
        hp = _merge(hp, attn, gate_a, conv, gate_c, w_out[l])
        kp_l.append(k)
        vp_l.append(v)
        kip_l.append(ki)
        cp_l.append(u_ext[:, -(CONV_W - 1):])
        q, k, v, gate_a, qi, ki, wi, u, gate_c = _mixer_inputs(hs, norm_g[l], w_in[l], q_norm_g[l], k_norm_g[l])
        attn = _sample_attention(q, k, v, qi, ki, wi, cache_k[l], cache_v[l], cache_k_idx[l], page_table, rel_bias)
        u_ext = jnp.concatenate([state_conv[l].astype(u.dtype), u], axis=1)
        conv = _conv_tail(u_ext, dw_w[l], dw_b[l], ln_g[l], ln_b[l])
        hs = _merge(hs, attn, gate_a, conv, gate_c, w_out[l])
        ks_l.append(k)
        vs_l.append(v)
        kis_l.append(ki)
        cs_l.append(u_ext[:, -(CONV_W - 1):])
    k_prompt = jnp.stack(kp_l)
    v_prompt = jnp.stack(vp_l)
    k_idx_prompt = jnp.stack(kip_l)
    conv_prompt = jnp.stack(cp_l)
    k_sample = jnp.stack(ks_l)
    v_sample = jnp.stack(vs_l)
    k_idx_sample = jnp.stack(kis_l)
    conv_sample = jnp.stack(cs_l)
    return (hp, hs, k_prompt, v_prompt, k_idx_prompt, conv_prompt, k_sample, v_sample, k_idx_sample, conv_sample)
```

```python
import functools
import math

import numpy as np
import jax
import jax.numpy as jnp
from jax import lax
from jax.experimental import pallas as pl
from jax.experimental.pallas import tpu as pltpu

N_HEADS = 8
HEAD_DIM = 64
N_IDX_HEADS = 8
IDX_DIM = 64
D_ATTN = N_HEADS * HEAD_DIM
D_CONV = 512
CONV_W = 31
TOPK_MAX = 256
N_BUCKETS = 32
MAX_DISTANCE = 128
EPS = 1e-6
NEG = -1e30
INT_MIN = -2 ** 31

LANES = 128
Q_TILE = 128
K_CHUNK = 256
VMEM_LIMIT = 56 * 1024 * 1024

f32 = jnp.float32
bf16 = jnp.bfloat16


def _bucket_thresholds():
    n = np.arange(0, 4 * MAX_DISTANCE)
    max_exact = N_BUCKETS // 2
    nf = np.maximum(n, 1).astype(np.float32)
    large = max_exact + (np.log(nf / np.float32(max_exact)) / np.float32(math.log(MAX_DISTANCE / max_exact))
                         * np.float32(N_BUCKETS - max_exact)).astype(np.int32)
    bucket = np.where(n < max_exact, n, np.minimum(large, N_BUCKETS - 1))
    return [int(n[bucket >= i].min()) for i in range(1, N_BUCKETS)]


BUCKET_THRESH = _bucket_thresholds()
assert BUCKET_THRESH[-1] <= Q_TILE + 1


def _silu(x):
    return x * jax.nn.sigmoid(x)


def _to_key(x):
    b = pltpu.bitcast(x, jnp.int32)
    return jnp.where(b < 0, b ^ jnp.int32(0x7FFFFFFF), b)


_C_Q, _C_K, _C_V, _C_GA, _C_QI, _C_KW, _C_A, _C_B, _C_GC, _C_END = (
    0, 512, 1024, 1536, 2048, 2560, 2688, 3200, 3712, 4224)


def _inproj_kernel(x_ref, g_ref, w_ref, qg_ref, kg_ref, bd_ref, *outs, prompt, tm):
    xf = x_ref[0]
    ms = jnp.mean(xf * xf, axis=-1, keepdims=True)
    xn = (xf * lax.rsqrt(ms + EPS) * g_ref[...]).astype(bf16)

    def proj(a, b):
        return jnp.dot(xn, w_ref[:, a:b], preferred_element_type=f32)

    def head_norm(z, gain_ref):
        msq = jnp.dot((z * z).astype(bf16), bd_ref[...], preferred_element_type=f32)
        return z * lax.rsqrt(msq + EPS) * gain_ref[...]

    q = head_norm(proj(_C_Q, _C_K), qg_ref)
    k = head_norm(proj(_C_K, _C_V), kg_ref)
    v = proj(_C_V, _C_GA)
    ga = _silu(proj(_C_GA, _C_QI))
    qi = proj(_C_QI, _C_KW)
    kw = proj(_C_KW, _C_A)
    u = proj(_C_A, _C_B) * jax.nn.sigmoid(proj(_C_B, _C_GC))
    gc = _silu(proj(_C_GC, _C_END))

    if prompt:
        q_o, k_o, kt_o, v_o, vh_o, ga_o, qi_o, kw_o, kit_o, u_o, gc_o = outs
        qs = (q * (HEAD_DIM ** -0.5)).astype(bf16)
        qis = (qi * (IDX_DIM ** -0.5)).astype(bf16)
        vb = v.astype(bf16)
        for h in range(N_HEADS):
            q_o[0, h] = qs[:, h * HEAD_DIM:(h + 1) * HEAD_DIM]
            qi_o[0, h] = qis[:, h * IDX_DIM:(h + 1) * IDX_DIM]
            vh_o[0, h] = vb[:, h * HEAD_DIM:(h + 1) * HEAD_DIM]
        for c in range(tm // K_CHUNK):
            rows = slice(c * K_CHUNK, (c + 1) * K_CHUNK)
            kt_o[0, c] = k[rows, :].T.astype(bf16)
            kit_o[0, c] = kw[rows, :].T[:IDX_DIM, :].astype(bf16)
    else:
        q_o, k_o, v_o, ga_o, qi_o, kw_o, u_o, gc_o = outs
        q_o[0] = q
        qi_o[0] = qi
    k_o[0] = k
    v_o[0] = v
    ga_o[0] = ga
    kw_o[0] = kw
    u_o[0] = u
    gc_o[0] = gc


def _inproj(x, norm_g, w_all, qg, kg, bd, *, prompt):
    B, S, D = x.shape
    tm = 512 if prompt else S
    assert S % tm == 0 and (not prompt or tm % K_CHUNK == 0)
    grid = (B, S // tm)
    row = lambda b, i: (b, i, 0)
    const2 = lambda b, i: (0, 0)
    hm = lambda b, i: (b, 0, i, 0)
    ck = lambda b, i: (b, i, 0, 0)
    nck = tm // K_CHUNK
    if prompt:
        out_shape = (
            jax.ShapeDtypeStruct((B, N_HEADS, S, HEAD_DIM), bf16),
            jax.ShapeDtypeStruct((B, S, D_ATTN), f32),
            jax.ShapeDtypeStruct((B, S // K_CHUNK, D_ATTN, K_CHUNK), bf16),
            jax.ShapeDtypeStruct((B, S, D_ATTN), f32),
            jax.ShapeDtypeStruct((B, N_HEADS, S, HEAD_DIM), bf16),
            jax.ShapeDtypeStruct((B, S, D_ATTN), f32),
            jax.ShapeDtypeStruct((B, N_IDX_HEADS, S, IDX_DIM), bf16),
            jax.ShapeDtypeStruct((B, S, LANES), f32),
            jax.ShapeDtypeStruct((B, S // K_CHUNK, IDX_DIM, K_CHUNK), bf16),
            jax.ShapeDtypeStruct((B, S, D_CONV), f32),
            jax.ShapeDtypeStruct((B, S, D_CONV), f32),
        )
        out_specs = (
            pl.BlockSpec((1, N_HEADS, tm, HEAD_DIM), hm),
            pl.BlockSpec((1, tm, D_ATTN), row),
            pl.BlockSpec((1, nck, D_ATTN, K_CHUNK), ck),
            pl.BlockSpec((1, tm, D_ATTN), row),
            pl.BlockSpec((1, N_HEADS, tm, HEAD_DIM), hm),
            pl.BlockSpec((1, tm, D_ATTN), row),
            pl.BlockSpec((1, N_IDX_HEADS, tm, IDX_DIM), hm),
            pl.BlockSpec((1, tm, LANES), row),
            pl.BlockSpec((1, nck, IDX_DIM, K_CHUNK), ck),
            pl.BlockSpec((1, tm, D_CONV), row),
            pl.BlockSpec((1, tm, D_CONV), row),
        )
    else:
        out_shape = (
            jax.ShapeDtypeStruct((B, S, D_ATTN), f32),
            jax.ShapeDtypeStruct((B, S, D_ATTN), f32),
            jax.ShapeDtypeStruct((B, S, D_ATTN), f32),
            jax.ShapeDtypeStruct((B, S, D_ATTN), f32),
            jax.ShapeDtypeStruct((B, S, D_ATTN), f32),
            jax.ShapeDtypeStruct((B, S, LANES), f32),
            jax.ShapeDtypeStruct((B, S, D_CONV), f32),
            jax.ShapeDtypeStruct((B, S, D_CONV), f32),
        )
        out_specs = tuple(pl.BlockSpec((1, tm, s.shape[-1]), row) for s in out_shape)
    return pl.pallas_call(
        functools.partial(_inproj_kernel, prompt=prompt, tm=tm),
        grid=grid,
        in_specs=[
            pl.BlockSpec((1, tm, D), row),
            pl.BlockSpec((1, D), const2),
            pl.BlockSpec((D, _C_END), const2),
            pl.BlockSpec((1, D_ATTN), const2),
            pl.BlockSpec((1, D_ATTN), const2),
            pl.BlockSpec((D_ATTN, D_ATTN), const2),
        ],
        out_specs=out_specs,
        out_shape=out_shape,
        compiler_params=pltpu.CompilerParams(
            dimension_semantics=("parallel", "parallel"), vmem_limit_bytes=VMEM_LIMIT),
        name="inproj_prompt" if prompt else "inproj_sample",
    )(x, norm_g, w_all, qg, kg, bd)


def _bias_table_kernel(rb_ref, out_ref):
    row = lax.broadcasted_iota(jnp.int32, (Q_TILE, K_CHUNK), 0)
    col = lax.broadcasted_iota(jnp.int32, (Q_TILE, K_CHUNK), 1)
    for par in range(2):
        for which in range(2):
            dist = Q_TILE * par + K_CHUNK * (1 - which) + row - col
            for h in range(N_HEADS):
                far = rb_ref[N_BUCKETS - 1, h]
                acc = jnp.full((Q_TILE, K_CHUNK), rb_ref[0, h] - far, f32)
                for i, t in enumerate(BUCKET_THRESH):
                    acc = jnp.where(dist >= t, rb_ref[i + 1, h] - far, acc)
                out_ref[par, which, h] = jnp.where(dist >= 0, acc, NEG)


def _bias_tables(rel_bias):
    return pl.pallas_call(
        _bias_table_kernel,
        in_specs=[pl.BlockSpec(memory_space=pltpu.SMEM)],
        out_specs=pl.BlockSpec(memory_space=pltpu.VMEM),
        out_shape=jax.ShapeDtypeStruct((2, 2, N_HEADS, Q_TILE, K_CHUNK), f32),
        name="bias_tables",
    )(rel_bias)


def _attn_kernel(q_ref, qi_ref, kw_ref, kt_ref, kit_ref, v_ref, ga_ref, bt_ref, out_ref,
                 sc_ref, key_ref, lg_ref, wb_ref, o_ref, *, ksel, idx_bits):
    j = pl.program_id(1)
    nck = j // 2 + 1
    par = j % 2
    shape = (Q_TILE, K_CHUNK)
    row_t = j * Q_TILE + lax.broadcasted_iota(jnp.int32, shape, 0)
    lane = lax.broadcasted_iota(jnp.int32, shape, 1)

    wi = kw_ref[0][:, IDX_DIM:IDX_DIM + N_IDX_HEADS] * (N_IDX_HEADS ** -0.5)
    for h in range(N_IDX_HEADS):
        wb_ref[h] = jnp.broadcast_to(wi[:, h:h + 1], shape)

    def score_body(c, carry):
        kic = kit_ref[0, c]
        acc = jnp.zeros(shape, f32)
        for h in range(N_IDX_HEADS):
            s = jnp.dot(qi_ref[0, h], kic, preferred_element_type=f32)
            acc = acc + jnp.maximum(s, 0.0) * wb_ref[h]
        sc_ref[c] = jnp.where(c * K_CHUNK + lane <= row_t, acc, -jnp.inf)
        return carry

    lax.fori_loop(0, nck, score_body, 0)

    def count(pred):
        def body(c, cnt):
            return cnt + jnp.where(pred(c), 1.0, 0.0)
        cnt = lax.fori_loop(0, nck, body, jnp.zeros(shape, f32))
        return jnp.sum(cnt, axis=1, keepdims=True)

    def write_mask(pred):
        def body(c, carry):
            sc_ref[c] = jnp.where(pred(c), 0.0, NEG)
            return carry
        lax.fori_loop(0, nck, body, 0)

    need_search = (j + 1) * Q_TILE > ksel

    @pl.when(jnp.logical_not(need_search))
    def _():
        write_mask(lambda c: c * K_CHUNK + lane < (j + 1) * Q_TILE)

    @pl.when(need_search)
    def _():
        def key_body(c, carry):
            key_ref[c] = _to_key(sc_ref[c])
            return carry
        lax.fori_loop(0, nck, key_body, 0)

        def bit_body(i, lo):
            cand = lo + jnp.left_shift(jnp.int32(1), 31 - i)
            candb = jnp.broadcast_to(cand, shape)
            tot = count(lambda c: key_ref[c] >= candb)
            return jnp.where(tot >= ksel, cand, lo)

        thr = lax.fori_loop(0, 32, bit_body, jnp.full((Q_TILE, 1), INT_MIN, jnp.int32))
        thrb = jnp.broadcast_to(thr, shape)
        n_ge = count(lambda c: key_ref[c] >= thrb)
        has_tie = jnp.max(n_ge) > ksel

        @pl.when(jnp.logical_not(has_tie))
        def _():
            write_mask(lambda c: key_ref[c] >= thrb)

        @pl.when(has_tie)
        def _():
            need = ksel - count(lambda c: key_ref[c] > thrb)

            def cut_body(i, x):
                cand = x + jnp.left_shift(jnp.int32(1), idx_bits - 1 - i)
                candb = jnp.broadcast_to(cand, shape)
                g = count(lambda c: jnp.where(key_ref[c] == thrb, c * K_CHUNK + lane, 2 ** 30) < candb)
                return jnp.where(g < need, cand, x)

            cut = lax.fori_loop(0, idx_bits, cut_body, jnp.zeros((Q_TILE, 1), jnp.int32))
            cutb = jnp.broadcast_to(cut, shape)

            def sel(c):
                kc = key_ref[c]
                eq_idx = jnp.where(kc == thrb, c * K_CHUNK + lane, 2 ** 30)
                return jnp.where(kc > thrb, 0, eq_idx) <= cutb
            write_mask(sel)

    for h in range(N_HEADS):
        qh = q_ref[0, h]
        hs = slice(h * HEAD_DIM, (h + 1) * HEAD_DIM)

        def logits(c, bias, m):
            s = jnp.dot(qh, kt_ref[0, c, hs, :], preferred_element_type=f32) + sc_ref[c]
            if bias is not None:
                s = s + bias
            lg_ref[c] = s
            return jnp.maximum(m, s)

        m = lax.fori_loop(0, jnp.maximum(nck - 2, 0), lambda c, m: logits(c, None, m),
                          jnp.full(shape, NEG, f32))
        m = lax.cond(nck >= 2, lambda m: logits(nck - 2, bt_ref[par, 0, h], m), lambda m: m, m)
        m = logits(nck - 1, bt_ref[par, 1, h], m)
        mb = jnp.broadcast_to(jnp.max(m, axis=1, keepdims=True), shape)

        def pv_body(c, carry):
            l, o = carry
            p = jnp.exp(lg_ref[c] - mb)
            vc = v_ref[0, h, pl.ds(pl.multiple_of(c * K_CHUNK, K_CHUNK), K_CHUNK), :]
            return l + p, o + jnp.dot(p.astype(bf16), vc, preferred_element_type=f32)

        l, o = lax.fori_loop(0, nck, pv_body,
                             (jnp.zeros(shape, f32), jnp.zeros((Q_TILE, HEAD_DIM), f32)))
        o_ref[:, hs] = o / jnp.sum(l, axis=1, keepdims=True)

    out_ref[0] = (o_ref[...] * ga_ref[0]).astype(bf16)


def _prompt_attention(q_hm, qi_hm, kw, kt, kit, v_hm, ga, btab):
    B, _, S, _ = q_hm.shape
    assert S % K_CHUNK == 0
    nq = S // Q_TILE
    nc = S // K_CHUNK
    ksel = min(TOPK_MAX, S // 4)
    qblk = lambda b, j: (b, 0, j, 0)
    perb4 = lambda b, j: (b, 0, 0, 0)
    return pl.pallas_call(
        functools.partial(_attn_kernel, ksel=ksel, idx_bits=max(1, (S - 1).bit_length())),
        grid=(B, nq),
        in_specs=[
            pl.BlockSpec((1, N_HEADS, Q_TILE, HEAD_DIM), qblk),
            pl.BlockSpec((1, N_IDX_HEADS, Q_TILE, IDX_DIM), qblk),
            pl.BlockSpec((1, Q_TILE, LANES), lambda b, j: (b, j, 0)),
            pl.BlockSpec((1, nc, D_ATTN, K_CHUNK), perb4),
            pl.BlockSpec((1, nc, IDX_DIM, K_CHUNK), perb4),
            pl.BlockSpec((1, N_HEADS, S, HEAD_DIM), perb4),
            pl.BlockSpec((1, Q_TILE, D_ATTN), lambda b, j: (b, j, 0)),
            pl.BlockSpec((2, 2, N_HEADS, Q_TILE, K_CHUNK), lambda b, j: (0, 0, 0, 0, 0)),
        ],
        out_specs=pl.BlockSpec((1, Q_TILE, D_ATTN), lambda b, j: (b, j, 0)),
        out_shape=jax.ShapeDtypeStruct((B, S, D_ATTN), bf16),
        scratch_shapes=[
            pltpu.VMEM((nc, Q_TILE, K_CHUNK), f32),
            pltpu.VMEM((nc, Q_TILE, K_CHUNK), jnp.int32),
            pltpu.VMEM((nc, Q_TILE, K_CHUNK), f32),
            pltpu.VMEM((N_IDX_HEADS, Q_TILE, K_CHUNK), f32),
            pltpu.VMEM((Q_TILE, D_ATTN), f32),
        ],
        compiler_params=pltpu.CompilerParams(
            dimension_semantics=("parallel", "arbitrary"), vmem_limit_bytes=VMEM_LIMIT),
        name="prompt_attention",
    )(q_hm, qi_hm, kw, kt, kit, v_hm, ga, btab)


CONV_HALO = 32
CONV_ROWS = 32


def _layer_norm_swish(c, lng_ref, lnb_ref):
    mu = jnp.mean(c, axis=-1, keepdims=True)
    d = c - mu
    var = jnp.mean(d * d, axis=-1, keepdims=True)
    return _silu(d * lax.rsqrt(var + EPS) * lng_ref[...] + lnb_ref[...])


def _merge_kernel(mixa_ref, ucur_ref, uprev_ref, gc_ref, x_ref, wo_ref, dw_ref, dwb_ref, lng_ref,
                  lnb_ref, y_ref, ext_ref, cg_ref, *, tq):
    i = pl.program_id(1)
    ext_ref[0:CONV_HALO] = jnp.where(i > 0, uprev_ref[0], 0.0)
    ext_ref[CONV_HALO:CONV_HALO + tq] = ucur_ref[0]
    first = CONV_HALO - (CONV_W - 1)
    for r in range(tq // CONV_ROWS):
        acc = jnp.broadcast_to(dwb_ref[...], (CONV_ROWS, D_CONV))
        for w in range(CONV_W):
            start = r * CONV_ROWS + first + w
            acc = acc + ext_ref[start:start + CONV_ROWS, :] * dw_ref[w:w + 1, :]
        rows = slice(r * CONV_ROWS, (r + 1) * CONV_ROWS)
        cg_ref[rows] = (_layer_norm_swish(acc, lng_ref, lnb_ref) * gc_ref[0, rows]).astype(bf16)
    y_ref[0] = (x_ref[0]
                + jnp.dot(mixa_ref[0], wo_ref[0:D_ATTN], preferred_element_type=f32)
                + jnp.dot(cg_ref[...], wo_ref[D_ATTN:D_ATTN + D_CONV], preferred_element_type=f32))


def _prompt_merge(mixa, u, gc, x, wo, dw, dwb, lng, lnb):
    B, S, D = x.shape
    tq = 256
    assert S % tq == 0 and tq % CONV_HALO == 0
    row = lambda b, i: (b, i, 0)
    const2 = lambda b, i: (0, 0)
    halo = lambda b, i: (b, jnp.maximum(i * (tq // CONV_HALO) - 1, 0), 0)
    return pl.pallas_call(
        functools.partial(_merge_kernel, tq=tq),
        grid=(B, S // tq),
        in_specs=[
            pl.BlockSpec((1, tq, D_ATTN), row),
            pl.BlockSpec((1, tq, D_CONV), row),
            pl.BlockSpec((1, CONV_HALO, D_CONV), halo),
            pl.BlockSpec((1, tq, D_CONV), row),
            pl.BlockSpec((1, tq, D), row),
            pl.BlockSpec((D_ATTN + D_CONV, D), const2),
            pl.BlockSpec((CONV_W, D_CONV), const2),
            pl.BlockSpec((1, D_CONV), const2),
            pl.BlockSpec((1, D_CONV), const2),
            pl.BlockSpec((1, D_CONV), const2),
        ],
        out_specs=pl.BlockSpec((1, tq, D), row),
        out_shape=jax.ShapeDtypeStruct((B, S, D), f32),
        scratch_shapes=[
            pltpu.VMEM((CONV_HALO + tq, D_CONV), f32),
            pltpu.VMEM((tq, D_CONV), bf16),
        ],
        compiler_params=pltpu.CompilerParams(
            dimension_semantics=("parallel", "arbitrary"), vmem_limit_bytes=VMEM_LIMIT),
        name="prompt_merge",
    )(mixa, u, u, gc, x, wo, dw, dwb, lng, lnb)


def _sample_score_kernel(pt_ref, qi_ref, w_ref, cki_ref, out_ref, buf_ref, sem_ref, *, npg):
    b = pl.program_id(0)
    nb = pl.num_programs(0)
    slot = b % 2

    def page_copy(bb, p, sl):
        return pltpu.make_async_copy(cki_ref.at[pt_ref[bb, p]], buf_ref.at[sl, p], sem_ref.at[sl])

    def start_all(bb, sl):
        def body(p, carry):
            page_copy(bb, p, sl).start()
            return carry
        lax.fori_loop(0, npg, body, 0)

    @pl.when(b == 0)
    def _():
        start_all(0, 0)

    @pl.when(b + 1 < nb)
    def _():
        start_all(b + 1, 1 - slot)

    def wait_body(p, carry):
        page_copy(b, p, slot).wait()
        return carry
    lax.fori_loop(0, npg, wait_body, 0)

    qs = (qi_ref[0] * (IDX_DIM ** -0.5)).astype(bf16)
    wcol = w_ref[0] * (N_IDX_HEADS ** -0.5)

    def body(p, carry):
        page = buf_ref[slot, p].astype(bf16)
        s = lax.dot_general(qs, page, (((1,), (1,)), ((), ())), preferred_element_type=f32)
        out_ref[0, pl.ds(p, 1), :] = jnp.sum(jnp.maximum(s, 0.0) * wcol, axis=0, keepdims=True)
        return carry
    lax.fori_loop(0, npg, body, 0)


def _sample_scores(page_table, qi3, wrow, cki):
    Bd, npg = page_table.shape
    page = cki.shape[1]
    assert page == LANES
    grid_spec = pltpu.PrefetchScalarGridSpec(
        num_scalar_prefetch=1,
        grid=(Bd,),
        in_specs=[
            pl.BlockSpec((1, N_IDX_HEADS, IDX_DIM), lambda b, pt: (b, 0, 0)),
            pl.BlockSpec((1, N_IDX_HEADS, LANES), lambda b, pt: (b, 0, 0)),
            pl.BlockSpec(memory_space=pl.ANY),
        ],
        out_specs=pl.BlockSpec((1, npg, page), lambda b, pt: (b, 0, 0)),
        scratch_shapes=[
            pltpu.VMEM((2, npg, page, IDX_DIM), f32),
            pltpu.SemaphoreType.DMA((2,)),
        ],
    )
    return pl.pallas_call(
        functools.partial(_sample_score_kernel, npg=npg),
        grid_spec=grid_spec,
        out_shape=jax.ShapeDtypeStruct((Bd, npg, page), f32),
        compiler_params=pltpu.CompilerParams(
            dimension_semantics=("arbitrary",), vmem_limit_bytes=VMEM_LIMIT),
        name="sample_scores",
    )(page_table, qi3, wrow, cki)


def _sample_select_kernel(sc_ref, qi_ref, kit_ref, kw_ref, out_ref, key_ref, msk_ref, *,
                          npg, ksel, past, idx_bits):
    Bd = sc_ref.shape[0]
    shape = (Bd, npg, LANES)

    qb = (qi_ref[...] * (IDX_DIM ** -0.5)).astype(bf16).astype(f32)
    kb = kit_ref[...].astype(bf16).astype(f32)
    prod = qb * kb
    grp = lax.broadcasted_iota(jnp.int32, prod.shape, 1) // IDX_DIM
    kwv = kw_ref[...]
    s_new = jnp.zeros((Bd, 1), f32)
    for h in range(N_IDX_HEADS):
        sh = jnp.sum(jnp.where(grp == h, prod, 0.0), axis=1, keepdims=True)
        s_new = s_new + jnp.maximum(sh, 0.0) * (kwv[:, IDX_DIM + h:IDX_DIM + h + 1] * (N_IDX_HEADS ** -0.5))
    key_new = _to_key(s_new).reshape(Bd, 1, 1)

    key_ref[...] = _to_key(sc_ref[...])
    pos = (lax.broadcasted_iota(jnp.int32, shape, 1) * LANES
           + lax.broadcasted_iota(jnp.int32, shape, 2))

    def count(pred, pred_new):
        x = jnp.where(pred, 1.0, 0.0)
        s = jnp.sum(jnp.sum(x, axis=1, keepdims=True), axis=2, keepdims=True)
        return s + jnp.where(pred_new, 1.0, 0.0)

    def bit_body(i, lo):
        cand = lo + jnp.left_shift(jnp.int32(1), 31 - i)
        tot = count(key_ref[...] >= cand, key_new >= cand)
        return jnp.where(tot >= ksel, cand, lo)

    thr = lax.fori_loop(0, 32, bit_body, jnp.full((Bd, 1, 1), INT_MIN, jnp.int32))
    keys = key_ref[...]
    need = ksel - count(keys > thr, key_new > thr)
    big = jnp.int32(2 ** 30)
    eq_pos = jnp.where(keys == thr, pos, big)
    eq_pos_new = jnp.where(key_new == thr, jnp.int32(past), big)

    def cut_body(i, x):
        cand = x + jnp.left_shift(jnp.int32(1), idx_bits - 1 - i)
        g = count(eq_pos < cand, eq_pos_new < cand)
        return jnp.where(g < need, cand, x)

    cut = lax.fori_loop(0, idx_bits, cut_body, jnp.zeros((Bd, 1, 1), jnp.int32))
    msk_ref[...] = jnp.where(jnp.where(keys > thr, 0, eq_pos) <= cut, 1.0, 0.0)

    lane_c = lax.broadcasted_iota(jnp.int32, (LANES, LANES), 1)
    row_c = lax.broadcasted_iota(jnp.int32, (LANES, LANES), 0)
    upper = jnp.where(row_c < lane_c, 1.0, 0.0).astype(bf16)
    ones_l = jnp.ones((LANES, LANES), bf16)
    rr = lax.broadcasted_iota(jnp.int32, (npg, npg), 0)
    cc = lax.broadcasted_iota(jnp.int32, (npg, npg), 1)
    lower_p = jnp.where(cc < rr, 1.0, 0.0).astype(bf16)
    upper_p = jnp.where(rr < cc, 1.0, 0.0).astype(bf16)
    ones_8 = jnp.ones((8, LANES), bf16)
    slot = lax.broadcasted_iota(jnp.int32, (ksel, 1), 0).astype(f32)
    lane_f = lax.broadcasted_iota(jnp.int32, (ksel, LANES), 1).astype(f32)
    page_f = lax.broadcasted_iota(jnp.int32, (ksel, npg), 1).astype(f32)

    def compact_body(b, carry):
        m = msk_ref[b]
        mb = m.astype(bf16)
        within = jnp.dot(mb, upper, preferred_element_type=f32)
        rowtot = jnp.dot(mb, ones_l, preferred_element_type=f32)
        rowoff = jnp.dot(lower_p, rowtot.astype(bf16), preferred_element_type=f32)
        rank = jnp.where(m > 0.0, within + rowoff, -1.0)
        rowtot_t = lax.dot_general(ones_8, mb, (((1,), (1,)), ((), ())), preferred_element_type=f32)
        rowoff_t = jnp.dot(rowtot_t.astype(bf16), upper_p, preferred_element_type=f32)
        lo_t = rowoff_t[0:1, :]
        hi_t = lo_t + rowtot_t[0:1, :]
        in_row = jnp.logical_and(lo_t <= slot, slot < hi_t)
        ranks = jnp.dot(jnp.where(in_row, 1.0, 0.0).astype(bf16), rank.astype(bf16),
                        preferred_element_type=f32)
        hit = ranks == slot
        lane_j = jnp.sum(jnp.where(hit, lane_f, 0.0), axis=1, keepdims=True)
        page_j = jnp.sum(jnp.where(in_row, page_f, 0.0), axis=1, keepdims=True)
        valid = jnp.sum(jnp.where(hit, 1.0, 0.0), axis=1, keepdims=True) > 0.0
        pos_j = (page_j * LANES + lane_j).astype(jnp.int32)
        out_ref[b] = jnp.where(valid, pos_j, -1)
        return carry

    lax.fori_loop(0, Bd, compact_body, 0)


def _sample_select(scores, qi, ki_tiled, kw, *, past):
    Bd, npg, _ = scores.shape
    ksel = min(TOPK_MAX, (past + 1) // 4)
    assert ksel <= 256 and npg % 8 == 0
    vm = pl.BlockSpec(memory_space=pltpu.VMEM)
    return pl.pallas_call(
        functools.partial(_sample_select_kernel, npg=npg, ksel=ksel, past=past,
                          idx_bits=past.bit_length()),
        in_specs=[vm, vm, vm, vm],
        out_specs=vm,
        out_shape=jax.ShapeDtypeStruct((Bd, ksel, 1), jnp.int32),
        scratch_shapes=[pltpu.VMEM((Bd, npg, LANES), jnp.int32), pltpu.VMEM((Bd, npg, LANES), f32)],
        compiler_params=pltpu.CompilerParams(vmem_limit_bytes=VMEM_LIMIT),
        name="sample_select",
    )(scores, qi, ki_tiled, kw)


def _sample_attn_kernel(sel_ref, pt_ref, q_ref, rbt_ref, knew_ref, vnew_ref, ck_ref, cv_ref, out_ref,
                        kbuf, vbuf, dist_ref, sem_ref, *, ksel, past, page):
    b = pl.program_id(0)
    nb = pl.num_programs(0)
    slot = b % 2

    def issue(bb, sl):
        def body(j, carry):
            pos = sel_ref[bb, j]

            @pl.when(pos >= 0)
            def _():
                phys = pt_ref[bb, pos // page]
                off = pos % page
                pltpu.make_async_copy(ck_ref.at[phys, off], kbuf.at[sl, j], sem_ref.at[0, sl]).start()
                pltpu.make_async_copy(cv_ref.at[phys, off], vbuf.at[sl, j], sem_ref.at[1, sl]).start()

            @pl.when(pos < 0)
            def _():
                pltpu.make_async_copy(knew_ref.at[bb], kbuf.at[sl, j], sem_ref.at[0, sl]).start()
                pltpu.make_async_copy(vnew_ref.at[bb], vbuf.at[sl, j], sem_ref.at[1, sl]).start()
            return carry
        lax.fori_loop(0, ksel, body, 0)

    @pl.when(b == 0)
    def _():
        issue(0, 0)

    @pl.when(b + 1 < nb)
    def _():
        issue(b + 1, 1 - slot)

    def dist_body(j, carry):
        pos = sel_ref[b, j]
        d = jnp.where(pos >= 0, past - pos, 0)
        dist_ref[j] = jnp.full((N_HEADS, HEAD_DIM), d, jnp.int32)
        return carry
    lax.fori_loop(0, ksel, dist_body, 0)

    def wait_body(j, carry):
        pltpu.make_async_copy(knew_ref.at[0], kbuf.at[slot, j], sem_ref.at[0, slot]).wait()
        pltpu.make_async_copy(vnew_ref.at[0], vbuf.at[slot, j], sem_ref.at[1, slot]).wait()
        return carry
    lax.fori_loop(0, ksel, wait_body, 0)

    q = q_ref[0] * (HEAD_DIM ** -0.5)
    prod = (kbuf[slot] * q[None]).reshape(ksel * N_HEADS, HEAD_DIM)
    hi = prod.astype(bf16)
    lo = (prod - hi.astype(f32)).astype(bf16)
    ones = jnp.ones((HEAD_DIM, HEAD_DIM), bf16)
    logit = (jnp.dot(hi, ones, preferred_element_type=f32)
             + jnp.dot(lo, ones, preferred_element_type=f32)).reshape(ksel, N_HEADS, HEAD_DIM)
    dist = dist_ref[...]
    bias = jnp.broadcast_to(rbt_ref[0][None], dist.shape)
    for i, t in enumerate(BUCKET_THRESH):
        bias = jnp.where(dist >= t, rbt_ref[i + 1][None], bias)
    logit = logit + bias
    m = jnp.max(logit, axis=0, keepdims=True)
    p = jnp.exp(logit - m)
    l = jnp.sum(p, axis=0, keepdims=True)
    o = jnp.sum(p * vbuf[slot], axis=0, keepdims=True)
    out_ref[...] = o / l


def _sample_attention(sel, page_table, q3, rbt, k_new3, v_new3, cache_k, cache_v, *, past):
    Bd, ksel = sel.shape
    page = cache_k.shape[1]
    tile = (N_HEADS, HEAD_DIM)
    grid_spec = pltpu.PrefetchScalarGridSpec(
        num_scalar_prefetch=2,
        grid=(Bd,),
        in_specs=[
            pl.BlockSpec((1,) + tile, lambda b, s, pt: (b, 0, 0)),
            pl.BlockSpec((N_BUCKETS,) + tile, lambda b, s, pt: (0, 0, 0)),
            pl.BlockSpec(memory_space=pl.ANY),
            pl.BlockSpec(memory_space=pl.ANY),
            pl.BlockSpec(memory_space=pl.ANY),
            pl.BlockSpec(memory_space=pl.ANY),
        ],
        out_specs=pl.BlockSpec((1,) + tile, lambda b, s, pt: (b, 0, 0)),
        scratch_shapes=[
            pltpu.VMEM((2, ksel) + tile, f32),
            pltpu.VMEM((2, ksel) + tile, f32),
            pltpu.VMEM((ksel,) + tile, jnp.int32),
            pltpu.SemaphoreType.DMA((2, 2)),
        ],
    )
    return pl.pallas_call(
        functools.partial(_sample_attn_kernel, ksel=ksel, past=past, page=page),
        grid_spec=grid_spec,
        out_shape=jax.ShapeDtypeStruct((Bd,) + tile, f32),
        compiler_params=pltpu.CompilerParams(
            dimension_semantics=("arbitrary",), vmem_limit_bytes=VMEM_LIMIT),
        name="sample_attention",
    )(sel, page_table, q3, rbt, k_new3, v_new3, cache_k, cache_v)


def _sample_merge_kernel(attn_ref, ga_ref, u_ref, gc_ref, st_ref, x_ref, wo_ref, dw_ref, dwb_ref,
                         lng_ref, lnb_ref, y_ref):
    acc = dwb_ref[...] + u_ref[...] * dw_ref[CONV_W - 1:CONV_W, :]
    for w in range(CONV_W - 1):
        acc = acc + st_ref[w] * dw_ref[w:w + 1, :]
    cg = (_layer_norm_swish(acc, lng_ref, lnb_ref) * gc_ref[...]).astype(bf16)
    mixa = (attn_ref[...] * ga_ref[...]).astype(bf16)
    y_ref[...] = (x_ref[...]
                  + jnp.dot(mixa, wo_ref[0:D_ATTN], preferred_element_type=f32)
                  + jnp.dot(cg, wo_ref[D_ATTN:D_ATTN + D_CONV], preferred_element_type=f32))


def _sample_merge(attn, ga, u, gc, state_t, x, wo, dw, dwb, lng, lnb):
    vm = pl.BlockSpec(memory_space=pltpu.VMEM)
    return pl.pallas_call(
        _sample_merge_kernel,
        in_specs=[vm] * 11,
        out_specs=vm,
        out_shape=jax.ShapeDtypeStruct(x.shape, f32),
        compiler_params=pltpu.CompilerParams(vmem_limit_bytes=VMEM_LIMIT),
        name="sample_merge",
    )(attn, ga, u, gc, state_t, x, wo, dw, dwb, lng, lnb)


def kernel(x_prompt, x_sample, cache_k, cache_v, cache_k_idx, state_conv, page_table, rel_bias, norm_g,
           w_in, q_norm_g, k_norm_g, dw_w, dw_b, ln_g, ln_b, w_out):
    depth = norm_g.shape[0]
    assert depth == 1, "single-layer step"
    B, S, D = x_prompt.shape
    Bd, T, _ = x_sample.shape
    assert T == 1
    page = cache_k.shape[2]
    past = page_table.shape[1] * page

    w = w_in[0]
    c_ki = 2 * D_ATTN + 2 * N_HEADS * HEAD_DIM + N_IDX_HEADS * IDX_DIM
    c_conv = c_ki + IDX_DIM + N_IDX_HEADS
    pad = jnp.zeros((D, LANES - IDX_DIM - N_IDX_HEADS), w.dtype)
    w_all = jnp.concatenate([w[:, :c_ki], w[:, c_ki:c_conv], pad, w[:, c_conv:]], axis=1).astype(bf16)
    assert w_all.shape[1] == _C_END
    wo = w_out[0].astype(bf16)
    g = norm_g[0][None]
    qg = jnp.tile(q_norm_g[0], N_HEADS)[None]
    kg = jnp.tile(k_norm_g[0], N_HEADS)[None]
    hid = np.arange(D_ATTN) // HEAD_DIM
    bd = jnp.asarray((hid[:, None] == hid[None, :]) / HEAD_DIM, dtype=bf16)
    dwb, lng, lnb = dw_b[0][None], ln_g[0][None], ln_b[0][None]

    (q_hm, k_p, kt, v_p, v_hm, ga_p, qi_hm, kw_p, kit, u_p, gc_p) = _inproj(
        x_prompt, g, w_all, qg, kg, bd, prompt=True)
    btab = _bias_tables(rel_bias)
    mixa = _prompt_attention(q_hm, qi_hm, kw_p, kt, kit, v_hm, ga_p, btab)
    y_prompt = _prompt_merge(mixa, u_p, gc_p, x_prompt, wo, dw_w[0], dwb, lng, lnb)

    xs = x_sample.reshape(1, Bd, D)
    q_s, k_s, v_s, ga_s, qi_s, kw_s, u_s, gc_s = (a[0] for a in _inproj(
        xs, g, w_all, qg, kg, bd, prompt=False))
    ki_s = kw_s[:, :IDX_DIM]
    wrow = jnp.broadcast_to(kw_s[:, IDX_DIM:IDX_DIM + N_IDX_HEADS, None], (Bd, N_IDX_HEADS, LANES))
    scores = _sample_scores(page_table, qi_s.reshape(Bd, N_IDX_HEADS, IDX_DIM), wrow, cache_k_idx[0])
    sel = _sample_select(scores, qi_s, jnp.tile(ki_s, (1, N_IDX_HEADS)), kw_s, past=past)
    rbt = jnp.broadcast_to(rel_bias[:, :, None], (N_BUCKETS, N_HEADS, HEAD_DIM))
    tile = (Bd, N_HEADS, HEAD_DIM)
    attn_s = _sample_attention(sel.reshape(Bd, -1), page_table, q_s.reshape(tile), rbt,
                               k_s.reshape(tile), v_s.reshape(tile), cache_k[0], cache_v[0], past=past)
    state_t = jnp.transpose(state_conv[0], (1, 0, 2))
    y_sample = _sample_merge(attn_s.reshape(Bd, D_ATTN), ga_s, u_s, gc_s, state_t, x_sample[:, 0], wo,
                             dw_w[0], dwb, lng, lnb)

    heads = (N_HEADS, HEAD_DIM)
    return (
        y_prompt,
        y_sample[:, None],
        k_p.reshape((1, B, S) + heads),
        v_p.reshape((1, B, S) + heads),
        kw_p[None, :, :, :IDX_DIM],
        u_p[None, :, S - (CONV_W - 1):],
        k_s.reshape((1, Bd, 1) + heads),
        v_s.reshape((1, Bd, 1) + heads),
        ki_s.reshape(1, Bd, 1, IDX_DIM),
        jnp.concatenate([state_conv[0][:, 1:], u_s[:, None]], axis=1)[None],
    )
```

```python
import functools
import math

import numpy as np
import jax
import jax.numpy as jnp
from jax import lax
from jax.experimental import pallas as pl
from jax.experimental.pallas import tpu as pltpu

N_HEADS = 8
HEAD_DIM = 64
N_IDX_HEADS = 8
IDX_DIM = 64
D_ATTN = N_HEADS * HEAD_DIM
D_CONV = 512
CONV_W = 31
TOPK_MAX = 256
N_BUCKETS = 32
MAX_DISTANCE = 128
EPS = 1e-6
NEG = -1e30
INT_MIN = -2 ** 31
LOG2E = math.log2(math.e)

LANES = 128
Q_TILE = 128
K_CHUNK = 256
VMEM_LIMIT = 56 * 1024 * 1024

f32 = jnp.float32
bf16 = jnp.bfloat16


def _bucket_thresholds():
    n = np.arange(0, 4 * MAX_DISTANCE)
    max_exact = N_BUCKETS // 2
    nf = np.maximum(n, 1).astype(np.float32)
    large = max_exact + (np.log(nf / np.float32(max_exact)) / np.float32(math.log(MAX_DISTANCE / max_exact))
                         * np.float32(N_BUCKETS - max_exact)).astype(np.int32)
    bucket = np.where(n < max_exact, n, np.minimum(large, N_BUCKETS - 1))
    return [int(n[bucket >= i].min()) for i in range(1, N_BUCKETS)]


BUCKET_THRESH = _bucket_thresholds()
assert BUCKET_THRESH[-1] <= Q_TILE + 1


def _silu(x):
    return x * jax.nn.sigmoid(x)


def _to_key(x):
    b = pltpu.bitcast(x, jnp.int32)
    return jnp.where(b < 0, b ^ jnp.int32(0x7FFFFFFF), b)


_C_Q, _C_K, _C_V, _C_GA, _C_QI, _C_KW, _C_A, _C_B, _C_GC, _C_END = (
    0, 512, 1024, 1536, 2048, 2560, 2688, 3200, 3712, 4224)


def _inproj_kernel(x_ref, g_ref, w_ref, qg_ref, kg_ref, bd_ref, *outs, prompt, tm):
    xf = x_ref[0]
    ms = jnp.mean(xf * xf, axis=-1, keepdims=True)
    xn = (xf * lax.rsqrt(ms + EPS) * g_ref[...]).astype(bf16)

    def proj(a, b):
        return jnp.dot(xn, w_ref[:, a:b], preferred_element_type=f32)

    def head_norm(z, gain_ref):
        msq = jnp.dot((z * z).astype(bf16), bd_ref[...], preferred_element_type=f32)
        return z * lax.rsqrt(msq + EPS) * gain_ref[...]

    q = head_norm(proj(_C_Q, _C_K), qg_ref)
    k = head_norm(proj(_C_K, _C_V), kg_ref)
    v = proj(_C_V, _C_GA)
    ga = _silu(proj(_C_GA, _C_QI))
    qi = proj(_C_QI, _C_KW)
    kw = proj(_C_KW, _C_A)
    u = proj(_C_A, _C_B) * jax.nn.sigmoid(proj(_C_B, _C_GC))
    gc = _silu(proj(_C_GC, _C_END))

    if prompt:
        q_o, k_o, kt_o, v_o, vh_o, ga_o, qi_o, kw_o, kit_o, u_o, gc_o = outs
        qs = (q * (HEAD_DIM ** -0.5 * LOG2E)).astype(bf16)
        qis = (qi * (IDX_DIM ** -0.5)).astype(bf16)
        vb = v.astype(bf16)
        ones_col = jnp.where(lax.broadcasted_iota(jnp.int32, (tm, LANES - HEAD_DIM), 1) == 0,
                             1.0, 0.0).astype(bf16)
        for h in range(N_HEADS):
            q_o[0, h] = qs[:, h * HEAD_DIM:(h + 1) * HEAD_DIM]
            qi_o[0, h] = qis[:, h * IDX_DIM:(h + 1) * IDX_DIM]
            vh_o[0, h] = jnp.concatenate([vb[:, h * HEAD_DIM:(h + 1) * HEAD_DIM], ones_col], axis=1)
        for c in range(tm // K_CHUNK):
            rows = slice(c * K_CHUNK, (c + 1) * K_CHUNK)
            kt_o[0, c] = k[rows, :].T.astype(bf16)
            kit_o[0, c] = kw[rows, :].T[:IDX_DIM, :].astype(bf16)
    else:
        q_o, k_o, v_o, ga_o, qi_o, kw_o, u_o, gc_o = outs
        q_o[0] = q
        qi_o[0] = qi
    k_o[0] = k
    v_o[0] = v
    ga_o[0] = ga
    kw_o[0] = kw
    u_o[0] = u
    gc_o[0] = gc


def _inproj(x, norm_g, w_all, qg, kg, bd, *, prompt):
    B, S, D = x.shape
    tm = 512 if prompt else S
    assert S % tm == 0 and (not prompt or tm % K_CHUNK == 0)
    grid = (B, S // tm)
    row = lambda b, i: (b, i, 0)
    const2 = lambda b, i: (0, 0)
    hm = lambda b, i: (b, 0, i, 0)
    ck = lambda b, i: (b, i, 0, 0)
    nck = tm // K_CHUNK
    if prompt:
        out_shape = (
            jax.ShapeDtypeStruct((B, N_HEADS, S, HEAD_DIM), bf16),
            jax.ShapeDtypeStruct((B, S, D_ATTN), f32),
            jax.ShapeDtypeStruct((B, S // K_CHUNK, D_ATTN, K_CHUNK), bf16),
            jax.ShapeDtypeStruct((B, S, D_ATTN), f32),
            jax.ShapeDtypeStruct((B, N_HEADS, S, LANES), bf16),
            jax.ShapeDtypeStruct((B, S, D_ATTN), f32),
            jax.ShapeDtypeStruct((B, N_IDX_HEADS, S, IDX_DIM), bf16),
            jax.ShapeDtypeStruct((B, S, LANES), f32),
            jax.ShapeDtypeStruct((B, S // K_CHUNK, IDX_DIM, K_CHUNK), bf16),
            jax.ShapeDtypeStruct((B, S, D_CONV), f32),
            jax.ShapeDtypeStruct((B, S, D_CONV), f32),
        )
        out_specs = (
            pl.BlockSpec((1, N_HEADS, tm, HEAD_DIM), hm),
            pl.BlockSpec((1, tm, D_ATTN), row),
            pl.BlockSpec((1, nck, D_ATTN, K_CHUNK), ck),
            pl.BlockSpec((1, tm, D_ATTN), row),
            pl.BlockSpec((1, N_HEADS, tm, LANES), hm),
            pl.BlockSpec((1, tm, D_ATTN), row),
            pl.BlockSpec((1, N_IDX_HEADS, tm, IDX_DIM), hm),
            pl.BlockSpec((1, tm, LANES), row),
            pl.BlockSpec((1, nck, IDX_DIM, K_CHUNK), ck),
            pl.BlockSpec((1, tm, D_CONV), row),
            pl.BlockSpec((1, tm, D_CONV), row),
        )
    else:
        out_shape = (
            jax.ShapeDtypeStruct((B, S, D_ATTN), f32),
            jax.ShapeDtypeStruct((B, S, D_ATTN), f32),
            jax.ShapeDtypeStruct((B, S, D_ATTN), f32),
            jax.ShapeDtypeStruct((B, S, D_ATTN), f32),
            jax.ShapeDtypeStruct((B, S, D_ATTN), f32),
            jax.ShapeDtypeStruct((B, S, LANES), f32),
            jax.ShapeDtypeStruct((B, S, D_CONV), f32),
            jax.ShapeDtypeStruct((B, S, D_CONV), f32),
        )
        out_specs = tuple(pl.BlockSpec((1, tm, s.shape[-1]), row) for s in out_shape)
    return pl.pallas_call(
        functools.partial(_inproj_kernel, prompt=prompt, tm=tm),
        grid=grid,
        in_specs=[
            pl.BlockSpec((1, tm, D), row),
            pl.BlockSpec((1, D), const2),
            pl.BlockSpec((D, _C_END), const2),
            pl.BlockSpec((1, D_ATTN), const2),
            pl.BlockSpec((1, D_ATTN), const2),
            pl.BlockSpec((D_ATTN, D_ATTN), const2),
        ],
        out_specs=out_specs,
        out_shape=out_shape,
        compiler_params=pltpu.CompilerParams(
            dimension_semantics=("parallel", "parallel"), vmem_limit_bytes=VMEM_LIMIT),
        name="inproj_prompt" if prompt else "inproj_sample",
    )(x, norm_g, w_all, qg, kg, bd)


def _bias_table_kernel(rb_ref, out_ref):
    row = lax.broadcasted_iota(jnp.int32, (Q_TILE, K_CHUNK), 0)
    col = lax.broadcasted_iota(jnp.int32, (Q_TILE, K_CHUNK), 1)
    for par in range(2):
        for which in range(2):
            dist = Q_TILE * par + K_CHUNK * (1 - which) + row - col
            for h in range(N_HEADS):
                far = rb_ref[N_BUCKETS - 1, h]
                acc = jnp.full((Q_TILE, K_CHUNK), (rb_ref[0, h] - far) * LOG2E, f32)
                for i, t in enumerate(BUCKET_THRESH):
                    acc = jnp.where(dist >= t, (rb_ref[i + 1, h] - far) * LOG2E, acc)
                out_ref[par, which, h] = jnp.where(dist >= 0, acc, NEG)


def _bias_tables(rel_bias):
    return pl.pallas_call(
        _bias_table_kernel,
        in_specs=[pl.BlockSpec(memory_space=pltpu.SMEM)],
        out_specs=pl.BlockSpec(memory_space=pltpu.VMEM),
        out_shape=jax.ShapeDtypeStruct((2, 2, N_HEADS, Q_TILE, K_CHUNK), f32),
        name="bias_tables",
    )(rel_bias)


def _attn_kernel(q_ref, qi_ref, kw_ref, kt_ref, kit_ref, v_ref, ga_ref, bt_ref, out_ref,
                 sc_ref, key_ref, lg_ref, wb_ref, mx_ref, o_ref, *, ksel, idx_bits):
    j = pl.program_id(1)
    nck = j // 2 + 1
    par = j % 2
    shape = (Q_TILE, K_CHUNK)
    row_t = j * Q_TILE + lax.broadcasted_iota(jnp.int32, shape, 0)
    lane = lax.broadcasted_iota(jnp.int32, shape, 1)

    wi = kw_ref[0][:, IDX_DIM:IDX_DIM + N_IDX_HEADS] * (N_IDX_HEADS ** -0.5)
    for h in range(N_IDX_HEADS):
        wb_ref[h] = jnp.broadcast_to(wi[:, h:h + 1], shape)

    def score_body(c, carry):
        kic = kit_ref[0, c]
        acc = jnp.zeros(shape, f32)
        for h in range(N_IDX_HEADS):
            s = jnp.dot(qi_ref[0, h], kic, preferred_element_type=f32)
            acc = acc + jnp.maximum(s, 0.0) * wb_ref[h]
        sc_ref[c] = jnp.where(c * K_CHUNK + lane <= row_t, acc, -jnp.inf)
        return carry

    lax.fori_loop(0, nck, score_body, 0)

    def count(pred):
        def body(c, cnt):
            return cnt + jnp.where(pred(c), 1.0, 0.0)
        cnt = lax.fori_loop(0, nck, body, jnp.zeros(shape, f32))
        return jnp.sum(cnt, axis=1, keepdims=True)

    def write_mask(pred):
        def body(c, carry):
            sc_ref[c] = jnp.where(pred(c), 0.0, NEG)
            return carry
        lax.fori_loop(0, nck, body, 0)

    need_search = (j + 1) * Q_TILE > ksel

    @pl.when(jnp.logical_not(need_search))
    def _():
        write_mask(lambda c: c * K_CHUNK + lane < (j + 1) * Q_TILE)

    @pl.when(need_search)
    def _():
        def key_body(c, carry):
            key_ref[c] = _to_key(sc_ref[c])
            return carry
        lax.fori_loop(0, nck, key_body, 0)

        def bit_body(i, lo):
            cand = lo + jnp.left_shift(jnp.int32(1), 31 - i)
            candb = jnp.broadcast_to(cand, shape)
            tot = count(lambda c: key_ref[c] >= candb)
            return jnp.where(tot >= ksel, cand, lo)

        thr = lax.fori_loop(0, 32, bit_body, jnp.full((Q_TILE, 1), INT_MIN, jnp.int32))
        thrb = jnp.broadcast_to(thr, shape)
        n_ge = count(lambda c: key_ref[c] >= thrb)
        has_tie = jnp.max(n_ge) > ksel

        @pl.when(jnp.logical_not(has_tie))
        def _():
            write_mask(lambda c: key_ref[c] >= thrb)

        @pl.when(has_tie)
        def _():
            need = ksel - count(lambda c: key_ref[c] > thrb)

            def cut_body(i, x):
                cand = x + jnp.left_shift(jnp.int32(1), idx_bits - 1 - i)
                candb = jnp.broadcast_to(cand, shape)
                g = count(lambda c: jnp.where(key_ref[c] == thrb, c * K_CHUNK + lane, 2 ** 30) < candb)
                return jnp.where(g < need, cand, x)

            cut = lax.fori_loop(0, idx_bits, cut_body, jnp.zeros((Q_TILE, 1), jnp.int32))
            cutb = jnp.broadcast_to(cut, shape)

            def sel(c):
                kc = key_ref[c]
                eq_idx = jnp.where(kc == thrb, c * K_CHUNK + lane, 2 ** 30)
                return jnp.where(kc > thrb, 0, eq_idx) <= cutb
            write_mask(sel)

    half = K_CHUNK // 2
    mx_ref[...] = jnp.full(mx_ref.shape, NEG, f32)

    def logits_chunk(c, which):
        mask = sc_ref[c]
        for h in range(N_HEADS):
            s = jnp.dot(q_ref[0, h], kt_ref[0, c, h * HEAD_DIM:(h + 1) * HEAD_DIM, :],
                        preferred_element_type=f32) + mask
            if which is not None:
                s = s + bt_ref[par, which, h]
            lg_ref[h, c] = s
            mx_ref[h] = jnp.maximum(mx_ref[h], jnp.maximum(s[:, :half], s[:, half:]))

    def far_body(c, carry):
        logits_chunk(c, None)
        return carry
    lax.fori_loop(0, jnp.maximum(nck - 2, 0), far_body, 0)

    @pl.when(nck >= 2)
    def _():
        logits_chunk(nck - 2, 0)
    logits_chunk(nck - 1, 1)

    for h in range(N_HEADS):
        mx_ref[h] = jnp.broadcast_to(jnp.max(mx_ref[h], axis=1, keepdims=True), (Q_TILE, half))
    o_ref[...] = jnp.zeros(o_ref.shape, f32)

    def pv_body(c, carry):
        start = pl.multiple_of(c * K_CHUNK, K_CHUNK)
        for h in range(N_HEADS):
            mb = mx_ref[h]
            p = jnp.exp2(lg_ref[h, c] - jnp.concatenate([mb, mb], axis=1)).astype(bf16)
            o_ref[h] += jnp.dot(p, v_ref[0, h, pl.ds(start, K_CHUNK), :], preferred_element_type=f32)
        return carry
    lax.fori_loop(0, nck, pv_body, 0)

    parts = []
    for h in range(N_HEADS):
        oh = o_ref[h]
        parts.append(oh[:, :HEAD_DIM] / oh[:, HEAD_DIM:HEAD_DIM + 1])
    out_ref[0] = (jnp.concatenate(parts, axis=1) * ga_ref[0]).astype(bf16)


def _prompt_attention(q_hm, qi_hm, kw, kt, kit, v_hm, ga, btab):
    B, _, S, _ = q_hm.shape
    assert S % K_CHUNK == 0
    nq = S // Q_TILE
    nc = S // K_CHUNK
    ksel = min(TOPK_MAX, S // 4)
    qblk = lambda b, j: (b, 0, j, 0)
    perb4 = lambda b, j: (b, 0, 0, 0)
    return pl.pallas_call(
        functools.partial(_attn_kernel, ksel=ksel, idx_bits=max(1, (S - 1).bit_length())),
        grid=(B, nq),
        in_specs=[
            pl.BlockSpec((1, N_HEADS, Q_TILE, HEAD_DIM), qblk),
            pl.BlockSpec((1, N_IDX_HEADS, Q_TILE, IDX_DIM), qblk),
            pl.BlockSpec((1, Q_TILE, LANES), lambda b, j: (b, j, 0)),
            pl.BlockSpec((1, nc, D_ATTN, K_CHUNK), perb4),
            pl.BlockSpec((1, nc, IDX_DIM, K_CHUNK), perb4),
            pl.BlockSpec((1, N_HEADS, S, LANES), perb4),
            pl.BlockSpec((1, Q_TILE, D_ATTN), lambda b, j: (b, j, 0)),
            pl.BlockSpec((2, 2, N_HEADS, Q_TILE, K_CHUNK), lambda b, j: (0, 0, 0, 0, 0)),
        ],
        out_specs=pl.BlockSpec((1, Q_TILE, D_ATTN), lambda b, j: (b, j, 0)),
        out_shape=jax.ShapeDtypeStruct((B, S, D_ATTN), bf16),
        scratch_shapes=[
            pltpu.VMEM((nc, Q_TILE, K_CHUNK), f32),
            pltpu.VMEM((nc, Q_TILE, K_CHUNK), jnp.int32),
            pltpu.VMEM((N_HEADS, nc, Q_TILE, K_CHUNK), f32),
            pltpu.VMEM((N_IDX_HEADS, Q_TILE, K_CHUNK), f32),
            pltpu.VMEM((N_HEADS, Q_TILE, K_CHUNK // 2), f32),
            pltpu.VMEM((N_HEADS, Q_TILE, LANES), f32),
        ],
        compiler_params=pltpu.CompilerParams(
            dimension_semantics=("parallel", "arbitrary"), vmem_limit_bytes=VMEM_LIMIT),
        name="prompt_attention",
    )(q_hm, qi_hm, kw, kt, kit, v_hm, ga, btab)


CONV_HALO = 32
CONV_ROWS = 32


def _layer_norm_swish(c, lng_ref, lnb_ref):
    mu = jnp.mean(c, axis=-1, keepdims=True)
    d = c - mu
    var = jnp.mean(d * d, axis=-1, keepdims=True)
    return _silu(d * lax.rsqrt(var + EPS) * lng_ref[...] + lnb_ref[...])


def _merge_kernel(mixa_ref, ucur_ref, uprev_ref, gc_ref, x_ref, wo_ref, dw_ref, dwb_ref, lng_ref,
                  lnb_ref, y_ref, ext_ref, cg_ref, *, tq):
    i = pl.program_id(1)
    ext_ref[0:CONV_HALO] = jnp.where(i > 0, uprev_ref[0], 0.0)
    ext_ref[CONV_HALO:CONV_HALO + tq] = ucur_ref[0]
    first = CONV_HALO - (CONV_W - 1)
    for r in range(tq // CONV_ROWS):
        acc = jnp.broadcast_to(dwb_ref[...], (CONV_ROWS, D_CONV))
        for w in range(CONV_W):
            start = r * CONV_ROWS + first + w
            acc = acc + ext_ref[start:start + CONV_ROWS, :] * dw_ref[w:w + 1, :]
        rows = slice(r * CONV_ROWS, (r + 1) * CONV_ROWS)
        cg_ref[rows] = (_layer_norm_swish(acc, lng_ref, lnb_ref) * gc_ref[0, rows]).astype(bf16)
    y_ref[0] = (x_ref[0]
                + jnp.dot(mixa_ref[0], wo_ref[0:D_ATTN], preferred_element_type=f32)
                + jnp.dot(cg_ref[...], wo_ref[D_ATTN:D_ATTN + D_CONV], preferred_element_type=f32))


def _prompt_merge(mixa, u, gc, x, wo, dw, dwb, lng, lnb):
    B, S, D = x.shape
    tq = 256
    assert S % tq == 0 and tq % CONV_HALO == 0
    row = lambda b, i: (b, i, 0)
    const2 = lambda b, i: (0, 0)
    halo = lambda b, i: (b, jnp.maximum(i * (tq // CONV_HALO) - 1, 0), 0)
    return pl.pallas_call(
        functools.partial(_merge_kernel, tq=tq),
        grid=(B, S // tq),
        in_specs=[
            pl.BlockSpec((1, tq, D_ATTN), row),
            pl.BlockSpec((1, tq, D_CONV), row),
            pl.BlockSpec((1, CONV_HALO, D_CONV), halo),
            pl.BlockSpec((1, tq, D_CONV), row),
            pl.BlockSpec((1, tq, D), row),
            pl.BlockSpec((D_ATTN + D_CONV, D), const2),
            pl.BlockSpec((CONV_W, D_CONV), const2),
            pl.BlockSpec((1, D_CONV), const2),
            pl.BlockSpec((1, D_CONV), const2),
            pl.BlockSpec((1, D_CONV), const2),
        ],
        out_specs=pl.BlockSpec((1, tq, D), row),
        out_shape=jax.ShapeDtypeStruct((B, S, D), f32),
        scratch_shapes=[
            pltpu.VMEM((CONV_HALO + tq, D_CONV), f32),
            pltpu.VMEM((tq, D_CONV), bf16),
        ],
        compiler_params=pltpu.CompilerParams(
            dimension_semantics=("parallel", "arbitrary"), vmem_limit_bytes=VMEM_LIMIT),
        name="prompt_merge",
    )(mixa, u, u, gc, x, wo, dw, dwb, lng, lnb)


SCORE_UNROLL = 4


def _sample_score_kernel(pt_ref, qi_ref, w_ref, cki_ref, out_ref, buf_ref, sem_ref, *, npg):
    b = pl.program_id(0)
    nb = pl.num_programs(0)
    slot = b % 2

    def page_copy(bb, p, sl):
        return pltpu.make_async_copy(cki_ref.at[pt_ref[bb, p]], buf_ref.at[sl, p], sem_ref.at[sl])

    def start_all(bb, sl):
        def body(p, carry):
            page_copy(bb, p, sl).start()
            return carry
        lax.fori_loop(0, npg, body, 0)

    @pl.when(b == 0)
    def _():
        start_all(0, 0)

    @pl.when(b + 1 < nb)
    def _():
        start_all(b + 1, 1 - slot)

    def wait_body(p, carry):
        page_copy(b, p, slot).wait()
        return carry
    lax.fori_loop(0, npg, wait_body, 0)

    qs = (qi_ref[0] * (IDX_DIM ** -0.5)).astype(bf16)
    wcol = w_ref[0] * (N_IDX_HEADS ** -0.5)

    def body(i, carry):
        for u in range(SCORE_UNROLL):
            p = i * SCORE_UNROLL + u
            page = buf_ref[slot, p].astype(bf16)
            s = jnp.dot(qs, page, preferred_element_type=f32)
            out_ref[0, pl.ds(p, 1), :] = jnp.sum(jnp.maximum(s, 0.0) * wcol, axis=0, keepdims=True)
        return carry
    lax.fori_loop(0, npg // SCORE_UNROLL, body, 0)


def _sample_scores(page_table, qi3, wrow, cki_t):
    Bd, npg = page_table.shape
    page = cki_t.shape[2]
    assert page == LANES and npg % SCORE_UNROLL == 0
    grid_spec = pltpu.PrefetchScalarGridSpec(
        num_scalar_prefetch=1,
        grid=(Bd,),
        in_specs=[
            pl.BlockSpec((1, N_IDX_HEADS, IDX_DIM), lambda b, pt: (b, 0, 0)),
            pl.BlockSpec((1, N_IDX_HEADS, LANES), lambda b, pt: (b, 0, 0)),
            pl.BlockSpec(memory_space=pl.ANY),
        ],
        out_specs=pl.BlockSpec((1, npg, page), lambda b, pt: (b, 0, 0)),
        scratch_shapes=[
            pltpu.VMEM((2, npg, IDX_DIM, page), f32),
            pltpu.SemaphoreType.DMA((2,)),
        ],
    )
    return pl.pallas_call(
        functools.partial(_sample_score_kernel, npg=npg),
        grid_spec=grid_spec,
        out_shape=jax.ShapeDtypeStruct((Bd, npg, page), f32),
        compiler_params=pltpu.CompilerParams(
            dimension_semantics=("arbitrary",), vmem_limit_bytes=VMEM_LIMIT),
        name="sample_scores",
    )(page_table, qi3, wrow, cki_t)


def _sample_select_kernel(sc_ref, qi_ref, kit_ref, kw_ref, mask_ref, mnew_ref, key_ref, *,
                          npg, ksel, past, idx_bits):
    Bd = sc_ref.shape[0]
    shape = (Bd, npg, LANES)

    qb = (qi_ref[...] * (IDX_DIM ** -0.5)).astype(bf16).astype(f32)
    kb = kit_ref[...].astype(bf16).astype(f32)
    prod = qb * kb
    grp = lax.broadcasted_iota(jnp.int32, prod.shape, 1) // IDX_DIM
    kwv = kw_ref[...]
    s_new = jnp.zeros((Bd, 1), f32)
    for h in range(N_IDX_HEADS):
        sh = jnp.sum(jnp.where(grp == h, prod, 0.0), axis=1, keepdims=True)
        s_new = s_new + jnp.maximum(sh, 0.0) * (kwv[:, IDX_DIM + h:IDX_DIM + h + 1] * (N_IDX_HEADS ** -0.5))
    key_new = _to_key(s_new).reshape(Bd, 1, 1)

    key_ref[...] = _to_key(sc_ref[...])
    pos = (lax.broadcasted_iota(jnp.int32, shape, 1) * LANES
           + lax.broadcasted_iota(jnp.int32, shape, 2))

    def count(pred, pred_new):
        x = jnp.where(pred, 1.0, 0.0)
        s = jnp.sum(jnp.sum(x, axis=1, keepdims=True), axis=2, keepdims=True)
        return s + jnp.where(pred_new, 1.0, 0.0)

    def bit_body(i, lo):
        cand = lo + jnp.left_shift(jnp.int32(1), 31 - i)
        tot = count(key_ref[...] >= cand, key_new >= cand)
        return jnp.where(tot >= ksel, cand, lo)

    thr = lax.fori_loop(0, 32, bit_body, jnp.full((Bd, 1, 1), INT_MIN, jnp.int32))
    keys = key_ref[...]
    need = ksel - count(keys > thr, key_new > thr)
    big = jnp.int32(2 ** 30)
    eq_pos = jnp.where(keys == thr, pos, big)
    eq_pos_new = jnp.where(key_new == thr, jnp.int32(past), big)

    def cut_body(i, x):
        cand = x + jnp.left_shift(jnp.int32(1), idx_bits - 1 - i)
        g = count(eq_pos < cand, eq_pos_new < cand)
        return jnp.where(g < need, cand, x)

    cut = lax.fori_loop(0, idx_bits, cut_body, jnp.zeros((Bd, 1, 1), jnp.int32))
    mask_ref[...] = jnp.where(jnp.where(keys > thr, 0, eq_pos) <= cut, 0.0, NEG)
    sel_new = jnp.where(key_new > thr, 0, eq_pos_new) <= cut
    mnew_ref[...] = jnp.broadcast_to(jnp.where(sel_new, 0.0, NEG), mnew_ref.shape)


def _sample_select(scores, qi, ki_tiled, kw, *, past):
    Bd, npg, _ = scores.shape
    ksel = min(TOPK_MAX, (past + 1) // 4)
    vm = pl.BlockSpec(memory_space=pltpu.VMEM)
    return pl.pallas_call(
        functools.partial(_sample_select_kernel, npg=npg, ksel=ksel, past=past,
                          idx_bits=past.bit_length()),
        in_specs=[vm, vm, vm, vm],
        out_specs=(vm, vm),
        out_shape=(jax.ShapeDtypeStruct((Bd, npg, LANES), f32),
                   jax.ShapeDtypeStruct((Bd, N_HEADS, LANES), f32)),
        scratch_shapes=[pltpu.VMEM((Bd, npg, LANES), jnp.int32)],
        compiler_params=pltpu.CompilerParams(vmem_limit_bytes=VMEM_LIMIT),
        name="sample_select",
    )(scores, qi, ki_tiled, kw)


PAGE_BUFS = 8


def _sample_attn_kernel(pt_ref, qbd_ref, knew_ref, vnew_ref, mask_ref, mnew_ref, rbt_ref, ck_ref, cv_ref,
                        out_ref, buf_ref, lg_ref, sem_ref, *, npg, past):
    b = pl.program_id(0)
    nb = pl.num_programs(0)
    per_b = 2 * npg
    total = nb * per_b

    def page_copy(g, src_ref):
        phys = pt_ref[g // per_b, g % npg]
        slot = g % PAGE_BUFS
        return pltpu.make_async_copy(src_ref.at[phys], buf_ref.at[slot], sem_ref.at[slot])

    def start(g):
        is_k = g % per_b < npg

        @pl.when(is_k)
        def _():
            page_copy(g, ck_ref).start()

        @pl.when(jnp.logical_not(is_k))
        def _():
            page_copy(g, cv_ref).start()

    def wait(g):
        page_copy(g, ck_ref).wait()

    @pl.when(b == 0)
    def _():
        for g in range(PAGE_BUFS):
            start(g)

    g0 = b * per_b
    qbd = (qbd_ref[0] * (HEAD_DIM ** -0.5 * LOG2E)).astype(bf16)
    blk = (lax.broadcasted_iota(jnp.int32, (N_HEADS, D_ATTN), 0)
           == lax.broadcasted_iota(jnp.int32, (N_HEADS, D_ATTN), 1) // HEAD_DIM)
    tok = lax.broadcasted_iota(jnp.int32, (N_HEADS, LANES), 1)

    def bias_of(dist):
        far = rbt_ref[N_BUCKETS - 1]
        acc = jnp.broadcast_to(rbt_ref[0] - far, dist.shape)
        for i, t in enumerate(BUCKET_THRESH):
            acc = jnp.where(dist >= t, rbt_ref[i + 1] - far, acc)
        return acc * LOG2E

    def k_body(p, m):
        g = g0 + p
        wait(g)
        kp = buf_ref[g % PAGE_BUFS].astype(bf16)
        s = jnp.dot(qbd, kp, preferred_element_type=f32) + mask_ref[0, pl.ds(p, 1), :]

        @pl.when(g + PAGE_BUFS < total)
        def _():
            start(g + PAGE_BUFS)
        lg_ref[p] = s
        return jnp.maximum(m, s)

    m = lax.fori_loop(0, npg, k_body, jnp.full((N_HEADS, LANES), NEG, f32))
    last = lg_ref[npg - 1] + bias_of(past - ((npg - 1) * LANES + tok))
    lg_ref[npg - 1] = last
    m = jnp.maximum(m, last)

    s_new = (jnp.sum(qbd_ref[0] * knew_ref[0], axis=1, keepdims=True) * (HEAD_DIM ** -0.5 * LOG2E)
             + bias_of(jnp.zeros((N_HEADS, LANES), jnp.int32))[:, 0:1] + mnew_ref[0][:, 0:1])
    m_row = jnp.maximum(jnp.max(m, axis=1, keepdims=True), s_new)
    mb = jnp.broadcast_to(m_row, (N_HEADS, LANES))

    def v_body(p, carry):
        acc, l = carry
        g = g0 + npg + p
        wait(g)
        vp = buf_ref[g % PAGE_BUFS].astype(bf16)
        pr = jnp.exp2(lg_ref[p] - mb)
        acc = acc + lax.dot_general(pr.astype(bf16), vp, (((1,), (1,)), ((), ())),
                                    preferred_element_type=f32)

        @pl.when(g + PAGE_BUFS < total)
        def _():
            start(g + PAGE_BUFS)
        return acc, l + pr

    acc, l = lax.fori_loop(0, npg, v_body, (jnp.zeros((N_HEADS, D_ATTN), f32),
                                            jnp.zeros((N_HEADS, LANES), f32)))
    p_new = jnp.exp2(s_new - m_row)
    l_tot = jnp.sum(l, axis=1, keepdims=True) + p_new
    num = jnp.where(blk, acc + p_new * vnew_ref[0], 0.0) / l_tot
    out_ref[0] = jnp.sum(num, axis=0, keepdims=True)


def _sample_attention(page_table, qbd, k_new_flat, v_new_flat, mask, mnew, rbt, ck_t, cv_t, *, past):
    Bd, npg = page_table.shape
    rows, page = ck_t.shape[1], ck_t.shape[2]
    assert rows == D_ATTN and page == LANES
    grid_spec = pltpu.PrefetchScalarGridSpec(
        num_scalar_prefetch=1,
        grid=(Bd,),
        in_specs=[
            pl.BlockSpec((1, N_HEADS, D_ATTN), lambda b, pt: (b, 0, 0)),
            pl.BlockSpec((1, 1, D_ATTN), lambda b, pt: (b, 0, 0)),
            pl.BlockSpec((1, 1, D_ATTN), lambda b, pt: (b, 0, 0)),
            pl.BlockSpec((1, npg, LANES), lambda b, pt: (b, 0, 0)),
            pl.BlockSpec((1, N_HEADS, LANES), lambda b, pt: (b, 0, 0)),
            pl.BlockSpec((N_BUCKETS, N_HEADS, LANES), lambda b, pt: (0, 0, 0)),
            pl.BlockSpec(memory_space=pl.ANY),
            pl.BlockSpec(memory_space=pl.ANY),
        ],
        out_specs=pl.BlockSpec((1, 1, D_ATTN), lambda b, pt: (b, 0, 0)),
        scratch_shapes=[
            pltpu.VMEM((PAGE_BUFS, rows, page), f32),
            pltpu.VMEM((npg, N_HEADS, LANES), f32),
            pltpu.SemaphoreType.DMA((PAGE_BUFS,)),
        ],
    )
    return pl.pallas_call(
        functools.partial(_sample_attn_kernel, npg=npg, past=past),
        grid_spec=grid_spec,
        out_shape=jax.ShapeDtypeStruct((Bd, 1, D_ATTN), f32),
        compiler_params=pltpu.CompilerParams(
            dimension_semantics=("arbitrary",), vmem_limit_bytes=VMEM_LIMIT),
        name="sample_attention",
    )(page_table, qbd, k_new_flat, v_new_flat, mask, mnew, rbt, ck_t, cv_t)


def _sample_merge_kernel(attn_ref, ga_ref, u_ref, gc_ref, st_ref, x_ref, wo_ref, dw_ref, dwb_ref,
                         lng_ref, lnb_ref, y_ref):
    acc = dwb_ref[...] + u_ref[...] * dw_ref[CONV_W - 1:CONV_W, :]
    for w in range(CONV_W - 1):
        acc = acc + st_ref[w] * dw_ref[w:w + 1, :]
    cg = (_layer_norm_swish(acc, lng_ref, lnb_ref) * gc_ref[...]).astype(bf16)
    mixa = (attn_ref[...] * ga_ref[...]).astype(bf16)
    y_ref[...] = (x_ref[...]
                  + jnp.dot(mixa, wo_ref[0:D_ATTN], preferred_element_type=f32)
                  + jnp.dot(cg, wo_ref[D_ATTN:D_ATTN + D_CONV], preferred_element_type=f32))


def _sample_merge(attn, ga, u, gc, state_t, x, wo, dw, dwb, lng, lnb):
    vm = pl.BlockSpec(memory_space=pltpu.VMEM)
    return pl.pallas_call(
        _sample_merge_kernel,
        in_specs=[vm] * 11,
        out_specs=vm,
        out_shape=jax.ShapeDtypeStruct(x.shape, f32),
        compiler_params=pltpu.CompilerParams(vmem_limit_bytes=VMEM_LIMIT),
        name="sample_merge",
    )(attn, ga, u, gc, state_t, x, wo, dw, dwb, lng, lnb)


def kernel(x_prompt, x_sample, cache_k, cache_v, cache_k_idx, state_conv, page_table, rel_bias, norm_g,
           w_in, q_norm_g, k_norm_g, dw_w, dw_b, ln_g, ln_b, w_out):
    depth = norm_g.shape[0]
    assert depth == 1, "single-layer step"
    B, S, D = x_prompt.shape
    Bd, T, _ = x_sample.shape
    assert T == 1
    n_phys, page = cache_k.shape[1], cache_k.shape[2]
    past = page_table.shape[1] * page

    w = w_in[0]
    c_ki = 2 * D_ATTN + 2 * N_HEADS * HEAD_DIM + N_IDX_HEADS * IDX_DIM
    c_conv = c_ki + IDX_DIM + N_IDX_HEADS
    pad = jnp.zeros((D, LANES - IDX_DIM - N_IDX_HEADS), w.dtype)
    w_all = jnp.concatenate([w[:, :c_ki], w[:, c_ki:c_conv], pad, w[:, c_conv:]], axis=1).astype(bf16)
    assert w_all.shape[1] == _C_END
    wo = w_out[0].astype(bf16)
    g = norm_g[0][None]
    qg = jnp.tile(q_norm_g[0], N_HEADS)[None]
    kg = jnp.tile(k_norm_g[0], N_HEADS)[None]
    hid = np.arange(D_ATTN) // HEAD_DIM
    bd = jnp.asarray((hid[:, None] == hid[None, :]) / HEAD_DIM, dtype=bf16)
    dwb, lng, lnb = dw_b[0][None], ln_g[0][None], ln_b[0][None]

    (q_hm, k_p, kt, v_p, v_hm, ga_p, qi_hm, kw_p, kit, u_p, gc_p) = _inproj(
        x_prompt, g, w_all, qg, kg, bd, prompt=True)
    btab = _bias_tables(rel_bias)
    mixa = _prompt_attention(q_hm, qi_hm, kw_p, kt, kit, v_hm, ga_p, btab)
    y_prompt = _prompt_merge(mixa, u_p, gc_p, x_prompt, wo, dw_w[0], dwb, lng, lnb)

    xs = x_sample.reshape(1, Bd, D)
    q_s, k_s, v_s, ga_s, qi_s, kw_s, u_s, gc_s = (a[0] for a in _inproj(
        xs, g, w_all, qg, kg, bd, prompt=False))
    ki_s = kw_s[:, :IDX_DIM]
    wrow = jnp.broadcast_to(kw_s[:, IDX_DIM:IDX_DIM + N_IDX_HEADS, None], (Bd, N_IDX_HEADS, LANES))
    cki_t = jnp.transpose(cache_k_idx[0], (0, 2, 1))
    ck_t = jnp.transpose(cache_k[0], (0, 2, 3, 1)).reshape(n_phys, D_ATTN, page)
    cv_t = jnp.transpose(cache_v[0], (0, 2, 3, 1)).reshape(n_phys, D_ATTN, page)
    scores = _sample_scores(page_table, qi_s.reshape(Bd, N_IDX_HEADS, IDX_DIM), wrow, cki_t)
    mask, mnew = _sample_select(scores, qi_s, jnp.tile(ki_s, (1, N_IDX_HEADS)), kw_s, past=past)
    rbt = jnp.broadcast_to(rel_bias[:, :, None], (N_BUCKETS, N_HEADS, LANES))
    eye = jnp.asarray(np.arange(N_HEADS)[:, None] == hid[None, :], dtype=f32)
    qbd = q_s[:, None, :] * eye[None]
    attn_s = _sample_attention(page_table, qbd, k_s[:, None, :], v_s[:, None, :],
                               mask, mnew, rbt, ck_t, cv_t, past=past)
    state_t = jnp.transpose(state_conv[0], (1, 0, 2))
    y_sample = _sample_merge(attn_s[:, 0], ga_s, u_s, gc_s, state_t, x_sample[:, 0], wo,
                             dw_w[0], dwb, lng, lnb)

    heads = (N_HEADS, HEAD_DIM)
    return (
        y_prompt,
        y_sample[:, None],
        k_p.reshape((1, B, S) + heads),
        v_p.reshape((1, B, S) + heads),
        kw_p[None, :, :, :IDX_DIM],
        u_p[None, :, S - (CONV_W - 1):],
        k_s.reshape((1, Bd, 1) + heads),
        v_s.reshape((1, Bd, 1) + heads),
        ki_s.reshape(1, Bd, 1, IDX_DIM),
        jnp.concatenate([state_conv[0][:, 1:], u_s[:, None]], axis=1)[None],
    )
```

```python
import functools
import math

import numpy as np
import jax
import jax.numpy as jnp
from jax import lax
from jax.experimental import pallas as pl
from jax.experimental.pallas import tpu as pltpu

N_HEADS = 8
HEAD_DIM = 64
N_IDX_HEADS = 8
IDX_DIM = 64
D_ATTN = N_HEADS * HEAD_DIM
D_CONV = 512
CONV_W = 31
TOPK_MAX = 256
N_BUCKETS = 32
MAX_DISTANCE = 128
EPS = 1e-6
NEG = -1e30
INT_MIN = -2 ** 31
LOG2E = math.log2(math.e)

LANES = 128
Q_TILE = 128
K_CHUNK = 256
PV_ROWS = 144
VMEM_LIMIT = 56 * 1024 * 1024

f32 = jnp.float32
bf16 = jnp.bfloat16


def _bucket_thresholds():
    n = np.arange(0, 4 * MAX_DISTANCE)
    max_exact = N_BUCKETS // 2
    nf = np.maximum(n, 1).astype(np.float32)
    large = max_exact + (np.log(nf / np.float32(max_exact)) / np.float32(math.log(MAX_DISTANCE / max_exact))
                         * np.float32(N_BUCKETS - max_exact)).astype(np.int32)
    bucket = np.where(n < max_exact, n, np.minimum(large, N_BUCKETS - 1))
    return [int(n[bucket >= i].min()) for i in range(1, N_BUCKETS)]


BUCKET_THRESH = _bucket_thresholds()
assert BUCKET_THRESH[-1] <= Q_TILE + 1


def _silu(x):
    return x * jax.nn.sigmoid(x)


def _to_key(x):
    b = pltpu.bitcast(x, jnp.int32)
    return jnp.where(b < 0, b ^ jnp.int32(0x7FFFFFFF), b)


_C_Q, _C_K, _C_V, _C_GA, _C_QI, _C_KW, _C_A, _C_B, _C_GC, _C_END = (
    0, 512, 1024, 1536, 2048, 2560, 2688, 3200, 3712, 4224)


def _inproj_kernel(x_ref, g_ref, w_ref, qg_ref, kg_ref, bd_ref, *outs, prompt, tm):
    xf = x_ref[0]
    ms = jnp.mean(xf * xf, axis=-1, keepdims=True)
    xn = (xf * lax.rsqrt(ms + EPS) * g_ref[...]).astype(bf16)

    def proj(a, b):
        return jnp.dot(xn, w_ref[:, a:b], preferred_element_type=f32)

    def head_norm(z, gain_ref):
        msq = jnp.dot((z * z).astype(bf16), bd_ref[...], preferred_element_type=f32)
        return z * lax.rsqrt(msq + EPS) * gain_ref[...]

    q = head_norm(proj(_C_Q, _C_K), qg_ref)
    k = head_norm(proj(_C_K, _C_V), kg_ref)
    v = proj(_C_V, _C_GA)
    ga = _silu(proj(_C_GA, _C_QI))
    qi = proj(_C_QI, _C_KW)
    kw = proj(_C_KW, _C_A)
    u = proj(_C_A, _C_B) * jax.nn.sigmoid(proj(_C_B, _C_GC))
    gc = _silu(proj(_C_GC, _C_END))

    if prompt:
        q_o, k_o, kb_o, v_o, vt_o, ga_o, qi_o, kw_o, wi_o, ki2_o, u_o, gc_o = outs
        q_o[0] = (q * (HEAD_DIM ** -0.5 * LOG2E)).T.astype(bf16)
        qi_o[0] = (qi * (IDX_DIM ** -0.5)).T.astype(bf16)
        wi_o[0] = kw.T[IDX_DIM:IDX_DIM + N_IDX_HEADS, :]
        kb_o[0] = k.astype(bf16)
        lane = lax.broadcasted_iota(jnp.int32, kw.shape, 1)
        ki2_o[0] = jnp.where(lane < IDX_DIM, kw, pltpu.roll(kw, IDX_DIM, 1)).astype(bf16)
        vt = v.T.astype(bf16)
        tail = jnp.where(lax.broadcasted_iota(jnp.int32, (PV_ROWS - 2 * HEAD_DIM, K_CHUNK), 0) < 8,
                         1.0, 0.0).astype(bf16)
        for c in range(tm // K_CHUNK):
            for i in range(N_HEADS // 2):
                blk = vt[i * 2 * HEAD_DIM:(i + 1) * 2 * HEAD_DIM, c * K_CHUNK:(c + 1) * K_CHUNK]
                vt_o[0, c, i] = jnp.concatenate([blk, tail], axis=0)
    else:
        q_o, k_o, v_o, ga_o, qi_o, kw_o, u_o, gc_o = outs
        q_o[0] = q
        qi_o[0] = qi
    k_o[0] = k
    v_o[0] = v
    ga_o[0] = ga
    kw_o[0] = kw
    u_o[0] = u
    gc_o[0] = gc


def _inproj(x, norm_g, w_all, qg, kg, bd, *, prompt):
    B, S, D = x.shape
    tm = 512 if prompt else S
    assert S % tm == 0 and (not prompt or tm % K_CHUNK == 0)
    grid = (B, S // tm)
    row = lambda b, i: (b, i, 0)
    const2 = lambda b, i: (0, 0)
    col = lambda b, i: (b, 0, i)
    nck = tm // K_CHUNK
    if prompt:
        out_shape = (
            jax.ShapeDtypeStruct((B, D_ATTN, S), bf16),
            jax.ShapeDtypeStruct((B, S, D_ATTN), f32),
            jax.ShapeDtypeStruct((B, S, D_ATTN), bf16),
            jax.ShapeDtypeStruct((B, S, D_ATTN), f32),
            jax.ShapeDtypeStruct((B, S // K_CHUNK, N_HEADS // 2, PV_ROWS, K_CHUNK), bf16),
            jax.ShapeDtypeStruct((B, S, D_ATTN), f32),
            jax.ShapeDtypeStruct((B, N_IDX_HEADS * IDX_DIM, S), bf16),
            jax.ShapeDtypeStruct((B, S, LANES), f32),
            jax.ShapeDtypeStruct((B, N_IDX_HEADS, S), f32),
            jax.ShapeDtypeStruct((B, S, LANES), bf16),
            jax.ShapeDtypeStruct((B, S, D_CONV), f32),
            jax.ShapeDtypeStruct((B, S, D_CONV), f32),
        )
        out_specs = (
            pl.BlockSpec((1, D_ATTN, tm), col),
            pl.BlockSpec((1, tm, D_ATTN), row),
            pl.BlockSpec((1, tm, D_ATTN), row),
            pl.BlockSpec((1, tm, D_ATTN), row),
            pl.BlockSpec((1, nck, N_HEADS // 2, PV_ROWS, K_CHUNK), lambda b, i: (b, i, 0, 0, 0)),
            pl.BlockSpec((1, tm, D_ATTN), row),
            pl.BlockSpec((1, N_IDX_HEADS * IDX_DIM, tm), col),
            pl.BlockSpec((1, tm, LANES), row),
            pl.BlockSpec((1, N_IDX_HEADS, tm), col),
            pl.BlockSpec((1, tm, LANES), row),
            pl.BlockSpec((1, tm, D_CONV), row),
            pl.BlockSpec((1, tm, D_CONV), row),
        )
    else:
        out_shape = (
            jax.ShapeDtypeStruct((B, S, D_ATTN), f32),
            jax.ShapeDtypeStruct((B, S, D_ATTN), f32),
            jax.ShapeDtypeStruct((B, S, D_ATTN), f32),
            jax.ShapeDtypeStruct((B, S, D_ATTN), f32),
            jax.ShapeDtypeStruct((B, S, D_ATTN), f32),
            jax.ShapeDtypeStruct((B, S, LANES), f32),
            jax.ShapeDtypeStruct((B, S, D_CONV), f32),
            jax.ShapeDtypeStruct((B, S, D_CONV), f32),
        )
        out_specs = tuple(pl.BlockSpec((1, tm, s.shape[-1]), row) for s in out_shape)
    return pl.pallas_call(
        functools.partial(_inproj_kernel, prompt=prompt, tm=tm),
        grid=grid,
        in_specs=[
            pl.BlockSpec((1, tm, D), row),
            pl.BlockSpec((1, D), const2),
            pl.BlockSpec((D, _C_END), const2),
            pl.BlockSpec((1, D_ATTN), const2),
            pl.BlockSpec((1, D_ATTN), const2),
            pl.BlockSpec((D_ATTN, D_ATTN), const2),
        ],
        out_specs=out_specs,
        out_shape=out_shape,
        compiler_params=pltpu.CompilerParams(
            dimension_semantics=("parallel", "parallel"), vmem_limit_bytes=VMEM_LIMIT),
        name="inproj_prompt" if prompt else "inproj_sample",
    )(x, norm_g, w_all, qg, kg, bd)


def _bias_table_kernel(rb_ref, out_ref):
    shape = (K_CHUNK, Q_TILE)
    krow = lax.broadcasted_iota(jnp.int32, shape, 0)
    qcol = lax.broadcasted_iota(jnp.int32, shape, 1)
    for par in range(2):
        for which in range(2):
            dist = Q_TILE * par + K_CHUNK * (1 - which) + qcol - krow
            for h in range(N_HEADS):
                far = rb_ref[N_BUCKETS - 1, h]
                acc = jnp.full(shape, (rb_ref[0, h] - far) * LOG2E, f32)
                for i, t in enumerate(BUCKET_THRESH):
                    acc = jnp.where(dist >= t, (rb_ref[i + 1, h] - far) * LOG2E, acc)
                out_ref[par, which, h // 2, :, (h % 2) * Q_TILE:(h % 2 + 1) * Q_TILE] = (
                    jnp.where(dist >= 0, acc, NEG))


def _bias_tables(rel_bias):
    return pl.pallas_call(
        _bias_table_kernel,
        in_specs=[pl.BlockSpec(memory_space=pltpu.SMEM)],
        out_specs=pl.BlockSpec(memory_space=pltpu.VMEM),
        out_shape=jax.ShapeDtypeStruct((2, 2, N_HEADS // 2, K_CHUNK, 2 * Q_TILE), f32),
        name="bias_tables",
    )(rel_bias)


N_PAIRS = N_HEADS // 2
SUB = 8


COUNT_ROWS = 32
MAX_ROWS = 16


def _fold(x, op, rows):
    return op(x.reshape(x.shape[0] // rows, rows, x.shape[1]), axis=0)


def _attn_kernel(qt_ref, qit_ref, wit_ref, k_ref, ki_ref, vt_ref, ga_ref, bt_ref, out_ref,
                 sc_ref, key_ref, lg_ref, qbd_ref, qibd_ref, o_ref, *, ksel, idx_bits):
    j = pl.program_id(1)
    nck = j // 2 + 1
    par = j % 2
    shape = (K_CHUNK, Q_TILE)
    key_pos = lax.broadcasted_iota(jnp.int32, shape, 0)
    q_pos = j * Q_TILE + lax.broadcasted_iota(jnp.int32, shape, 1)

    top = lax.broadcasted_iota(jnp.int32, (2 * HEAD_DIM, Q_TILE), 0) < HEAD_DIM
    for i in range(N_PAIRS):
        rows = slice(i * 2 * HEAD_DIM, (i + 1) * 2 * HEAD_DIM)
        for src, dst in ((qt_ref, qbd_ref), (qit_ref, qibd_ref)):
            x = src[0, rows, :]
            zero = jnp.zeros_like(x)
            dst[i] = jnp.concatenate([jnp.where(top, x, zero), jnp.where(top, zero, x)], axis=1)
    wi = wit_ref[0] * (N_IDX_HEADS ** -0.5)
    wpair = [jnp.concatenate([wi[2 * i:2 * i + 1], wi[2 * i + 1:2 * i + 2]], axis=1)
             for i in range(N_PAIRS)]

    def chunk_rows(c):
        return pl.ds(pl.multiple_of(c * K_CHUNK, K_CHUNK), K_CHUNK)

    def score_body(c, carry):
        ki2 = ki_ref[0, chunk_rows(c), :]
        acc = jnp.zeros(shape, f32)
        for i in range(N_PAIRS):
            r = jnp.maximum(jnp.dot(ki2, qibd_ref[i], preferred_element_type=f32), 0.0) * wpair[i]
            acc = acc + r[:, :Q_TILE] + r[:, Q_TILE:]
        sc_ref[c] = jnp.where(c * K_CHUNK + key_pos <= q_pos, acc, -jnp.inf)
        return carry

    lax.fori_loop(0, nck, score_body, 0)

    def count(pred):
        def body(c, cnt):
            return cnt + _fold(jnp.where(pred(c), 1.0, 0.0), jnp.sum, COUNT_ROWS)
        cnt = lax.fori_loop(0, nck, body, jnp.zeros((COUNT_ROWS, Q_TILE), f32))
        return jnp.sum(cnt, axis=0, keepdims=True)

    def write_mask(pred):
        def body(c, carry):
            sc_ref[c] = jnp.where(pred(c), 0.0, NEG)
            return carry
        lax.fori_loop(0, nck, body, 0)

    need_search = (j + 1) * Q_TILE > ksel

    @pl.when(jnp.logical_not(need_search))
    def _():
        write_mask(lambda c: c * K_CHUNK + key_pos < (j + 1) * Q_TILE)

    @pl.when(need_search)
    def _():
        def key_body(c, carry):
            key_ref[c] = _to_key(sc_ref[c])
            return carry
        lax.fori_loop(0, nck, key_body, 0)

        def bit_body(i, lo):
            cand = lo + jnp.left_shift(jnp.int32(1), 31 - i)
            candb = jnp.broadcast_to(cand, shape)
            tot = count(lambda c: key_ref[c] >= candb)
            return jnp.where(tot >= ksel, cand, lo)

        thr = lax.fori_loop(0, 32, bit_body, jnp.full((1, Q_TILE), INT_MIN, jnp.int32))
        thrb = jnp.broadcast_to(thr, shape)
        n_ge = count(lambda c: key_ref[c] >= thrb)
        has_tie = jnp.max(n_ge) > ksel

        @pl.when(jnp.logical_not(has_tie))
        def _():
            write_mask(lambda c: key_ref[c] >= thrb)

        @pl.when(has_tie)
        def _():
            need = ksel - count(lambda c: key_ref[c] > thrb)

            def cut_body(i, x):
                cand = x + jnp.left_shift(jnp.int32(1), idx_bits - 1 - i)
                candb = jnp.broadcast_to(cand, shape)
                g = count(lambda c: jnp.where(key_ref[c] == thrb, c * K_CHUNK + key_pos, 2 ** 30) < candb)
                return jnp.where(g < need, cand, x)

            cut = lax.fori_loop(0, idx_bits, cut_body, jnp.zeros((1, Q_TILE), jnp.int32))
            cutb = jnp.broadcast_to(cut, shape)

            def sel(c):
                kc = key_ref[c]
                eq_idx = jnp.where(kc == thrb, c * K_CHUNK + key_pos, 2 ** 30)
                return jnp.where(kc > thrb, 0, eq_idx) <= cutb
            write_mask(sel)

    def logits_chunk(c, which, ms):
        mask = sc_ref[c]
        mask2 = jnp.concatenate([mask, mask], axis=1)
        out = []
        for i in range(N_PAIRS):
            kc = k_ref[0, chunk_rows(c), i * 2 * HEAD_DIM:(i + 1) * 2 * HEAD_DIM]
            s = jnp.dot(kc, qbd_ref[i], preferred_element_type=f32) + mask2
            if which is not None:
                s = s + bt_ref[par, which, i]
            lg_ref[i, c] = s
            out.append(jnp.maximum(ms[i], _fold(s, jnp.max, MAX_ROWS)))
        return tuple(out)

    ms = tuple(jnp.full((MAX_ROWS, 2 * Q_TILE), NEG, f32) for _ in range(N_PAIRS))
    ms = lax.fori_loop(0, jnp.maximum(nck - 2, 0), lambda c, ms: logits_chunk(c, None, ms), ms)
    ms = lax.cond(nck >= 2, lambda ms: logits_chunk(nck - 2, 0, ms), lambda ms: ms, ms)
    ms = logits_chunk(nck - 1, 1, ms)
    mrow = [jnp.max(m, axis=0, keepdims=True) for m in ms]

    o_ref[...] = jnp.zeros(o_ref.shape, f32)

    def pv_body(c, carry):
        for i in range(N_PAIRS):
            p = jnp.exp2(lg_ref[i, c] - mrow[i]).astype(bf16)
            o_ref[i] += jnp.dot(vt_ref[0, c, i], p, preferred_element_type=f32)
        return carry
    lax.fori_loop(0, nck, pv_body, 0)

    pieces = []
    for i in range(N_PAIRS):
        o = o_ref[i]
        l = o[2 * HEAD_DIM:2 * HEAD_DIM + 1, :]
        pieces.append(o[0:HEAD_DIM, 0:Q_TILE] / l[:, 0:Q_TILE])
        pieces.append(o[HEAD_DIM:2 * HEAD_DIM, Q_TILE:] / l[:, Q_TILE:])
    out_ref[0] = (jnp.concatenate(pieces, axis=0).T * ga_ref[0]).astype(bf16)


def _prompt_attention(qt, qit, wit, kb, ki2, vt, ga, btab):
    B, S, _ = kb.shape
    assert S % K_CHUNK == 0
    nq = S // Q_TILE
    nc = S // K_CHUNK
    ksel = min(TOPK_MAX, S // 4)
    qcol = lambda b, j: (b, 0, j)
    perb3 = lambda b, j: (b, 0, 0)
    return pl.pallas_call(
        functools.partial(_attn_kernel, ksel=ksel, idx_bits=max(1, (S - 1).bit_length())),
        grid=(B, nq),
        in_specs=[
            pl.BlockSpec((1, D_ATTN, Q_TILE), qcol),
            pl.BlockSpec((1, N_IDX_HEADS * IDX_DIM, Q_TILE), qcol),
            pl.BlockSpec((1, N_IDX_HEADS, Q_TILE), qcol),
            pl.BlockSpec((1, S, D_ATTN), perb3),
            pl.BlockSpec((1, S, LANES), perb3),
            pl.BlockSpec((1, nc, N_PAIRS, PV_ROWS, K_CHUNK), lambda b, j: (b, 0, 0, 0, 0)),
            pl.BlockSpec((1, Q_TILE, D_ATTN), lambda b, j: (b, j, 0)),
            pl.BlockSpec((2, 2, N_PAIRS, K_CHUNK, 2 * Q_TILE), lambda b, j: (0, 0, 0, 0, 0)),
        ],
        out_specs=pl.BlockSpec((1, Q_TILE, D_ATTN), lambda b, j: (b, j, 0)),
        out_shape=jax.ShapeDtypeStruct((B, S, D_ATTN), bf16),
        scratch_shapes=[
            pltpu.VMEM((nc, K_CHUNK, Q_TILE), f32),
            pltpu.VMEM((nc, K_CHUNK, Q_TILE), jnp.int32),
            pltpu.VMEM((N_PAIRS, nc, K_CHUNK, 2 * Q_TILE), f32),
            pltpu.VMEM((N_PAIRS, 2 * HEAD_DIM, 2 * Q_TILE), bf16),
            pltpu.VMEM((N_PAIRS, 2 * IDX_DIM, 2 * Q_TILE), bf16),
            pltpu.VMEM((N_PAIRS, PV_ROWS, 2 * Q_TILE), f32),
        ],
        compiler_params=pltpu.CompilerParams(
            dimension_semantics=("parallel", "arbitrary"), vmem_limit_bytes=VMEM_LIMIT),
        name="prompt_attention",
    )(qt, qit, wit, kb, ki2, vt, ga, btab)


CONV_HALO = 32
CONV_ROWS = 32


def _layer_norm_swish(c, lng_ref, lnb_ref):
    mu = jnp.mean(c, axis=-1, keepdims=True)
    d = c - mu
    var = jnp.mean(d * d, axis=-1, keepdims=True)
    return _silu(d * lax.rsqrt(var + EPS) * lng_ref[...] + lnb_ref[...])


def _merge_kernel(mixa_ref, ucur_ref, uprev_ref, gc_ref, x_ref, wo_ref, dw_ref, dwb_ref, lng_ref,
                  lnb_ref, y_ref, ext_ref, sh_ref, cg_ref, *, tq):
    i = pl.program_id(1)
    n = CONV_HALO + tq
    ext_ref[0:CONV_HALO] = jnp.where(i > 0, uprev_ref[0], 0.0)
    ext_ref[CONV_HALO:n] = ucur_ref[0]
    ext_ref[n:n + SUB] = jnp.zeros((SUB, D_CONV), f32)
    for r in range(1, SUB):
        sh_ref[r - 1] = ext_ref[r:r + n, :]
    first = CONV_HALO - (CONV_W - 1)
    for c in range(tq // CONV_ROWS):
        acc = jnp.broadcast_to(dwb_ref[...], (CONV_ROWS, D_CONV))
        for w in range(CONV_W):
            a, r = divmod(first + w, SUB)
            start = c * CONV_ROWS + a * SUB
            src = ext_ref if r == 0 else sh_ref.at[r - 1]
            acc = acc + src[start:start + CONV_ROWS, :] * dw_ref[w:w + 1, :]
        rows = slice(c * CONV_ROWS, (c + 1) * CONV_ROWS)
        cg_ref[rows] = (_layer_norm_swish(acc, lng_ref, lnb_ref) * gc_ref[0, rows]).astype(bf16)
    y_ref[0] = (x_ref[0]
                + jnp.dot(mixa_ref[0], wo_ref[0:D_ATTN], preferred_element_type=f32)
                + jnp.dot(cg_ref[...], wo_ref[D_ATTN:D_ATTN + D_CONV], preferred_element_type=f32))


def _prompt_merge(mixa, u, gc, x, wo, dw, dwb, lng, lnb):
    B, S, D = x.shape
    tq = 256
    assert S % tq == 0 and tq % CONV_HALO == 0
    row = lambda b, i: (b, i, 0)
    const2 = lambda b, i: (0, 0)
    halo = lambda b, i: (b, jnp.maximum(i * (tq // CONV_HALO) - 1, 0), 0)
    return pl.pallas_call(
        functools.partial(_merge_kernel, tq=tq),
        grid=(B, S // tq),
        in_specs=[
            pl.BlockSpec((1, tq, D_ATTN), row),
            pl.BlockSpec((1, tq, D_CONV), row),
            pl.BlockSpec((1, CONV_HALO, D_CONV), halo),
            pl.BlockSpec((1, tq, D_CONV), row),
            pl.BlockSpec((1, tq, D), row),
            pl.BlockSpec((D_ATTN + D_CONV, D), const2),
            pl.BlockSpec((CONV_W, D_CONV), const2),
            pl.BlockSpec((1, D_CONV), const2),
            pl.BlockSpec((1, D_CONV), const2),
            pl.BlockSpec((1, D_CONV), const2),
        ],
        out_specs=pl.BlockSpec((1, tq, D), row),
        out_shape=jax.ShapeDtypeStruct((B, S, D), f32),
        scratch_shapes=[
            pltpu.VMEM((CONV_HALO + tq + SUB, D_CONV), f32),
            pltpu.VMEM((SUB - 1, CONV_HALO + tq, D_CONV), f32),
            pltpu.VMEM((tq, D_CONV), bf16),
        ],
        compiler_params=pltpu.CompilerParams(
            dimension_semantics=("parallel", "arbitrary"), vmem_limit_bytes=VMEM_LIMIT),
        name="prompt_merge",
    )(mixa, u, u, gc, x, wo, dw, dwb, lng, lnb)


SCORE_UNROLL = 8


def _sample_score_kernel(pt_ref, qi_ref, w_ref, cki_ref, out_ref, buf_ref, sem_ref, *, npg):
    b = pl.program_id(0)
    nb = pl.num_programs(0)
    slot = b % 2

    def page_copy(bb, p, sl):
        return pltpu.make_async_copy(cki_ref.at[pt_ref[bb, p]], buf_ref.at[sl, p], sem_ref.at[sl])

    def start_all(bb, sl):
        def body(p, carry):
            page_copy(bb, p, sl).start()
            return carry
        lax.fori_loop(0, npg, body, 0)

    @pl.when(b == 0)
    def _():
        start_all(0, 0)

    @pl.when(b + 1 < nb)
    def _():
        start_all(b + 1, 1 - slot)

    def wait_body(p, carry):
        page_copy(b, p, slot).wait()
        return carry
    lax.fori_loop(0, npg, wait_body, 0)

    qs = (qi_ref[0] * (IDX_DIM ** -0.5)).astype(bf16)
    wcol = w_ref[0] * (N_IDX_HEADS ** -0.5)
    wcol2 = jnp.concatenate([wcol, wcol], axis=1)

    def body(i, carry):
        for u in range(0, SCORE_UNROLL, 2):
            p = i * SCORE_UNROLL + u
            pages = jnp.concatenate([buf_ref[slot, p], buf_ref[slot, p + 1]], axis=1).astype(bf16)
            s = jnp.dot(qs, pages, preferred_element_type=f32)
            r = jnp.sum(jnp.maximum(s, 0.0) * wcol2, axis=0, keepdims=True)
            out_ref[0, pl.ds(p, 1), :] = r[:, :LANES]
            out_ref[0, pl.ds(p + 1, 1), :] = r[:, LANES:]
        return carry
    lax.fori_loop(0, npg // SCORE_UNROLL, body, 0)


def _sample_scores(page_table, qi3, wrow, cki_t):
    Bd, npg = page_table.shape
    page = cki_t.shape[2]
    assert page == LANES and npg % SCORE_UNROLL == 0
    grid_spec = pltpu.PrefetchScalarGridSpec(
        num_scalar_prefetch=1,
        grid=(Bd,),
        in_specs=[
            pl.BlockSpec((1, N_IDX_HEADS, IDX_DIM), lambda b, pt: (b, 0, 0)),
            pl.BlockSpec((1, N_IDX_HEADS, LANES), lambda b, pt: (b, 0, 0)),
            pl.BlockSpec(memory_space=pl.ANY),
        ],
        out_specs=pl.BlockSpec((1, npg, page), lambda b, pt: (b, 0, 0)),
        scratch_shapes=[
            pltpu.VMEM((2, npg, IDX_DIM, page), f32),
            pltpu.SemaphoreType.DMA((2,)),
        ],
    )
    return pl.pallas_call(
        functools.partial(_sample_score_kernel, npg=npg),
        grid_spec=grid_spec,
        out_shape=jax.ShapeDtypeStruct((Bd, npg, page), f32),
        compiler_params=pltpu.CompilerParams(
            dimension_semantics=("arbitrary",), vmem_limit_bytes=VMEM_LIMIT),
        name="sample_scores",
    )(page_table, qi3, wrow, cki_t)


def _sample_select_kernel(sc_ref, qi_ref, kit_ref, kw_ref, mask_ref, mnew_ref, key_ref, *,
                          npg, ksel, past, idx_bits):
    Bd = sc_ref.shape[0]
    shape = (Bd, npg, LANES)

    qb = (qi_ref[...] * (IDX_DIM ** -0.5)).astype(bf16).astype(f32)
    kb = kit_ref[...].astype(bf16).astype(f32)
    prod = qb * kb
    grp = lax.broadcasted_iota(jnp.int32, prod.shape, 1) // IDX_DIM
    kwv = kw_ref[...]
    s_new = jnp.zeros((Bd, 1), f32)
    for h in range(N_IDX_HEADS):
        sh = jnp.sum(jnp.where(grp == h, prod, 0.0), axis=1, keepdims=True)
        s_new = s_new + jnp.maximum(sh, 0.0) * (kwv[:, IDX_DIM + h:IDX_DIM + h + 1] * (N_IDX_HEADS ** -0.5))
    key_new = _to_key(s_new).reshape(Bd, 1, 1)

    key_ref[...] = _to_key(sc_ref[...])
    pos = (lax.broadcasted_iota(jnp.int32, shape, 1) * LANES
           + lax.broadcasted_iota(jnp.int32, shape, 2))

    def count(pred, pred_new):
        x = jnp.where(pred, 1.0, 0.0)
        s = jnp.sum(jnp.sum(x, axis=1, keepdims=True), axis=2, keepdims=True)
        return s + jnp.where(pred_new, 1.0, 0.0)

    def bit_body(i, lo):
        cand = lo + jnp.left_shift(jnp.int32(1), 31 - i)
        tot = count(key_ref[...] >= cand, key_new >= cand)
        return jnp.where(tot >= ksel, cand, lo)

    thr = lax.fori_loop(0, 32, bit_body, jnp.full((Bd, 1, 1), INT_MIN, jnp.int32))
    keys = key_ref[...]
    need = ksel - count(keys > thr, key_new > thr)
    big = jnp.int32(2 ** 30)
    eq_pos = jnp.where(keys == thr, pos, big)
    eq_pos_new = jnp.where(key_new == thr, jnp.int32(past), big)

    def cut_body(i, x):
        cand = x + jnp.left_shift(jnp.int32(1), idx_bits - 1 - i)
        g = count(eq_pos < cand, eq_pos_new < cand)
        return jnp.where(g < need, cand, x)

    cut = lax.fori_loop(0, idx_bits, cut_body, jnp.zeros((Bd, 1, 1), jnp.int32))
    mask_ref[...] = jnp.where(jnp.where(keys > thr, 0, eq_pos) <= cut, 0.0, NEG)
    sel_new = jnp.where(key_new > thr, 0, eq_pos_new) <= cut
    mnew_ref[...] = jnp.broadcast_to(jnp.where(sel_new, 0.0, NEG), mnew_ref.shape)


def _sample_select(scores, qi, ki_tiled, kw, *, past):
    Bd, npg, _ = scores.shape
    ksel = min(TOPK_MAX, (past + 1) // 4)
    vm = pl.BlockSpec(memory_space=pltpu.VMEM)
    return pl.pallas_call(
        functools.partial(_sample_select_kernel, npg=npg, ksel=ksel, past=past,
                          idx_bits=past.bit_length()),
        in_specs=[vm, vm, vm, vm],
        out_specs=(vm, vm),
        out_shape=(jax.ShapeDtypeStruct((Bd, npg, LANES), f32),
                   jax.ShapeDtypeStruct((Bd, N_HEADS, LANES), f32)),
        scratch_shapes=[pltpu.VMEM((Bd, npg, LANES), jnp.int32)],
        compiler_params=pltpu.CompilerParams(vmem_limit_bytes=VMEM_LIMIT),
        name="sample_select",
    )(scores, qi, ki_tiled, kw)


PAGE_BUFS = 8
PAGE_GROUP = 4


def _sample_attn_kernel(pt_ref, qbd_ref, knew_ref, vnew_ref, mask_ref, mnew_ref, rbt_ref, ck_ref, cv_ref,
                        out_ref, buf_ref, lg_ref, sem_ref, *, npg, past):
    b = pl.program_id(0)
    nb = pl.num_programs(0)
    per_b = 2 * npg
    total = nb * per_b

    def page_copy(g, src_ref):
        phys = pt_ref[g // per_b, g % npg]
        slot = g % PAGE_BUFS
        return pltpu.make_async_copy(src_ref.at[phys], buf_ref.at[slot], sem_ref.at[slot])

    def start(g):
        is_k = g % per_b < npg

        @pl.when(is_k)
        def _():
            page_copy(g, ck_ref).start()

        @pl.when(jnp.logical_not(is_k))
        def _():
            page_copy(g, cv_ref).start()

    def wait(g):
        page_copy(g, ck_ref).wait()

    @pl.when(b == 0)
    def _():
        for g in range(PAGE_BUFS):
            start(g)

    g0 = b * per_b
    qbd = (qbd_ref[0] * (HEAD_DIM ** -0.5 * LOG2E)).astype(bf16)
    blk = (lax.broadcasted_iota(jnp.int32, (N_HEADS, D_ATTN), 0)
           == lax.broadcasted_iota(jnp.int32, (N_HEADS, D_ATTN), 1) // HEAD_DIM)
    tok = lax.broadcasted_iota(jnp.int32, (N_HEADS, LANES), 1)

    def bias_of(dist):
        far = rbt_ref[N_BUCKETS - 1]
        acc = jnp.broadcast_to(rbt_ref[0] - far, dist.shape)
        for i, t in enumerate(BUCKET_THRESH):
            acc = jnp.where(dist >= t, rbt_ref[i + 1] - far, acc)
        return acc * LOG2E

    def refill(gs):
        for g in gs:
            @pl.when(g + PAGE_BUFS < total)
            def _():
                start(g + PAGE_BUFS)

    def k_body(it, m):
        gs = [g0 + it * PAGE_GROUP + u for u in range(PAGE_GROUP)]
        for g in gs:
            wait(g)
        for u, g in enumerate(gs):
            p = it * PAGE_GROUP + u
            kp = buf_ref[g % PAGE_BUFS].astype(bf16)
            s = jnp.dot(qbd, kp, preferred_element_type=f32) + mask_ref[0, pl.ds(p, 1), :]
            lg_ref[p] = s
            m = jnp.maximum(m, s)
        refill(gs)
        return m

    m = lax.fori_loop(0, npg // PAGE_GROUP, k_body, jnp.full((N_HEADS, LANES), NEG, f32))
    last = lg_ref[npg - 1] + bias_of(past - ((npg - 1) * LANES + tok))
    lg_ref[npg - 1] = last
    m = jnp.maximum(m, last)

    s_new = (jnp.sum(qbd_ref[0] * knew_ref[0], axis=1, keepdims=True) * (HEAD_DIM ** -0.5 * LOG2E)
             + bias_of(jnp.zeros((N_HEADS, LANES), jnp.int32))[:, 0:1] + mnew_ref[0][:, 0:1])
    m_row = jnp.maximum(jnp.max(m, axis=1, keepdims=True), s_new)
    mb = jnp.broadcast_to(m_row, (N_HEADS, LANES))

    def v_body(it, carry):
        acc, l = carry
        gs = [g0 + npg + it * PAGE_GROUP + u for u in range(PAGE_GROUP)]
        for g in gs:
            wait(g)
        for u, g in enumerate(gs):
            vp = buf_ref[g % PAGE_BUFS].astype(bf16)
            pr = jnp.exp2(lg_ref[it * PAGE_GROUP + u] - mb)
            acc = acc + lax.dot_general(pr.astype(bf16), vp, (((1,), (1,)), ((), ())),
                                        preferred_element_type=f32)
            l = l + pr
        refill(gs)
        return acc, l

    acc, l = lax.fori_loop(0, npg // PAGE_GROUP, v_body, (jnp.zeros((N_HEADS, D_ATTN), f32),
                                                          jnp.zeros((N_HEADS, LANES), f32)))
    p_new = jnp.exp2(s_new - m_row)
    l_tot = jnp.sum(l, axis=1, keepdims=True) + p_new
    num = jnp.where(blk, acc + p_new * vnew_ref[0], 0.0) / l_tot
    out_ref[0] = jnp.sum(num, axis=0, keepdims=True)


def _sample_attention(page_table, qbd, k_new_flat, v_new_flat, mask, mnew, rbt, ck_t, cv_t, *, past):
    Bd, npg = page_table.shape
    rows, page = ck_t.shape[1], ck_t.shape[2]
    assert rows == D_ATTN and page == LANES
    assert npg % PAGE_GROUP == 0 and PAGE_BUFS % PAGE_GROUP == 0 and 2 * npg >= PAGE_BUFS
    grid_spec = pltpu.PrefetchScalarGridSpec(
        num_scalar_prefetch=1,
        grid=(Bd,),
        in_specs=[
            pl.BlockSpec((1, N_HEADS, D_ATTN), lambda b, pt: (b, 0, 0)),
            pl.BlockSpec((1, 1, D_ATTN), lambda b, pt: (b, 0, 0)),
            pl.BlockSpec((1, 1, D_ATTN), lambda b, pt: (b, 0, 0)),
            pl.BlockSpec((1, npg, LANES), lambda b, pt: (b, 0, 0)),
            pl.BlockSpec((1, N_HEADS, LANES), lambda b, pt: (b, 0, 0)),
            pl.BlockSpec((N_BUCKETS, N_HEADS, LANES), lambda b, pt: (0, 0, 0)),
            pl.BlockSpec(memory_space=pl.ANY),
            pl.BlockSpec(memory_space=pl.ANY),
        ],
        out_specs=pl.BlockSpec((1, 1, D_ATTN), lambda b, pt: (b, 0, 0)),
        scratch_shapes=[
            pltpu.VMEM((PAGE_BUFS, rows, page), f32),
            pltpu.VMEM((npg, N_HEADS, LANES), f32),
            pltpu.SemaphoreType.DMA((PAGE_BUFS,)),
        ],
    )
    return pl.pallas_call(
        functools.partial(_sample_attn_kernel, npg=npg, past=past),
        grid_spec=grid_spec,
        out_shape=jax.ShapeDtypeStruct((Bd, 1, D_ATTN), f32),
        compiler_params=pltpu.CompilerParams(
            dimension_semantics=("arbitrary",), vmem_limit_bytes=VMEM_LIMIT),
        name="sample_attention",
    )(page_table, qbd, k_new_flat, v_new_flat, mask, mnew, rbt, ck_t, cv_t)


def _sample_merge_kernel(attn_ref, ga_ref, u_ref, gc_ref, st_ref, x_ref, wo_ref, dw_ref, dwb_ref,
                         lng_ref, lnb_ref, y_ref):
    acc = dwb_ref[...] + u_ref[...] * dw_ref[CONV_W - 1:CONV_W, :]
    for w in range(CONV_W - 1):
        acc = acc + st_ref[w] * dw_ref[w:w + 1, :]
    cg = (_layer_norm_swish(acc, lng_ref, lnb_ref) * gc_ref[...]).astype(bf16)
    mixa = (attn_ref[...] * ga_ref[...]).astype(bf16)
    y_ref[...] = (x_ref[...]
                  + jnp.dot(mixa, wo_ref[0:D_ATTN], preferred_element_type=f32)
                  + jnp.dot(cg, wo_ref[D_ATTN:D_ATTN + D_CONV], preferred_element_type=f32))


def _sample_merge(attn, ga, u, gc, state_t, x, wo, dw, dwb, lng, lnb):
    vm = pl.BlockSpec(memory_space=pltpu.VMEM)
    return pl.pallas_call(
        _sample_merge_kernel,
        in_specs=[vm] * 11,
        out_specs=vm,
        out_shape=jax.ShapeDtypeStruct(x.shape, f32),
        compiler_params=pltpu.CompilerParams(vmem_limit_bytes=VMEM_LIMIT),
        name="sample_merge",
    )(attn, ga, u, gc, state_t, x, wo, dw, dwb, lng, lnb)


def kernel(x_prompt, x_sample, cache_k, cache_v, cache_k_idx, state_conv, page_table, rel_bias, norm_g,
           w_in, q_norm_g, k_norm_g, dw_w, dw_b, ln_g, ln_b, w_out):
    depth = norm_g.shape[0]
    assert depth == 1, "single-layer step"
    B, S, D = x_prompt.shape
    Bd, T, _ = x_sample.shape
    assert T == 1
    n_phys, page = cache_k.shape[1], cache_k.shape[2]
    past = page_table.shape[1] * page

    w = w_in[0]
    c_ki = 2 * D_ATTN + 2 * N_HEADS * HEAD_DIM + N_IDX_HEADS * IDX_DIM
    c_conv = c_ki + IDX_DIM + N_IDX_HEADS
    pad = jnp.zeros((D, LANES - IDX_DIM - N_IDX_HEADS), w.dtype)
    w_all = jnp.concatenate([w[:, :c_ki], w[:, c_ki:c_conv], pad, w[:, c_conv:]], axis=1).astype(bf16)
    assert w_all.shape[1] == _C_END
    wo = w_out[0].astype(bf16)
    g = norm_g[0][None]
    qg = jnp.tile(q_norm_g[0], N_HEADS)[None]
    kg = jnp.tile(k_norm_g[0], N_HEADS)[None]
    hid = np.arange(D_ATTN) // HEAD_DIM
    bd = jnp.asarray((hid[:, None] == hid[None, :]) / HEAD_DIM, dtype=bf16)
    dwb, lng, lnb = dw_b[0][None], ln_g[0][None], ln_b[0][None]

    (qt, k_p, kb, v_p, vt, ga_p, qit, kw_p, wit, ki2, u_p, gc_p) = _inproj(
        x_prompt, g, w_all, qg, kg, bd, prompt=True)
    btab = _bias_tables(rel_bias)
    mixa = _prompt_attention(qt, qit, wit, kb, ki2, vt, ga_p, btab)
    y_prompt = _prompt_merge(mixa, u_p, gc_p, x_prompt, wo, dw_w[0], dwb, lng, lnb)

    xs = x_sample.reshape(1, Bd, D)
    q_s, k_s, v_s, ga_s, qi_s, kw_s, u_s, gc_s = (a[0] for a in _inproj(
        xs, g, w_all, qg, kg, bd, prompt=False))
    ki_s = kw_s[:, :IDX_DIM]
    wrow = jnp.broadcast_to(kw_s[:, IDX_DIM:IDX_DIM + N_IDX_HEADS, None], (Bd, N_IDX_HEADS, LANES))
    cki_t = jnp.transpose(cache_k_idx[0], (0, 2, 1))
    ck_t = jnp.transpose(cache_k[0], (0, 2, 3, 1)).reshape(n_phys, D_ATTN, page)
    cv_t = jnp.transpose(cache_v[0], (0, 2, 3, 1)).reshape(n_phys, D_ATTN, page)
    scores = _sample_scores(page_table, qi_s.reshape(Bd, N_IDX_HEADS, IDX_DIM), wrow, cki_t)
    mask, mnew = _sample_select(scores, qi_s, jnp.tile(ki_s, (1, N_IDX_HEADS)), kw_s, past=past)
    rbt = jnp.broadcast_to(rel_bias[:, :, None], (N_BUCKETS, N_HEADS, LANES))
    eye = jnp.asarray(np.arange(N_HEADS)[:, None] == hid[None, :], dtype=f32)
    qbd = q_s[:, None, :] * eye[None]
    attn_s = _sample_attention(page_table, qbd, k_s[:, None, :], v_s[:, None, :],
                               mask, mnew, rbt, ck_t, cv_t, past=past)
    state_t = jnp.transpose(state_conv[0], (1, 0, 2))
    y_sample = _sample_merge(attn_s[:, 0], ga_s, u_s, gc_s, state_t, x_sample[:, 0], wo,
                             dw_w[0], dwb, lng, lnb)

    heads = (N_HEADS, HEAD_DIM)
    return (
        y_prompt,
        y_sample[:, None],
        k_p.reshape((1, B, S) + heads),
        v_p.reshape((1, B, S) + heads),
        kw_p[None, :, :, :IDX_DIM],
        u_p[None, :, S - (CONV_W - 1):],
        k_s.reshape((1, Bd, 1) + heads),
        v_s.reshape((1, Bd, 1) + heads),
        ki_s.reshape(1, Bd, 1, IDX_DIM),
        jnp.concatenate([state_conv[0][:, 1:], u_s[:, None]], axis=1)[None],
    )
```

```python
import functools
import math

import numpy as np
import jax
import jax.numpy as jnp
from jax import lax
from jax.experimental import pallas as pl
from jax.experimental.pallas import tpu as pltpu

N_HEADS = 8
HEAD_DIM = 64
N_IDX_HEADS = 8
IDX_DIM = 64
D_ATTN = N_HEADS * HEAD_DIM
D_CONV = 512
CONV_W = 31
TOPK_MAX = 256
N_BUCKETS = 32
MAX_DISTANCE = 128
EPS = 1e-6
NEG = -1e30
INT_MIN = -2 ** 31
LOG2E = math.log2(math.e)

LANES = 128
Q_TILE = 256
K_CHUNK = 256
PV_ROWS = 144
VMEM_LIMIT = 56 * 1024 * 1024

f32 = jnp.float32
bf16 = jnp.bfloat16


def _bucket_thresholds():
    n = np.arange(0, 4 * MAX_DISTANCE)
    max_exact = N_BUCKETS // 2
    nf = np.maximum(n, 1).astype(np.float32)
    large = max_exact + (np.log(nf / np.float32(max_exact)) / np.float32(math.log(MAX_DISTANCE / max_exact))
                         * np.float32(N_BUCKETS - max_exact)).astype(np.int32)
    bucket = np.where(n < max_exact, n, np.minimum(large, N_BUCKETS - 1))
    return [int(n[bucket >= i].min()) for i in range(1, N_BUCKETS)]


BUCKET_THRESH = _bucket_thresholds()
assert BUCKET_THRESH[-1] <= K_CHUNK + 1 and K_CHUNK % Q_TILE == 0
N_PAR = K_CHUNK // Q_TILE


def _silu(x):
    return x * jax.nn.sigmoid(x)


def _to_key(x):
    b = pltpu.bitcast(x, jnp.int32)
    return jnp.where(b < 0, b ^ jnp.int32(0x7FFFFFFF), b)


_C_Q, _C_K, _C_V, _C_GA, _C_QI, _C_KW, _C_A, _C_B, _C_GC, _C_END = (
    0, 512, 1024, 1536, 2048, 2560, 2688, 3200, 3712, 4224)


def _inproj_kernel(x_ref, g_ref, w_ref, qg_ref, kg_ref, bd_ref, *outs, prompt, tm):
    xf = x_ref[0]
    ms = jnp.mean(xf * xf, axis=-1, keepdims=True)
    xn = (xf * lax.rsqrt(ms + EPS) * g_ref[...]).astype(bf16)

    def proj(a, b):
        return jnp.dot(xn, w_ref[:, a:b], preferred_element_type=f32)

    def head_norm(z, gain_ref):
        msq = jnp.dot((z * z).astype(bf16), bd_ref[...], preferred_element_type=f32)
        return z * lax.rsqrt(msq + EPS) * gain_ref[...]

    q = head_norm(proj(_C_Q, _C_K), qg_ref)
    k = head_norm(proj(_C_K, _C_V), kg_ref)
    v = proj(_C_V, _C_GA)
    ga = _silu(proj(_C_GA, _C_QI))
    qi = proj(_C_QI, _C_KW)
    kw = proj(_C_KW, _C_A)
    u = proj(_C_A, _C_B) * jax.nn.sigmoid(proj(_C_B, _C_GC))
    gc = _silu(proj(_C_GC, _C_END))

    if prompt:
        q_o, k_o, kb_o, v_o, vt_o, ga_o, qi_o, kw_o, wi_o, ki2_o, u_o, gc_o = outs
        q_o[0] = (q * (HEAD_DIM ** -0.5 * LOG2E)).T.astype(bf16)
        qi_o[0] = (qi * (IDX_DIM ** -0.5)).T.astype(bf16)
        wi_o[0] = kw.T[IDX_DIM:IDX_DIM + N_IDX_HEADS, :]
        kb_o[0] = k.astype(bf16)
        lane = lax.broadcasted_iota(jnp.int32, kw.shape, 1)
        ki2_o[0] = jnp.where(lane < IDX_DIM, kw, pltpu.roll(kw, IDX_DIM, 1)).astype(bf16)
        vt = v.T.astype(bf16)
        tail = jnp.where(lax.broadcasted_iota(jnp.int32, (PV_ROWS - 2 * HEAD_DIM, K_CHUNK), 0) < 8,
                         1.0, 0.0).astype(bf16)
        for c in range(tm // K_CHUNK):
            for i in range(N_HEADS // 2):
                blk = vt[i * 2 * HEAD_DIM:(i + 1) * 2 * HEAD_DIM, c * K_CHUNK:(c + 1) * K_CHUNK]
                vt_o[0, c, i] = jnp.concatenate([blk, tail], axis=0)
    else:
        q_o, k_o, v_o, ga_o, qi_o, kw_o, u_o, gc_o = outs
        q_o[0] = q
        qi_o[0] = qi
    k_o[0] = k
    v_o[0] = v
    ga_o[0] = ga
    kw_o[0] = kw
    u_o[0] = u
    gc_o[0] = gc


def _inproj(x, norm_g, w_all, qg, kg, bd, *, prompt):
    B, S, D = x.shape
    tm = 512 if prompt else S
    assert S % tm == 0 and (not prompt or tm % K_CHUNK == 0)
    grid = (B, S // tm)
    row = lambda b, i: (b, i, 0)
    const2 = lambda b, i: (0, 0)
    col = lambda b, i: (b, 0, i)
    nck = tm // K_CHUNK
    if prompt:
        out_shape = (
            jax.ShapeDtypeStruct((B, D_ATTN, S), bf16),
            jax.ShapeDtypeStruct((B, S, D_ATTN), f32),
            jax.ShapeDtypeStruct((B, S, D_ATTN), bf16),
            jax.ShapeDtypeStruct((B, S, D_ATTN), f32),
            jax.ShapeDtypeStruct((B, S // K_CHUNK, N_HEADS // 2, PV_ROWS, K_CHUNK), bf16),
            jax.ShapeDtypeStruct((B, S, D_ATTN), f32),
            jax.ShapeDtypeStruct((B, N_IDX_HEADS * IDX_DIM, S), bf16),
            jax.ShapeDtypeStruct((B, S, LANES), f32),
            jax.ShapeDtypeStruct((B, N_IDX_HEADS, S), f32),
            jax.ShapeDtypeStruct((B, S, LANES), bf16),
            jax.ShapeDtypeStruct((B, S, D_CONV), f32),
            jax.ShapeDtypeStruct((B, S, D_CONV), f32),
        )
        out_specs = (
            pl.BlockSpec((1, D_ATTN, tm), col),
            pl.BlockSpec((1, tm, D_ATTN), row),
            pl.BlockSpec((1, tm, D_ATTN), row),
            pl.BlockSpec((1, tm, D_ATTN), row),
            pl.BlockSpec((1, nck, N_HEADS // 2, PV_ROWS, K_CHUNK), lambda b, i: (b, i, 0, 0, 0)),
            pl.BlockSpec((1, tm, D_ATTN), row),
            pl.BlockSpec((1, N_IDX_HEADS * IDX_DIM, tm), col),
            pl.BlockSpec((1, tm, LANES), row),
            pl.BlockSpec((1, N_IDX_HEADS, tm), col),
            pl.BlockSpec((1, tm, LANES), row),
            pl.BlockSpec((1, tm, D_CONV), row),
            pl.BlockSpec((1, tm, D_CONV), row),
        )
    else:
        out_shape = (
            jax.ShapeDtypeStruct((B, S, D_ATTN), f32),
            jax.ShapeDtypeStruct((B, S, D_ATTN), f32),
            jax.ShapeDtypeStruct((B, S, D_ATTN), f32),
            jax.ShapeDtypeStruct((B, S, D_ATTN), f32),
            jax.ShapeDtypeStruct((B, S, D_ATTN), f32),
            jax.ShapeDtypeStruct((B, S, LANES), f32),
            jax.ShapeDtypeStruct((B, S, D_CONV), f32),
            jax.ShapeDtypeStruct((B, S, D_CONV), f32),
        )
        out_specs = tuple(pl.BlockSpec((1, tm, s.shape[-1]), row) for s in out_shape)
    return pl.pallas_call(
        functools.partial(_inproj_kernel, prompt=prompt, tm=tm),
        grid=grid,
        in_specs=[
            pl.BlockSpec((1, tm, D), row),
            pl.BlockSpec((1, D), const2),
            pl.BlockSpec((D, _C_END), const2),
            pl.BlockSpec((1, D_ATTN), const2),
            pl.BlockSpec((1, D_ATTN), const2),
            pl.BlockSpec((D_ATTN, D_ATTN), const2),
        ],
        out_specs=out_specs,
        out_shape=out_shape,
        compiler_params=pltpu.CompilerParams(
            dimension_semantics=("parallel", "parallel"), vmem_limit_bytes=VMEM_LIMIT),
        name="inproj_prompt" if prompt else "inproj_sample",
    )(x, norm_g, w_all, qg, kg, bd)


def _bias_table_kernel(rb_ref, out_ref):
    shape = (K_CHUNK, Q_TILE)
    krow = lax.broadcasted_iota(jnp.int32, shape, 0)
    qcol = lax.broadcasted_iota(jnp.int32, shape, 1)
    for par in range(N_PAR):
        for which in range(2):
            dist = Q_TILE * par + K_CHUNK * (1 - which) + qcol - krow
            for h in range(N_HEADS):
                far = rb_ref[N_BUCKETS - 1, h]
                acc = jnp.full(shape, (rb_ref[0, h] - far) * LOG2E, f32)
                for i, t in enumerate(BUCKET_THRESH):
                    acc = jnp.where(dist >= t, (rb_ref[i + 1, h] - far) * LOG2E, acc)
                out_ref[par, which, h // 2, :, (h % 2) * Q_TILE:(h % 2 + 1) * Q_TILE] = (
                    jnp.where(dist >= 0, acc, NEG))


def _bias_tables(rel_bias):
    return pl.pallas_call(
        _bias_table_kernel,
        in_specs=[pl.BlockSpec(memory_space=pltpu.SMEM)],
        out_specs=pl.BlockSpec(memory_space=pltpu.VMEM),
        out_shape=jax.ShapeDtypeStruct((N_PAR, 2, N_HEADS // 2, K_CHUNK, 2 * Q_TILE), f32),
        name="bias_tables",
    )(rel_bias)


N_PAIRS = N_HEADS // 2
SUB = 8


COUNT_ROWS = 32
SEARCH_BITS = 8
MAX_ROWS = 16


def _fold(x, op, rows):
    return op(x.reshape(x.shape[0] // rows, rows, x.shape[1]), axis=0)


def _attn_kernel(qt_ref, qit_ref, wit_ref, k_ref, ki_ref, vt_ref, ga_ref, bt_ref, out_ref,
                 sc_ref, key_ref, lg_ref, qbd_ref, qibd_ref, o_ref, *, ksel, idx_bits):
    j = pl.program_id(1)
    nck = j // N_PAR + 1
    par = j % N_PAR
    shape = (K_CHUNK, Q_TILE)
    key_pos = lax.broadcasted_iota(jnp.int32, shape, 0)
    q_pos = j * Q_TILE + lax.broadcasted_iota(jnp.int32, shape, 1)

    top = lax.broadcasted_iota(jnp.int32, (2 * HEAD_DIM, Q_TILE), 0) < HEAD_DIM
    for i in range(N_PAIRS):
        rows = slice(i * 2 * HEAD_DIM, (i + 1) * 2 * HEAD_DIM)
        for src, dst in ((qt_ref, qbd_ref), (qit_ref, qibd_ref)):
            x = src[0, rows, :]
            zero = jnp.zeros_like(x)
            dst[i] = jnp.concatenate([jnp.where(top, x, zero), jnp.where(top, zero, x)], axis=1)
    wi = wit_ref[0] * (N_IDX_HEADS ** -0.5)
    wpair = [jnp.concatenate([wi[2 * i:2 * i + 1], wi[2 * i + 1:2 * i + 2]], axis=1)
             for i in range(N_PAIRS)]

    def chunk_rows(c):
        return pl.ds(pl.multiple_of(c * K_CHUNK, K_CHUNK), K_CHUNK)

    def score_body(c, carry):
        ki2 = ki_ref[0, chunk_rows(c), :]
        acc = jnp.zeros(shape, f32)
        for i in range(N_PAIRS):
            r = jnp.maximum(jnp.dot(ki2, qibd_ref[i], preferred_element_type=f32), 0.0) * wpair[i]
            acc = acc + r[:, :Q_TILE] + r[:, Q_TILE:]
        sc_ref[c] = jnp.where(c * K_CHUNK + key_pos <= q_pos, acc, -jnp.inf)
        return carry

    lax.fori_loop(0, nck, score_body, 0)

    def count(pred):
        def body(c, cnt):
            return cnt + _fold(jnp.where(pred(c), 1.0, 0.0), jnp.sum, COUNT_ROWS)
        cnt = lax.fori_loop(0, nck, body, jnp.zeros((COUNT_ROWS, Q_TILE), f32))
        return jnp.sum(cnt, axis=0, keepdims=True)

    def write_mask(pred):
        def body(c, carry):
            sc_ref[c] = jnp.where(pred(c), 0.0, NEG)
            return carry
        lax.fori_loop(0, nck, body, 0)

    need_search = (j + 1) * Q_TILE > ksel

    @pl.when(jnp.logical_not(need_search))
    def _():
        write_mask(lambda c: c * K_CHUNK + key_pos < (j + 1) * Q_TILE)

    @pl.when(need_search)
    def _():
        def key_body(c, carry):
            key_ref[c] = _to_key(sc_ref[c])
            return carry
        lax.fori_loop(0, nck, key_body, 0)

        def bit_body(i, carry):
            lo, n_ge = carry
            cand = lo + jnp.left_shift(jnp.int32(1), 31 - i)
            candb = jnp.broadcast_to(cand, shape)
            tot = count(lambda c: key_ref[c] >= candb)
            ok = tot >= ksel
            return jnp.where(ok, cand, lo), jnp.where(ok, tot, n_ge)

        carry = (jnp.full((1, Q_TILE), INT_MIN, jnp.int32),
                 jnp.full((1, Q_TILE), nck * K_CHUNK, jnp.int32).astype(f32))
        for first_bit in range(0, 32, SEARCH_BITS):
            carry = lax.cond(
                jnp.max(carry[1]) > ksel,
                lambda c, first_bit=first_bit: lax.fori_loop(first_bit, first_bit + SEARCH_BITS, bit_body, c),
                lambda c: c, carry)
        thr, n_ge = carry
        thrb = jnp.broadcast_to(thr, shape)
        has_tie = jnp.max(n_ge) > ksel

        @pl.when(jnp.logical_not(has_tie))
        def _():
            write_mask(lambda c: key_ref[c] >= thrb)

        @pl.when(has_tie)
        def _():
            need = ksel - count(lambda c: key_ref[c] > thrb)

            def cut_body(i, x):
                cand = x + jnp.left_shift(jnp.int32(1), idx_bits - 1 - i)
                candb = jnp.broadcast_to(cand, shape)
                g = count(lambda c: jnp.where(key_ref[c] == thrb, c * K_CHUNK + key_pos, 2 ** 30) < candb)
                return jnp.where(g < need, cand, x)

            cut = lax.fori_loop(0, idx_bits, cut_body, jnp.zeros((1, Q_TILE), jnp.int32))
            cutb = jnp.broadcast_to(cut, shape)

            def sel(c):
                kc = key_ref[c]
                eq_idx = jnp.where(kc == thrb, c * K_CHUNK + key_pos, 2 ** 30)
                return jnp.where(kc > thrb, 0, eq_idx) <= cutb
            write_mask(sel)

    def logits_chunk(c, which, ms):
        mask = sc_ref[c]
        mask2 = jnp.concatenate([mask, mask], axis=1)
        out = []
        for i in range(N_PAIRS):
            kc = k_ref[0, chunk_rows(c), i * 2 * HEAD_DIM:(i + 1) * 2 * HEAD_DIM]
            s = jnp.dot(kc, qbd_ref[i], preferred_element_type=f32) + mask2
            if which is not None:
                s = s + bt_ref[par, which, i]
            lg_ref[i, c] = s
            out.append(jnp.maximum(ms[i], _fold(s, jnp.max, MAX_ROWS)))
        return tuple(out)

    ms = tuple(jnp.full((MAX_ROWS, 2 * Q_TILE), NEG, f32) for _ in range(N_PAIRS))
    ms = lax.fori_loop(0, jnp.maximum(nck - 2, 0), lambda c, ms: logits_chunk(c, None, ms), ms)
    ms = lax.cond(nck >= 2, lambda ms: logits_chunk(nck - 2, 0, ms), lambda ms: ms, ms)
    ms = logits_chunk(nck - 1, 1, ms)
    mrow = [jnp.max(m, axis=0, keepdims=True) for m in ms]

    o_ref[...] = jnp.zeros(o_ref.shape, f32)

    def pv_body(c, carry):
        for i in range(N_PAIRS):
            p = jnp.exp2((lg_ref[i, c] - mrow[i]).astype(bf16))
            o_ref[i] += jnp.dot(vt_ref[0, c, i], p, preferred_element_type=f32)
        return carry
    lax.fori_loop(0, nck, pv_body, 0)

    pieces = []
    for i in range(N_PAIRS):
        o = o_ref[i]
        l = o[2 * HEAD_DIM:2 * HEAD_DIM + 1, :]
        pieces.append(o[0:HEAD_DIM, 0:Q_TILE] / l[:, 0:Q_TILE])
        pieces.append(o[HEAD_DIM:2 * HEAD_DIM, Q_TILE:] / l[:, Q_TILE:])
    out_ref[0] = (jnp.concatenate(pieces, axis=0).T * ga_ref[0]).astype(bf16)


def _prompt_attention(qt, qit, wit, kb, ki2, vt, ga, btab):
    B, S, _ = kb.shape
    assert S % K_CHUNK == 0
    nq = S // Q_TILE
    nc = S // K_CHUNK
    ksel = min(TOPK_MAX, S // 4)
    qcol = lambda b, j: (b, 0, j)
    perb3 = lambda b, j: (b, 0, 0)
    return pl.pallas_call(
        functools.partial(_attn_kernel, ksel=ksel, idx_bits=max(1, (S - 1).bit_length())),
        grid=(B, nq),
        in_specs=[
            pl.BlockSpec((1, D_ATTN, Q_TILE), qcol),
            pl.BlockSpec((1, N_IDX_HEADS * IDX_DIM, Q_TILE), qcol),
            pl.BlockSpec((1, N_IDX_HEADS, Q_TILE), qcol),
            pl.BlockSpec((1, S, D_ATTN), perb3),
            pl.BlockSpec((1, S, LANES), perb3),
            pl.BlockSpec((1, nc, N_PAIRS, PV_ROWS, K_CHUNK), lambda b, j: (b, 0, 0, 0, 0)),
            pl.BlockSpec((1, Q_TILE, D_ATTN), lambda b, j: (b, j, 0)),
            pl.BlockSpec((N_PAR, 2, N_PAIRS, K_CHUNK, 2 * Q_TILE), lambda b, j: (0, 0, 0, 0, 0)),
        ],
        out_specs=pl.BlockSpec((1, Q_TILE, D_ATTN), lambda b, j: (b, j, 0)),
        out_shape=jax.ShapeDtypeStruct((B, S, D_ATTN), bf16),
        scratch_shapes=[
            pltpu.VMEM((nc, K_CHUNK, Q_TILE), f32),
            pltpu.VMEM((nc, K_CHUNK, Q_TILE), jnp.int32),
            pltpu.VMEM((N_PAIRS, nc, K_CHUNK, 2 * Q_TILE), f32),
            pltpu.VMEM((N_PAIRS, 2 * HEAD_DIM, 2 * Q_TILE), bf16),
            pltpu.VMEM((N_PAIRS, 2 * IDX_DIM, 2 * Q_TILE), bf16),
            pltpu.VMEM((N_PAIRS, PV_ROWS, 2 * Q_TILE), f32),
        ],
        compiler_params=pltpu.CompilerParams(
            dimension_semantics=("parallel", "arbitrary"), vmem_limit_bytes=VMEM_LIMIT),
        name="prompt_attention",
    )(qt, qit, wit, kb, ki2, vt, ga, btab)


CONV_HALO = 32
CONV_ROWS = 32


def _layer_norm_swish(c, lng_ref, lnb_ref):
    mu = jnp.mean(c, axis=-1, keepdims=True)
    d = c - mu
    var = jnp.mean(d * d, axis=-1, keepdims=True)
    return _silu(d * lax.rsqrt(var + EPS) * lng_ref[...] + lnb_ref[...])


def _merge_kernel(mixa_ref, ucur_ref, uprev_ref, gc_ref, x_ref, wo_ref, dw_ref, dwb_ref, lng_ref,
                  lnb_ref, y_ref, ext_ref, sh_ref, cg_ref, *, tq):
    i = pl.program_id(1)
    n = CONV_HALO + tq
    ext_ref[0:CONV_HALO] = jnp.where(i > 0, uprev_ref[0], 0.0)
    ext_ref[CONV_HALO:n] = ucur_ref[0]
    ext_ref[n:n + SUB] = jnp.zeros((SUB, D_CONV), f32)
    for r in range(1, SUB):
        sh_ref[r - 1] = ext_ref[r:r + n, :]
    first = CONV_HALO - (CONV_W - 1)
    for c in range(tq // CONV_ROWS):
        acc = jnp.broadcast_to(dwb_ref[...], (CONV_ROWS, D_CONV))
        for w in range(CONV_W):
            a, r = divmod(first + w, SUB)
            start = c * CONV_ROWS + a * SUB
            src = ext_ref if r == 0 else sh_ref.at[r - 1]
            acc = acc + src[start:start + CONV_ROWS, :] * dw_ref[w:w + 1, :]
        rows = slice(c * CONV_ROWS, (c + 1) * CONV_ROWS)
        cg_ref[rows] = (_layer_norm_swish(acc, lng_ref, lnb_ref) * gc_ref[0, rows]).astype(bf16)
    y_ref[0] = (x_ref[0]
                + jnp.dot(mixa_ref[0], wo_ref[0:D_ATTN], preferred_element_type=f32)
                + jnp.dot(cg_ref[...], wo_ref[D_ATTN:D_ATTN + D_CONV], preferred_element_type=f32))


def _prompt_merge(mixa, u, gc, x, wo, dw, dwb, lng, lnb):
    B, S, D = x.shape
    tq = 256
    assert S % tq == 0 and tq % CONV_HALO == 0
    row = lambda b, i: (b, i, 0)
    const2 = lambda b, i: (0, 0)
    halo = lambda b, i: (b, jnp.maximum(i * (tq // CONV_HALO) - 1, 0), 0)
    return pl.pallas_call(
        functools.partial(_merge_kernel, tq=tq),
        grid=(B, S // tq),
        in_specs=[
            pl.BlockSpec((1, tq, D_ATTN), row),
            pl.BlockSpec((1, tq, D_CONV), row),
            pl.BlockSpec((1, CONV_HALO, D_CONV), halo),
            pl.BlockSpec((1, tq, D_CONV), row),
            pl.BlockSpec((1, tq, D), row),
            pl.BlockSpec((D_ATTN + D_CONV, D), const2),
            pl.BlockSpec((CONV_W, D_CONV), const2),
            pl.BlockSpec((1, D_CONV), const2),
            pl.BlockSpec((1, D_CONV), const2),
            pl.BlockSpec((1, D_CONV), const2),
        ],
        out_specs=pl.BlockSpec((1, tq, D), row),
        out_shape=jax.ShapeDtypeStruct((B, S, D), f32),
        scratch_shapes=[
            pltpu.VMEM((CONV_HALO + tq + SUB, D_CONV), f32),
            pltpu.VMEM((SUB - 1, CONV_HALO + tq, D_CONV), f32),
            pltpu.VMEM((tq, D_CONV), bf16),
        ],
        compiler_params=pltpu.CompilerParams(
            dimension_semantics=("parallel", "arbitrary"), vmem_limit_bytes=VMEM_LIMIT),
        name="prompt_merge",
    )(mixa, u, u, gc, x, wo, dw, dwb, lng, lnb)


SCORE_UNROLL = 8


def _sample_score_kernel(pt_ref, qi_ref, w_ref, cki_ref, out_ref, buf_ref, sem_ref, *, npg):
    b = pl.program_id(0)
    nb = pl.num_programs(0)
    slot = b % 2

    def page_copy(bb, p, sl):
        return pltpu.make_async_copy(cki_ref.at[pt_ref[bb, p]], buf_ref.at[sl, p], sem_ref.at[sl])

    def start_all(bb, sl):
        def body(p, carry):
            page_copy(bb, p, sl).start()
            return carry
        lax.fori_loop(0, npg, body, 0)

    @pl.when(b == 0)
    def _():
        start_all(0, 0)

    @pl.when(b + 1 < nb)
    def _():
        start_all(b + 1, 1 - slot)

    def wait_body(p, carry):
        page_copy(b, p, slot).wait()
        return carry
    lax.fori_loop(0, npg, wait_body, 0)

    qs = (qi_ref[0] * (IDX_DIM ** -0.5)).astype(bf16)
    wcol = w_ref[0] * (N_IDX_HEADS ** -0.5)
    wcol2 = jnp.concatenate([wcol, wcol], axis=1)

    def body(i, carry):
        for u in range(0, SCORE_UNROLL, 2):
            p = i * SCORE_UNROLL + u
            pages = jnp.concatenate([buf_ref[slot, p], buf_ref[slot, p + 1]], axis=1).astype(bf16)
            s = jnp.dot(qs, pages, preferred_element_type=f32)
            r = jnp.sum(jnp.maximum(s, 0.0) * wcol2, axis=0, keepdims=True)
            out_ref[0, pl.ds(p, 1), :] = r[:, :LANES]
            out_ref[0, pl.ds(p + 1, 1), :] = r[:, LANES:]
        return carry
    lax.fori_loop(0, npg // SCORE_UNROLL, body, 0)


def _sample_scores(page_table, qi3, wrow, cki_t):
    Bd, npg = page_table.shape
    page = cki_t.shape[2]
    assert page == LANES and npg % SCORE_UNROLL == 0
    grid_spec = pltpu.PrefetchScalarGridSpec(
        num_scalar_prefetch=1,
        grid=(Bd,),
        in_specs=[
            pl.BlockSpec((1, N_IDX_HEADS, IDX_DIM), lambda b, pt: (b, 0, 0)),
            pl.BlockSpec((1, N_IDX_HEADS, LANES), lambda b, pt: (b, 0, 0)),
            pl.BlockSpec(memory_space=pl.ANY),
        ],
        out_specs=pl.BlockSpec((1, npg, page), lambda b, pt: (b, 0, 0)),
        scratch_shapes=[
            pltpu.VMEM((2, npg, IDX_DIM, page), f32),
            pltpu.SemaphoreType.DMA((2,)),
        ],
    )
    return pl.pallas_call(
        functools.partial(_sample_score_kernel, npg=npg),
        grid_spec=grid_spec,
        out_shape=jax.ShapeDtypeStruct((Bd, npg, page), f32),
        compiler_params=pltpu.CompilerParams(
            dimension_semantics=("arbitrary",), vmem_limit_bytes=VMEM_LIMIT),
        name="sample_scores",
    )(page_table, qi3, wrow, cki_t)


def _sample_select_kernel(sc_ref, qi_ref, kit_ref, kw_ref, mask_ref, mnew_ref, key_ref, *,
                          npg, ksel, past, idx_bits):
    Bd = sc_ref.shape[0]
    shape = (Bd, npg, LANES)

    qb = (qi_ref[...] * (IDX_DIM ** -0.5)).astype(bf16).astype(f32)
    kb = kit_ref[...].astype(bf16).astype(f32)
    prod = qb * kb
    grp = lax.broadcasted_iota(jnp.int32, prod.shape, 1) // IDX_DIM
    kwv = kw_ref[...]
    s_new = jnp.zeros((Bd, 1), f32)
    for h in range(N_IDX_HEADS):
        sh = jnp.sum(jnp.where(grp == h, prod, 0.0), axis=1, keepdims=True)
        s_new = s_new + jnp.maximum(sh, 0.0) * (kwv[:, IDX_DIM + h:IDX_DIM + h + 1] * (N_IDX_HEADS ** -0.5))
    key_new = _to_key(s_new).reshape(Bd, 1, 1)

    key_ref[...] = _to_key(sc_ref[...])
    pos = (lax.broadcasted_iota(jnp.int32, shape, 1) * LANES
           + lax.broadcasted_iota(jnp.int32, shape, 2))

    def count(pred, pred_new):
        x = jnp.where(pred, 1.0, 0.0)
        s = jnp.sum(jnp.sum(x, axis=1, keepdims=True), axis=2, keepdims=True)
        return s + jnp.where(pred_new, 1.0, 0.0)

    def bit_body(i, lo):
        cand = lo + jnp.left_shift(jnp.int32(1), 31 - i)
        tot = count(key_ref[...] >= cand, key_new >= cand)
        return jnp.where(tot >= ksel, cand, lo)

    thr = lax.fori_loop(0, 32, bit_body, jnp.full((Bd, 1, 1), INT_MIN, jnp.int32))
    keys = key_ref[...]
    need = ksel - count(keys > thr, key_new > thr)
    big = jnp.int32(2 ** 30)
    eq_pos = jnp.where(keys == thr, pos, big)
    eq_pos_new = jnp.where(key_new == thr, jnp.int32(past), big)

    def cut_body(i, x):
        cand = x + jnp.left_shift(jnp.int32(1), idx_bits - 1 - i)
        g = count(eq_pos < cand, eq_pos_new < cand)
        return jnp.where(g < need, cand, x)

    cut = lax.fori_loop(0, idx_bits, cut_body, jnp.zeros((Bd, 1, 1), jnp.int32))
    mask_ref[...] = jnp.where(jnp.where(keys > thr, 0, eq_pos) <= cut, 0.0, NEG)
    sel_new = jnp.where(key_new > thr, 0, eq_pos_new) <= cut
    mnew_ref[...] = jnp.broadcast_to(jnp.where(sel_new, 0.0, NEG), mnew_ref.shape)


def _sample_select(scores, qi, ki_tiled, kw, *, past):
    Bd, npg, _ = scores.shape
    ksel = min(TOPK_MAX, (past + 1) // 4)
    vm = pl.BlockSpec(memory_space=pltpu.VMEM)
    return pl.pallas_call(
        functools.partial(_sample_select_kernel, npg=npg, ksel=ksel, past=past,
                          idx_bits=past.bit_length()),
        in_specs=[vm, vm, vm, vm],
        out_specs=(vm, vm),
        out_shape=(jax.ShapeDtypeStruct((Bd, npg, LANES), f32),
                   jax.ShapeDtypeStruct((Bd, N_HEADS, LANES), f32)),
        scratch_shapes=[pltpu.VMEM((Bd, npg, LANES), jnp.int32)],
        compiler_params=pltpu.CompilerParams(vmem_limit_bytes=VMEM_LIMIT),
        name="sample_select",
    )(scores, qi, ki_tiled, kw)


PAGE_BUFS = 16
PAGE_GROUP = 4


def _sample_attn_kernel(pt_ref, qbd_ref, knew_ref, vnew_ref, mask_ref, mnew_ref, rbt_ref, ck_ref, cv_ref,
                        out_ref, buf_ref, lg_ref, sem_ref, *, npg, past):
    b = pl.program_id(0)
    nb = pl.num_programs(0)
    per_b = 2 * npg
    total = nb * per_b

    def page_copy(g, src_ref):
        phys = pt_ref[g // per_b, g % npg]
        slot = g % PAGE_BUFS
        return pltpu.make_async_copy(src_ref.at[phys], buf_ref.at[slot], sem_ref.at[slot])

    def start(g):
        is_k = g % per_b < npg

        @pl.when(is_k)
        def _():
            page_copy(g, ck_ref).start()

        @pl.when(jnp.logical_not(is_k))
        def _():
            page_copy(g, cv_ref).start()

    def wait(g):
        page_copy(g, ck_ref).wait()

    @pl.when(b == 0)
    def _():
        for g in range(PAGE_BUFS):
            start(g)

    g0 = b * per_b
    qbd = (qbd_ref[0] * (HEAD_DIM ** -0.5 * LOG2E)).astype(bf16)
    blk = (lax.broadcasted_iota(jnp.int32, (N_HEADS, D_ATTN), 0)
           == lax.broadcasted_iota(jnp.int32, (N_HEADS, D_ATTN), 1) // HEAD_DIM)
    tok = lax.broadcasted_iota(jnp.int32, (N_HEADS, LANES), 1)

    def bias_of(dist):
        far = rbt_ref[N_BUCKETS - 1]
        acc = jnp.broadcast_to(rbt_ref[0] - far, dist.shape)
        for i, t in enumerate(BUCKET_THRESH):
            acc = jnp.where(dist >= t, rbt_ref[i + 1] - far, acc)
        return acc * LOG2E

    def refill(gs):
        for g in gs:
            @pl.when(g + PAGE_BUFS < total)
            def _():
                start(g + PAGE_BUFS)

    def k_body(it, m):
        gs = [g0 + it * PAGE_GROUP + u for u in range(PAGE_GROUP)]
        for g in gs:
            wait(g)
        for u, g in enumerate(gs):
            p = it * PAGE_GROUP + u
            kp = buf_ref[g % PAGE_BUFS].astype(bf16)
            s = jnp.dot(qbd, kp, preferred_element_type=f32) + mask_ref[0, pl.ds(p, 1), :]
            lg_ref[p] = s
            m = jnp.maximum(m, s)
        refill(gs)
        return m

    m = lax.fori_loop(0, npg // PAGE_GROUP, k_body, jnp.full((N_HEADS, LANES), NEG, f32))
    last = lg_ref[npg - 1] + bias_of(past - ((npg - 1) * LANES + tok))
    lg_ref[npg - 1] = last
    m = jnp.maximum(m, last)

    s_new = (jnp.sum(qbd_ref[0] * knew_ref[0], axis=1, keepdims=True) * (HEAD_DIM ** -0.5 * LOG2E)
             + bias_of(jnp.zeros((N_HEADS, LANES), jnp.int32))[:, 0:1] + mnew_ref[0][:, 0:1])
    m_row = jnp.maximum(jnp.max(m, axis=1, keepdims=True), s_new)
    mb = jnp.broadcast_to(m_row, (N_HEADS, LANES))

    def v_body(it, carry):
        acc, l = carry
        gs = [g0 + npg + it * PAGE_GROUP + u for u in range(PAGE_GROUP)]
        for g in gs:
            wait(g)
        for u, g in enumerate(gs):
            vp = buf_ref[g % PAGE_BUFS].astype(bf16)
            pr = jnp.exp2(lg_ref[it * PAGE_GROUP + u] - mb)
            acc = acc + lax.dot_general(pr.astype(bf16), vp, (((1,), (1,)), ((), ())),
                                        preferred_element_type=f32)
            l = l + pr
        refill(gs)
        return acc, l

    acc, l = lax.fori_loop(0, npg // PAGE_GROUP, v_body, (jnp.zeros((N_HEADS, D_ATTN), f32),
                                                          jnp.zeros((N_HEADS, LANES), f32)))
    p_new = jnp.exp2(s_new - m_row)
    l_tot = jnp.sum(l, axis=1, keepdims=True) + p_new
    num = jnp.where(blk, acc + p_new * vnew_ref[0], 0.0) / l_tot
    out_ref[0] = jnp.sum(num, axis=0, keepdims=True)


def _sample_attention(page_table, qbd, k_new_flat, v_new_flat, mask, mnew, rbt, ck_t, cv_t, *, past):
    Bd, npg = page_table.shape
    rows, page = ck_t.shape[1], ck_t.shape[2]
    assert rows == D_ATTN and page == LANES
    assert npg % PAGE_GROUP == 0 and PAGE_BUFS % PAGE_GROUP == 0 and 2 * npg >= PAGE_BUFS
    grid_spec = pltpu.PrefetchScalarGridSpec(
        num_scalar_prefetch=1,
        grid=(Bd,),
        in_specs=[
            pl.BlockSpec((1, N_HEADS, D_ATTN), lambda b, pt: (b, 0, 0)),
            pl.BlockSpec((1, 1, D_ATTN), lambda b, pt: (b, 0, 0)),
            pl.BlockSpec((1, 1, D_ATTN), lambda b, pt: (b, 0, 0)),
            pl.BlockSpec((1, npg, LANES), lambda b, pt: (b, 0, 0)),
            pl.BlockSpec((1, N_HEADS, LANES), lambda b, pt: (b, 0, 0)),
            pl.BlockSpec((N_BUCKETS, N_HEADS, LANES), lambda b, pt: (0, 0, 0)),
            pl.BlockSpec(memory_space=pl.ANY),
            pl.BlockSpec(memory_space=pl.ANY),
        ],
        out_specs=pl.BlockSpec((1, 1, D_ATTN), lambda b, pt: (b, 0, 0)),
        scratch_shapes=[
            pltpu.VMEM((PAGE_BUFS, rows, page), f32),
            pltpu.VMEM((npg, N_HEADS, LANES), f32),
            pltpu.SemaphoreType.DMA((PAGE_BUFS,)),
        ],
    )
    return pl.pallas_call(
        functools.partial(_sample_attn_kernel, npg=npg, past=past),
        grid_spec=grid_spec,
        out_shape=jax.ShapeDtypeStruct((Bd, 1, D_ATTN), f32),
        compiler_params=pltpu.CompilerParams(
            dimension_semantics=("arbitrary",), vmem_limit_bytes=VMEM_LIMIT),
        name="sample_attention",
    )(page_table, qbd, k_new_flat, v_new_flat, mask, mnew, rbt, ck_t, cv_t)


def _sample_merge_kernel(attn_ref, ga_ref, u_ref, gc_ref, st_ref, x_ref, wo_ref, dw_ref, dwb_ref,
                         lng_ref, lnb_ref, y_ref):
    acc = dwb_ref[...] + u_ref[...] * dw_ref[CONV_W - 1:CONV_W, :]
    for w in range(CONV_W - 1):
        acc = acc + st_ref[w] * dw_ref[w:w + 1, :]
    cg = (_layer_norm_swish(acc, lng_ref, lnb_ref) * gc_ref[...]).astype(bf16)
    mixa = (attn_ref[...] * ga_ref[...]).astype(bf16)
    y_ref[...] = (x_ref[...]
                  + jnp.dot(mixa, wo_ref[0:D_ATTN], preferred_element_type=f32)
                  + jnp.dot(cg, wo_ref[D_ATTN:D_ATTN + D_CONV], preferred_element_type=f32))


def _sample_merge(attn, ga, u, gc, state_t, x, wo, dw, dwb, lng, lnb):
    vm = pl.BlockSpec(memory_space=pltpu.VMEM)
    return pl.pallas_call(
        _sample_merge_kernel,
        in_specs=[vm] * 11,
        out_specs=vm,
        out_shape=jax.ShapeDtypeStruct(x.shape, f32),
        compiler_params=pltpu.CompilerParams(vmem_limit_bytes=VMEM_LIMIT),
        name="sample_merge",
    )(attn, ga, u, gc, state_t, x, wo, dw, dwb, lng, lnb)


def kernel(x_prompt, x_sample, cache_k, cache_v, cache_k_idx, state_conv, page_table, rel_bias, norm_g,
           w_in, q_norm_g, k_norm_g, dw_w, dw_b, ln_g, ln_b, w_out):
    depth = norm_g.shape[0]
    assert depth == 1, "single-layer step"
    B, S, D = x_prompt.shape
    Bd, T, _ = x_sample.shape
    assert T == 1
    n_phys, page = cache_k.shape[1], cache_k.shape[2]
    past = page_table.shape[1] * page

    w = w_in[0]
    c_ki = 2 * D_ATTN + 2 * N_HEADS * HEAD_DIM + N_IDX_HEADS * IDX_DIM
    c_conv = c_ki + IDX_DIM + N_IDX_HEADS
    pad = jnp.zeros((D, LANES - IDX_DIM - N_IDX_HEADS), w.dtype)
    w_all = jnp.concatenate([w[:, :c_ki], w[:, c_ki:c_conv], pad, w[:, c_conv:]], axis=1).astype(bf16)
    assert w_all.shape[1] == _C_END
    wo = w_out[0].astype(bf16)
    g = norm_g[0][None]
    qg = jnp.tile(q_norm_g[0], N_HEADS)[None]
    kg = jnp.tile(k_norm_g[0], N_HEADS)[None]
    hid = np.arange(D_ATTN) // HEAD_DIM
    bd = jnp.asarray((hid[:, None] == hid[None, :]) / HEAD_DIM, dtype=bf16)
    dwb, lng, lnb = dw_b[0][None], ln_g[0][None], ln_b[0][None]

    (qt, k_p, kb, v_p, vt, ga_p, qit, kw_p, wit, ki2, u_p, gc_p) = _inproj(
        x_prompt, g, w_all, qg, kg, bd, prompt=True)
    btab = _bias_tables(rel_bias)
    mixa = _prompt_attention(qt, qit, wit, kb, ki2, vt, ga_p, btab)
    y_prompt = _prompt_merge(mixa, u_p, gc_p, x_prompt, wo, dw_w[0], dwb, lng, lnb)

    xs = x_sample.reshape(1, Bd, D)
    q_s, k_s, v_s, ga_s, qi_s, kw_s, u_s, gc_s = (a[0] for a in _inproj(
        xs, g, w_all, qg, kg, bd, prompt=False))
    ki_s = kw_s[:, :IDX_DIM]
    wrow = jnp.broadcast_to(kw_s[:, IDX_DIM:IDX_DIM + N_IDX_HEADS, None], (Bd, N_IDX_HEADS, LANES))
    cki_t = jnp.transpose(cache_k_idx[0], (0, 2, 1))
    ck_t = jnp.transpose(cache_k[0], (0, 2, 3, 1)).reshape(n_phys, D_ATTN, page)
    cv_t = jnp.transpose(cache_v[0], (0, 2, 3, 1)).reshape(n_phys, D_ATTN, page)
    scores = _sample_scores(page_table, qi_s.reshape(Bd, N_IDX_HEADS, IDX_DIM), wrow, cki_t)
    mask, mnew = _sample_select(scores, qi_s, jnp.tile(ki_s, (1, N_IDX_HEADS)), kw_s, past=past)
    rbt = jnp.broadcast_to(rel_bias[:, :, None], (N_BUCKETS, N_HEADS, LANES))
    eye = jnp.asarray(np.arange(N_HEADS)[:, None] == hid[None, :], dtype=f32)
    qbd = q_s[:, None, :] * eye[None]
    attn_s = _sample_attention(page_table, qbd, k_s[:, None, :], v_s[:, None, :],
                               mask, mnew, rbt, ck_t, cv_t, past=past)
    state_t = jnp.transpose(state_conv[0], (1, 0, 2))
    y_sample = _sample_merge(attn_s[:, 0], ga_s, u_s, gc_s, state_t, x_sample[:, 0], wo,
                             dw_w[0], dwb, lng, lnb)

    heads = (N_HEADS, HEAD_DIM)
    return (
        y_prompt,
        y_sample[:, None],
        k_p.reshape((1, B, S) + heads),
        v_p.reshape((1, B, S) + heads),
        kw_p[None, :, :, :IDX_DIM],
        u_p[None, :, S - (CONV_W - 1):],
        k_s.reshape((1, Bd, 1) + heads),
        v_s.reshape((1, Bd, 1) + heads),
        ki_s.reshape(1, Bd, 1, IDX_DIM),
        jnp.concatenate([state_conv[0][:, 1:], u_s[:, None]], axis=1)[None],
    )
```

```python
import functools
import math

import numpy as np
import jax
import jax.numpy as jnp
from jax import lax
from jax.experimental import pallas as pl
from jax.experimental.pallas import tpu as pltpu

N_HEADS = 8
HEAD_DIM = 64
N_IDX_HEADS = 8
IDX_DIM = 64
D_ATTN = N_HEADS * HEAD_DIM
D_CONV = 512
CONV_W = 31
TOPK_MAX = 256
N_BUCKETS = 32
MAX_DISTANCE = 128
EPS = 1e-6
NEG = -1e30
INT_MIN = -2 ** 31
LOG2E = math.log2(math.e)

LANES = 128
Q_TILE = 256
K_CHUNK = 256
PV_ROWS = 144
VMEM_LIMIT = 56 * 1024 * 1024

f32 = jnp.float32
bf16 = jnp.bfloat16


def _bucket_thresholds():
    n = np.arange(0, 4 * MAX_DISTANCE)
    max_exact = N_BUCKETS // 2
    nf = np.maximum(n, 1).astype(np.float32)
    large = max_exact + (np.log(nf / np.float32(max_exact)) / np.float32(math.log(MAX_DISTANCE / max_exact))
                         * np.float32(N_BUCKETS - max_exact)).astype(np.int32)
    bucket = np.where(n < max_exact, n, np.minimum(large, N_BUCKETS - 1))
    return [int(n[bucket >= i].min()) for i in range(1, N_BUCKETS)]


BUCKET_THRESH = _bucket_thresholds()
assert BUCKET_THRESH[-1] <= K_CHUNK + 1 and K_CHUNK % Q_TILE == 0
N_PAR = K_CHUNK // Q_TILE


def _silu(x):
    return x * jax.nn.sigmoid(x)


def _to_key(x):
    b = pltpu.bitcast(x, jnp.int32)
    return jnp.where(b < 0, b ^ jnp.int32(0x7FFFFFFF), b)


_C_Q, _C_K, _C_V, _C_GA, _C_QI, _C_KW, _C_A, _C_B, _C_GC, _C_END = (
    0, 512, 1024, 1536, 2048, 2560, 2688, 3200, 3712, 4224)


def _inproj_kernel(x_ref, g_ref, w_ref, qg_ref, kg_ref, bd_ref, *outs, prompt, tm):
    xf = x_ref[0]
    ms = jnp.mean(xf * xf, axis=-1, keepdims=True)
    xn = (xf * lax.rsqrt(ms + EPS) * g_ref[...]).astype(bf16)

    def proj(a, b):
        return jnp.dot(xn, w_ref[:, a:b], preferred_element_type=f32)

    def head_norm(z, gain_ref):
        msq = jnp.dot((z * z).astype(bf16), bd_ref[...], preferred_element_type=f32)
        return z * lax.rsqrt(msq + EPS) * gain_ref[...]

    q = head_norm(proj(_C_Q, _C_K), qg_ref)
    k = head_norm(proj(_C_K, _C_V), kg_ref)
    v = proj(_C_V, _C_GA)
    ga = _silu(proj(_C_GA, _C_QI))
    qi = proj(_C_QI, _C_KW)
    kw = proj(_C_KW, _C_A)
    u = proj(_C_A, _C_B) * jax.nn.sigmoid(proj(_C_B, _C_GC))
    gc = _silu(proj(_C_GC, _C_END))

    if prompt:
        q_o, k_o, kb_o, v_o, vt_o, ga_o, qi_o, kw_o, wi_o, ki2_o, u_o, gc_o = outs
        q_o[0] = (q * (HEAD_DIM ** -0.5 * LOG2E)).T.astype(bf16)
        qi_o[0] = (qi * (IDX_DIM ** -0.5)).T.astype(bf16)
        wi_o[0] = kw.T[IDX_DIM:IDX_DIM + N_IDX_HEADS, :]
        kb_o[0] = k.astype(bf16)
        lane = lax.broadcasted_iota(jnp.int32, kw.shape, 1)
        ki2_o[0] = jnp.where(lane < IDX_DIM, kw, pltpu.roll(kw, IDX_DIM, 1)).astype(bf16)
        vt = v.T.astype(bf16)
        tail = jnp.where(lax.broadcasted_iota(jnp.int32, (PV_ROWS - 2 * HEAD_DIM, K_CHUNK), 0) < 8,
                         1.0, 0.0).astype(bf16)
        for c in range(tm // K_CHUNK):
            for i in range(N_HEADS // 2):
                blk = vt[i * 2 * HEAD_DIM:(i + 1) * 2 * HEAD_DIM, c * K_CHUNK:(c + 1) * K_CHUNK]
                vt_o[0, c, i] = jnp.concatenate([blk, tail], axis=0)
    else:
        q_o, k_o, v_o, ga_o, qi_o, kw_o, u_o, gc_o = outs
        q_o[0] = q
        qi_o[0] = qi
    k_o[0] = k
    v_o[0] = v
    ga_o[0] = ga
    kw_o[0] = kw
    u_o[0] = u
    gc_o[0] = gc


def _inproj(x, norm_g, w_all, qg, kg, bd, *, prompt):
    B, S, D = x.shape
    tm = 512 if prompt else S
    assert S % tm == 0 and (not prompt or tm % K_CHUNK == 0)
    grid = (B, S // tm)
    row = lambda b, i: (b, i, 0)
    const2 = lambda b, i: (0, 0)
    col = lambda b, i: (b, 0, i)
    nck = tm // K_CHUNK
    if prompt:
        out_shape = (
            jax.ShapeDtypeStruct((B, D_ATTN, S), bf16),
            jax.ShapeDtypeStruct((B, S, D_ATTN), f32),
            jax.ShapeDtypeStruct((B, S, D_ATTN), bf16),
            jax.ShapeDtypeStruct((B, S, D_ATTN), f32),
            jax.ShapeDtypeStruct((B, S // K_CHUNK, N_HEADS // 2, PV_ROWS, K_CHUNK), bf16),
            jax.ShapeDtypeStruct((B, S, D_ATTN), f32),
            jax.ShapeDtypeStruct((B, N_IDX_HEADS * IDX_DIM, S), bf16),
            jax.ShapeDtypeStruct((B, S, LANES), f32),
            jax.ShapeDtypeStruct((B, N_IDX_HEADS, S), f32),
            jax.ShapeDtypeStruct((B, S, LANES), bf16),
            jax.ShapeDtypeStruct((B, S, D_CONV), f32),
            jax.ShapeDtypeStruct((B, S, D_CONV), f32),
        )
        out_specs = (
            pl.BlockSpec((1, D_ATTN, tm), col),
            pl.BlockSpec((1, tm, D_ATTN), row),
            pl.BlockSpec((1, tm, D_ATTN), row),
            pl.BlockSpec((1, tm, D_ATTN), row),
            pl.BlockSpec((1, nck, N_HEADS // 2, PV_ROWS, K_CHUNK), lambda b, i: (b, i, 0, 0, 0)),
            pl.BlockSpec((1, tm, D_ATTN), row),
            pl.BlockSpec((1, N_IDX_HEADS * IDX_DIM, tm), col),
            pl.BlockSpec((1, tm, LANES), row),
            pl.BlockSpec((1, N_IDX_HEADS, tm), col),
            pl.BlockSpec((1, tm, LANES), row),
            pl.BlockSpec((1, tm, D_CONV), row),
            pl.BlockSpec((1, tm, D_CONV), row),
        )
    else:
        out_shape = (
            jax.ShapeDtypeStruct((B, S, D_ATTN), f32),
            jax.ShapeDtypeStruct((B, S, D_ATTN), f32),
            jax.ShapeDtypeStruct((B, S, D_ATTN), f32),
            jax.ShapeDtypeStruct((B, S, D_ATTN), f32),
            jax.ShapeDtypeStruct((B, S, D_ATTN), f32),
            jax.ShapeDtypeStruct((B, S, LANES), f32),
            jax.ShapeDtypeStruct((B, S, D_CONV), f32),
            jax.ShapeDtypeStruct((B, S, D_CONV), f32),
        )
        out_specs = tuple(pl.BlockSpec((1, tm, s.shape[-1]), row) for s in out_shape)
    return pl.pallas_call(
        functools.partial(_inproj_kernel, prompt=prompt, tm=tm),
        grid=grid,
        in_specs=[
            pl.BlockSpec((1, tm, D), row),
            pl.BlockSpec((1, D), const2),
            pl.BlockSpec((D, _C_END), const2),
            pl.BlockSpec((1, D_ATTN), const2),
            pl.BlockSpec((1, D_ATTN), const2),
            pl.BlockSpec((D_ATTN, D_ATTN), const2),
        ],
        out_specs=out_specs,
        out_shape=out_shape,
        compiler_params=pltpu.CompilerParams(
            dimension_semantics=("parallel", "parallel"), vmem_limit_bytes=VMEM_LIMIT),
        name="inproj_prompt" if prompt else "inproj_sample",
    )(x, norm_g, w_all, qg, kg, bd)


def _bias_table_kernel(rb_ref, out_ref):
    shape = (K_CHUNK, Q_TILE)
    krow = lax.broadcasted_iota(jnp.int32, shape, 0)
    qcol = lax.broadcasted_iota(jnp.int32, shape, 1)
    for par in range(N_PAR):
        for which in range(2):
            dist = Q_TILE * par + K_CHUNK * (1 - which) + qcol - krow
            for h in range(N_HEADS):
                far = rb_ref[N_BUCKETS - 1, h]
                acc = jnp.full(shape, (rb_ref[0, h] - far) * LOG2E, f32)
                for i, t in enumerate(BUCKET_THRESH):
                    acc = jnp.where(dist >= t, (rb_ref[i + 1, h] - far) * LOG2E, acc)
                out_ref[par, which, h // 2, :, (h % 2) * Q_TILE:(h % 2 + 1) * Q_TILE] = (
                    jnp.where(dist >= 0, acc, NEG))


def _bias_tables(rel_bias):
    return pl.pallas_call(
        _bias_table_kernel,
        in_specs=[pl.BlockSpec(memory_space=pltpu.SMEM)],
        out_specs=pl.BlockSpec(memory_space=pltpu.VMEM),
        out_shape=jax.ShapeDtypeStruct((N_PAR, 2, N_HEADS // 2, K_CHUNK, 2 * Q_TILE), f32),
        name="bias_tables",
    )(rel_bias)


N_PAIRS = N_HEADS // 2
SUB = 8


COUNT_ROWS = 32
SEARCH_BITS = 8
MAX_ROWS = 16


def _fold(x, op, rows):
    return op(x.reshape(x.shape[0] // rows, rows, x.shape[1]), axis=0)


def _attn_kernel(qt_ref, qit_ref, wit_ref, k_ref, ki_ref, vt_ref, ga_ref, bt_ref, out_ref,
                 sc_ref, key_ref, lg_ref, qbd_ref, qibd_ref, o_ref, *, ksel, idx_bits):
    j = pl.program_id(1)
    nck = j // N_PAR + 1
    par = j % N_PAR
    shape = (K_CHUNK, Q_TILE)
    key_pos = lax.broadcasted_iota(jnp.int32, shape, 0)
    q_pos = j * Q_TILE + lax.broadcasted_iota(jnp.int32, shape, 1)

    top = lax.broadcasted_iota(jnp.int32, (2 * HEAD_DIM, Q_TILE), 0) < HEAD_DIM
    for i in range(N_PAIRS):
        rows = slice(i * 2 * HEAD_DIM, (i + 1) * 2 * HEAD_DIM)
        for src, dst in ((qt_ref, qbd_ref), (qit_ref, qibd_ref)):
            x = src[0, rows, :]
            zero = jnp.zeros_like(x)
            dst[i] = jnp.concatenate([jnp.where(top, x, zero), jnp.where(top, zero, x)], axis=1)
    wi = wit_ref[0] * (N_IDX_HEADS ** -0.5)
    wpair = [jnp.concatenate([wi[2 * i:2 * i + 1], wi[2 * i + 1:2 * i + 2]], axis=1)
             for i in range(N_PAIRS)]

    def chunk_rows(c):
        return pl.ds(pl.multiple_of(c * K_CHUNK, K_CHUNK), K_CHUNK)

    def for_chunk_pairs(n, body):
        def pair(cc, carry):
            body([2 * cc, 2 * cc + 1])
            return carry
        lax.fori_loop(0, n // 2, pair, 0)

        @pl.when(n % 2 == 1)
        def _():
            body([n - 1])

    def score_chunks(chunks):
        for c in chunks:
            ki2 = ki_ref[0, chunk_rows(c), :]
            acc = jnp.zeros(shape, f32)
            for i in range(N_PAIRS):
                r = jnp.maximum(jnp.dot(ki2, qibd_ref[i], preferred_element_type=f32), 0.0) * wpair[i]
                acc = acc + r[:, :Q_TILE] + r[:, Q_TILE:]
            sc_ref[c] = jnp.where(c * K_CHUNK + key_pos <= q_pos, acc, -jnp.inf)

    for_chunk_pairs(nck, score_chunks)

    def count(pred):
        def body(c, cnt):
            return cnt + _fold(jnp.where(pred(c), 1.0, 0.0), jnp.sum, COUNT_ROWS)
        cnt = lax.fori_loop(0, nck, body, jnp.zeros((COUNT_ROWS, Q_TILE), f32))
        return jnp.sum(cnt, axis=0, keepdims=True)

    def write_mask(pred):
        def body(c, carry):
            sc_ref[c] = jnp.where(pred(c), 0.0, NEG)
            return carry
        lax.fori_loop(0, nck, body, 0)

    need_search = (j + 1) * Q_TILE > ksel

    @pl.when(jnp.logical_not(need_search))
    def _():
        write_mask(lambda c: c * K_CHUNK + key_pos < (j + 1) * Q_TILE)

    @pl.when(need_search)
    def _():
        def key_body(c, carry):
            key_ref[c] = _to_key(sc_ref[c])
            return carry
        lax.fori_loop(0, nck, key_body, 0)

        def bit_body(i, carry):
            lo, n_ge = carry
            cand = lo + jnp.left_shift(jnp.int32(1), 31 - i)
            candb = jnp.broadcast_to(cand, shape)
            tot = count(lambda c: key_ref[c] >= candb)
            ok = tot >= ksel
            return jnp.where(ok, cand, lo), jnp.where(ok, tot, n_ge)

        carry = (jnp.full((1, Q_TILE), INT_MIN, jnp.int32),
                 jnp.full((1, Q_TILE), nck * K_CHUNK, jnp.int32).astype(f32))
        for first_bit in range(0, 32, SEARCH_BITS):
            carry = lax.cond(
                jnp.max(carry[1]) > ksel,
                lambda c, first_bit=first_bit: lax.fori_loop(first_bit, first_bit + SEARCH_BITS, bit_body, c),
                lambda c: c, carry)
        thr, n_ge = carry
        thrb = jnp.broadcast_to(thr, shape)
        has_tie = jnp.max(n_ge) > ksel

        @pl.when(jnp.logical_not(has_tie))
        def _():
            write_mask(lambda c: key_ref[c] >= thrb)

        @pl.when(has_tie)
        def _():
            need = ksel - count(lambda c: key_ref[c] > thrb)

            def cut_body(i, x):
                cand = x + jnp.left_shift(jnp.int32(1), idx_bits - 1 - i)
                candb = jnp.broadcast_to(cand, shape)
                g = count(lambda c: jnp.where(key_ref[c] == thrb, c * K_CHUNK + key_pos, 2 ** 30) < candb)
                return jnp.where(g < need, cand, x)

            cut = lax.fori_loop(0, idx_bits, cut_body, jnp.zeros((1, Q_TILE), jnp.int32))
            cutb = jnp.broadcast_to(cut, shape)

            def sel(c):
                kc = key_ref[c]
                eq_idx = jnp.where(kc == thrb, c * K_CHUNK + key_pos, 2 ** 30)
                return jnp.where(kc > thrb, 0, eq_idx) <= cutb
            write_mask(sel)

    def logits_chunk(c, which, ms):
        mask = sc_ref[c]
        mask2 = jnp.concatenate([mask, mask], axis=1)
        out = []
        for i in range(N_PAIRS):
            kc = k_ref[0, chunk_rows(c), i * 2 * HEAD_DIM:(i + 1) * 2 * HEAD_DIM]
            s = jnp.dot(kc, qbd_ref[i], preferred_element_type=f32) + mask2
            if which is not None:
                s = s + bt_ref[par, which, i]
            lg_ref[i, c] = s
            out.append(jnp.maximum(ms[i], _fold(s, jnp.max, MAX_ROWS)))
        return tuple(out)

    ms = tuple(jnp.full((MAX_ROWS, 2 * Q_TILE), NEG, f32) for _ in range(N_PAIRS))
    nfar = jnp.maximum(nck - 2, 0)
    ms = lax.fori_loop(
        0, nfar // 2, lambda cc, ms: logits_chunk(2 * cc + 1, None, logits_chunk(2 * cc, None, ms)), ms)
    ms = lax.cond(nfar % 2 == 1, lambda ms: logits_chunk(nfar - 1, None, ms), lambda ms: ms, ms)
    ms = lax.cond(nck >= 2,
                  lambda ms: logits_chunk(nck - 1, 1, logits_chunk(nck - 2, 0, ms)),
                  lambda ms: logits_chunk(nck - 1, 1, ms), ms)
    mrow = [jnp.max(m, axis=0, keepdims=True) for m in ms]

    o_ref[...] = jnp.zeros(o_ref.shape, f32)

    def pv_chunks(chunks):
        for i in range(N_PAIRS):
            acc = None
            for c in chunks:
                p = jnp.exp2((lg_ref[i, c] - mrow[i]).astype(bf16))
                d = jnp.dot(vt_ref[0, c, i], p, preferred_element_type=f32)
                acc = d if acc is None else acc + d
            o_ref[i] += acc
    for_chunk_pairs(nck, pv_chunks)

    pieces = []
    for i in range(N_PAIRS):
        o = o_ref[i]
        l = o[2 * HEAD_DIM:2 * HEAD_DIM + 1, :]
        pieces.append(o[0:HEAD_DIM, 0:Q_TILE] / l[:, 0:Q_TILE])
        pieces.append(o[HEAD_DIM:2 * HEAD_DIM, Q_TILE:] / l[:, Q_TILE:])
    out_ref[0] = (jnp.concatenate(pieces, axis=0).T * ga_ref[0]).astype(bf16)


def _prompt_attention(qt, qit, wit, kb, ki2, vt, ga, btab):
    B, S, _ = kb.shape
    assert S % K_CHUNK == 0
    nq = S // Q_TILE
    nc = S // K_CHUNK
    ksel = min(TOPK_MAX, S // 4)
    qcol = lambda b, j: (b, 0, j)
    perb3 = lambda b, j: (b, 0, 0)
    return pl.pallas_call(
        functools.partial(_attn_kernel, ksel=ksel, idx_bits=max(1, (S - 1).bit_length())),
        grid=(B, nq),
        in_specs=[
            pl.BlockSpec((1, D_ATTN, Q_TILE), qcol),
            pl.BlockSpec((1, N_IDX_HEADS * IDX_DIM, Q_TILE), qcol),
            pl.BlockSpec((1, N_IDX_HEADS, Q_TILE), qcol),
            pl.BlockSpec((1, S, D_ATTN), perb3),
            pl.BlockSpec((1, S, LANES), perb3),
            pl.BlockSpec((1, nc, N_PAIRS, PV_ROWS, K_CHUNK), lambda b, j: (b, 0, 0, 0, 0)),
            pl.BlockSpec((1, Q_TILE, D_ATTN), lambda b, j: (b, j, 0)),
            pl.BlockSpec((N_PAR, 2, N_PAIRS, K_CHUNK, 2 * Q_TILE), lambda b, j: (0, 0, 0, 0, 0)),
        ],
        out_specs=pl.BlockSpec((1, Q_TILE, D_ATTN), lambda b, j: (b, j, 0)),
        out_shape=jax.ShapeDtypeStruct((B, S, D_ATTN), bf16),
        scratch_shapes=[
            pltpu.VMEM((nc, K_CHUNK, Q_TILE), f32),
            pltpu.VMEM((nc, K_CHUNK, Q_TILE), jnp.int32),
            pltpu.VMEM((N_PAIRS, nc, K_CHUNK, 2 * Q_TILE), f32),
            pltpu.VMEM((N_PAIRS, 2 * HEAD_DIM, 2 * Q_TILE), bf16),
            pltpu.VMEM((N_PAIRS, 2 * IDX_DIM, 2 * Q_TILE), bf16),
            pltpu.VMEM((N_PAIRS, PV_ROWS, 2 * Q_TILE), f32),
        ],
        compiler_params=pltpu.CompilerParams(
            dimension_semantics=("parallel", "arbitrary"), vmem_limit_bytes=VMEM_LIMIT),
        name="prompt_attention",
    )(qt, qit, wit, kb, ki2, vt, ga, btab)


CONV_HALO = 32
CONV_ROWS = 32


def _layer_norm_swish(c, lng_ref, lnb_ref):
    mu = jnp.mean(c, axis=-1, keepdims=True)
    d = c - mu
    var = jnp.mean(d * d, axis=-1, keepdims=True)
    return _silu(d * lax.rsqrt(var + EPS) * lng_ref[...] + lnb_ref[...])


def _merge_kernel(mixa_ref, ucur_ref, uprev_ref, gc_ref, x_ref, wo_ref, dw_ref, dwb_ref, lng_ref,
                  lnb_ref, y_ref, ext_ref, sh_ref, cg_ref, *, tq):
    i = pl.program_id(1)
    n = CONV_HALO + tq
    ext_ref[0:CONV_HALO] = jnp.where(i > 0, uprev_ref[0], 0.0)
    ext_ref[CONV_HALO:n] = ucur_ref[0]
    ext_ref[n:n + SUB] = jnp.zeros((SUB, D_CONV), f32)
    ext = ext_ref[...]
    for r in range(1, SUB):
        sh_ref[r - 1] = pltpu.roll(ext, n + SUB - r, 0)[0:n, :]
    first = CONV_HALO - (CONV_W - 1)
    for c in range(tq // CONV_ROWS):
        acc = jnp.broadcast_to(dwb_ref[...], (CONV_ROWS, D_CONV))
        for w in range(CONV_W):
            a, r = divmod(first + w, SUB)
            start = c * CONV_ROWS + a * SUB
            src = ext_ref if r == 0 else sh_ref.at[r - 1]
            acc = acc + src[start:start + CONV_ROWS, :] * pltpu.repeat(dw_ref[w], CONV_ROWS // SUB, 0)
        rows = slice(c * CONV_ROWS, (c + 1) * CONV_ROWS)
        cg_ref[rows] = (_layer_norm_swish(acc, lng_ref, lnb_ref) * gc_ref[0, rows]).astype(bf16)
    y_ref[0] = (x_ref[0]
                + jnp.dot(mixa_ref[0], wo_ref[0:D_ATTN], preferred_element_type=f32)
                + jnp.dot(cg_ref[...], wo_ref[D_ATTN:D_ATTN + D_CONV], preferred_element_type=f32))


def _prompt_merge(mixa, u, gc, x, wo, dw, dwb, lng, lnb):
    B, S, D = x.shape
    tq = 256
    assert S % tq == 0 and tq % CONV_HALO == 0
    row = lambda b, i: (b, i, 0)
    const2 = lambda b, i: (0, 0)
    halo = lambda b, i: (b, jnp.maximum(i * (tq // CONV_HALO) - 1, 0), 0)
    return pl.pallas_call(
        functools.partial(_merge_kernel, tq=tq),
        grid=(B, S // tq),
        in_specs=[
            pl.BlockSpec((1, tq, D_ATTN), row),
            pl.BlockSpec((1, tq, D_CONV), row),
            pl.BlockSpec((1, CONV_HALO, D_CONV), halo),
            pl.BlockSpec((1, tq, D_CONV), row),
            pl.BlockSpec((1, tq, D), row),
            pl.BlockSpec((D_ATTN + D_CONV, D), const2),
            pl.BlockSpec((CONV_W, SUB, D_CONV), lambda b, i: (0, 0, 0)),
            pl.BlockSpec((1, D_CONV), const2),
            pl.BlockSpec((1, D_CONV), const2),
            pl.BlockSpec((1, D_CONV), const2),
        ],
        out_specs=pl.BlockSpec((1, tq, D), row),
        out_shape=jax.ShapeDtypeStruct((B, S, D), f32),
        scratch_shapes=[
            pltpu.VMEM((CONV_HALO + tq + SUB, D_CONV), f32),
            pltpu.VMEM((SUB - 1, CONV_HALO + tq, D_CONV), f32),
            pltpu.VMEM((tq, D_CONV), bf16),
        ],
        compiler_params=pltpu.CompilerParams(
            dimension_semantics=("parallel", "arbitrary"), vmem_limit_bytes=VMEM_LIMIT),
        name="prompt_merge",
    )(mixa, u, u, gc, x, wo, dw, dwb, lng, lnb)


SCORE_UNROLL = 16


def _sample_score_kernel(pt_ref, qi_ref, w_ref, cki_ref, out_ref, buf_ref, sem_ref, *, npg):
    b = pl.program_id(0)
    nb = pl.num_programs(0)
    slot = b % 2

    def page_copy(bb, p, sl):
        return pltpu.make_async_copy(cki_ref.at[pt_ref[bb, p]], buf_ref.at[sl, p], sem_ref.at[sl])

    def start_all(bb, sl):
        def body(p, carry):
            page_copy(bb, p, sl).start()
            return carry
        lax.fori_loop(0, npg, body, 0)

    @pl.when(b == 0)
    def _():
        start_all(0, 0)

    @pl.when(b + 1 < nb)
    def _():
        start_all(b + 1, 1 - slot)

    def wait_body(p, carry):
        page_copy(b, p, slot).wait()
        return carry
    lax.fori_loop(0, npg, wait_body, 0)

    qs = (qi_ref[0] * (IDX_DIM ** -0.5)).astype(bf16)
    wcol = w_ref[0] * (N_IDX_HEADS ** -0.5)
    wcol2 = jnp.concatenate([wcol, wcol], axis=1)

    def body(i, carry):
        for u in range(0, SCORE_UNROLL, 2):
            p = i * SCORE_UNROLL + u
            pages = jnp.concatenate([buf_ref[slot, p], buf_ref[slot, p + 1]], axis=1).astype(bf16)
            s = jnp.dot(qs, pages, preferred_element_type=f32)
            r = jnp.sum(jnp.maximum(s, 0.0) * wcol2, axis=0, keepdims=True)
            out_ref[0, pl.ds(p, 1), :] = r[:, :LANES]
            out_ref[0, pl.ds(p + 1, 1), :] = r[:, LANES:]
        return carry
    lax.fori_loop(0, npg // SCORE_UNROLL, body, 0)


def _sample_scores(page_table, qi3, wrow, cki_t):
    Bd, npg = page_table.shape
    page = cki_t.shape[2]
    assert page == LANES and npg % SCORE_UNROLL == 0
    grid_spec = pltpu.PrefetchScalarGridSpec(
        num_scalar_prefetch=1,
        grid=(Bd,),
        in_specs=[
            pl.BlockSpec((1, N_IDX_HEADS, IDX_DIM), lambda b, pt: (b, 0, 0)),
            pl.BlockSpec((1, N_IDX_HEADS, LANES), lambda b, pt: (b, 0, 0)),
            pl.BlockSpec(memory_space=pl.ANY),
        ],
        out_specs=pl.BlockSpec((1, npg, page), lambda b, pt: (b, 0, 0)),
        scratch_shapes=[
            pltpu.VMEM((2, npg, IDX_DIM, page), f32),
            pltpu.SemaphoreType.DMA((2,)),
        ],
    )
    return pl.pallas_call(
        functools.partial(_sample_score_kernel, npg=npg),
        grid_spec=grid_spec,
        out_shape=jax.ShapeDtypeStruct((Bd, npg, page), f32),
        compiler_params=pltpu.CompilerParams(
            dimension_semantics=("arbitrary",), vmem_limit_bytes=VMEM_LIMIT),
        name="sample_scores",
    )(page_table, qi3, wrow, cki_t)


def _sample_select_kernel(sc_ref, qi_ref, kit_ref, kw_ref, mask_ref, mnew_ref, key_ref, *,
                          npg, ksel, past, idx_bits):
    Bd = sc_ref.shape[0]
    shape = (Bd, npg, LANES)

    qb = (qi_ref[...] * (IDX_DIM ** -0.5)).astype(bf16).astype(f32)
    kb = kit_ref[...].astype(bf16).astype(f32)
    prod = qb * kb
    grp = lax.broadcasted_iota(jnp.int32, prod.shape, 1) // IDX_DIM
    kwv = kw_ref[...]
    s_new = jnp.zeros((Bd, 1), f32)
    for h in range(N_IDX_HEADS):
        sh = jnp.sum(jnp.where(grp == h, prod, 0.0), axis=1, keepdims=True)
        s_new = s_new + jnp.maximum(sh, 0.0) * (kwv[:, IDX_DIM + h:IDX_DIM + h + 1] * (N_IDX_HEADS ** -0.5))
    key_new = _to_key(s_new).reshape(Bd, 1, 1)

    key_ref[...] = _to_key(sc_ref[...])
    pos = (lax.broadcasted_iota(jnp.int32, shape, 1) * LANES
           + lax.broadcasted_iota(jnp.int32, shape, 2))

    def count(pred, pred_new):
        x = jnp.where(pred, 1.0, 0.0)
        s = jnp.sum(jnp.sum(x, axis=1, keepdims=True), axis=2, keepdims=True)
        return s + jnp.where(pred_new, 1.0, 0.0)

    def bit_body(i, lo):
        cand = lo + jnp.left_shift(jnp.int32(1), 31 - i)
        tot = count(key_ref[...] >= cand, key_new >= cand)
        return jnp.where(tot >= ksel, cand, lo)

    thr = lax.fori_loop(0, 32, bit_body, jnp.full((Bd, 1, 1), INT_MIN, jnp.int32))
    keys = key_ref[...]
    need = ksel - count(keys > thr, key_new > thr)
    big = jnp.int32(2 ** 30)
    eq_pos = jnp.where(keys == thr, pos, big)
    eq_pos_new = jnp.where(key_new == thr, jnp.int32(past), big)

    def cut_body(i, x):
        cand = x + jnp.left_shift(jnp.int32(1), idx_bits - 1 - i)
        g = count(eq_pos < cand, eq_pos_new < cand)
        return jnp.where(g < need, cand, x)

    cut = lax.fori_loop(0, idx_bits, cut_body, jnp.zeros((Bd, 1, 1), jnp.int32))
    mask_ref[...] = jnp.where(jnp.where(keys > thr, 0, eq_pos) <= cut, 0.0, NEG)
    sel_new = jnp.where(key_new > thr, 0, eq_pos_new) <= cut
    mnew_ref[...] = jnp.broadcast_to(jnp.where(sel_new, 0.0, NEG), mnew_ref.shape)


def _sample_select(scores, qi, ki_tiled, kw, *, past):
    Bd, npg, _ = scores.shape
    ksel = min(TOPK_MAX, (past + 1) // 4)
    vm = pl.BlockSpec(memory_space=pltpu.VMEM)
    return pl.pallas_call(
        functools.partial(_sample_select_kernel, npg=npg, ksel=ksel, past=past,
                          idx_bits=past.bit_length()),
        in_specs=[vm, vm, vm, vm],
        out_specs=(vm, vm),
        out_shape=(jax.ShapeDtypeStruct((Bd, npg, LANES), f32),
                   jax.ShapeDtypeStruct((Bd, N_HEADS, LANES), f32)),
        scratch_shapes=[pltpu.VMEM((Bd, npg, LANES), jnp.int32)],
        compiler_params=pltpu.CompilerParams(vmem_limit_bytes=VMEM_LIMIT),
        name="sample_select",
    )(scores, qi, ki_tiled, kw)


PAGE_BUFS = 32
PAGE_GROUP = 4


def _sample_attn_kernel(pt_ref, qbd_ref, knew_ref, vnew_ref, mask_ref, mnew_ref, rbt_ref, ck_ref, cv_ref,
                        out_ref, buf_ref, lg_ref, sem_ref, *, npg, past):
    b = pl.program_id(0)
    nb = pl.num_programs(0)
    per_b = 2 * npg
    total = nb * per_b

    def page_copy(g, src_ref):
        phys = pt_ref[g // per_b, g % npg]
        slot = g % PAGE_BUFS
        return pltpu.make_async_copy(src_ref.at[phys], buf_ref.at[slot], sem_ref.at[slot])

    def start(g):
        is_k = g % per_b < npg

        @pl.when(is_k)
        def _():
            page_copy(g, ck_ref).start()

        @pl.when(jnp.logical_not(is_k))
        def _():
            page_copy(g, cv_ref).start()

    def wait(g):
        page_copy(g, ck_ref).wait()

    @pl.when(b == 0)
    def _():
        for g in range(PAGE_BUFS):
            start(g)

    g0 = b * per_b
    qbd = (qbd_ref[0] * (HEAD_DIM ** -0.5 * LOG2E)).astype(bf16)
    blk = (lax.broadcasted_iota(jnp.int32, (N_HEADS, D_ATTN), 0)
           == lax.broadcasted_iota(jnp.int32, (N_HEADS, D_ATTN), 1) // HEAD_DIM)
    tok = lax.broadcasted_iota(jnp.int32, (N_HEADS, LANES), 1)

    def bias_of(dist):
        far = rbt_ref[N_BUCKETS - 1]
        acc = jnp.broadcast_to(rbt_ref[0] - far, dist.shape)
        for i, t in enumerate(BUCKET_THRESH):
            acc = jnp.where(dist >= t, rbt_ref[i + 1] - far, acc)
        return acc * LOG2E

    def refill(gs):
        for g in gs:
            @pl.when(g + PAGE_BUFS < total)
            def _():
                start(g + PAGE_BUFS)

    def k_body(it, m):
        gs = [g0 + it * PAGE_GROUP + u for u in range(PAGE_GROUP)]
        for g in gs:
            wait(g)
        for u, g in enumerate(gs):
            p = it * PAGE_GROUP + u
            kp = buf_ref[g % PAGE_BUFS].astype(bf16)
            s = jnp.dot(qbd, kp, preferred_element_type=f32) + mask_ref[0, pl.ds(p, 1), :]
            lg_ref[p] = s
            m = jnp.maximum(m, s)
        refill(gs)
        return m

    m = lax.fori_loop(0, npg // PAGE_GROUP, k_body, jnp.full((N_HEADS, LANES), NEG, f32))
    last = lg_ref[npg - 1] + bias_of(past - ((npg - 1) * LANES + tok))
    lg_ref[npg - 1] = last
    m = jnp.maximum(m, last)

    s_new = (jnp.sum(qbd_ref[0] * knew_ref[0], axis=1, keepdims=True) * (HEAD_DIM ** -0.5 * LOG2E)
             + bias_of(jnp.zeros((N_HEADS, LANES), jnp.int32))[:, 0:1] + mnew_ref[0][:, 0:1])
    m_row = jnp.maximum(jnp.max(m, axis=1, keepdims=True), s_new)
    mb = jnp.broadcast_to(m_row, (N_HEADS, LANES))

    def v_body(it, carry):
        acc, l = carry
        gs = [g0 + npg + it * PAGE_GROUP + u for u in range(PAGE_GROUP)]
        for g in gs:
            wait(g)
        for u, g in enumerate(gs):
            vp = buf_ref[g % PAGE_BUFS].astype(bf16)
            pr = jnp.exp2(lg_ref[it * PAGE_GROUP + u] - mb)
            acc = acc + lax.dot_general(pr.astype(bf16), vp, (((1,), (1,)), ((), ())),
                                        preferred_element_type=f32)
            l = l + pr
        refill(gs)
        return acc, l

    acc, l = lax.fori_loop(0, npg // PAGE_GROUP, v_body, (jnp.zeros((N_HEADS, D_ATTN), f32),
                                                          jnp.zeros((N_HEADS, LANES), f32)))
    p_new = jnp.exp2(s_new - m_row)
    l_tot = jnp.sum(l, axis=1, keepdims=True) + p_new
    num = jnp.where(blk, acc + p_new * vnew_ref[0], 0.0) / l_tot
    out_ref[0] = jnp.sum(num, axis=0, keepdims=True)


def _sample_attention(page_table, qbd, k_new_flat, v_new_flat, mask, mnew, rbt, ck_t, cv_t, *, past):
    Bd, npg = page_table.shape
    rows, page = ck_t.shape[1], ck_t.shape[2]
    assert rows == D_ATTN and page == LANES
    assert npg % PAGE_GROUP == 0 and PAGE_BUFS % PAGE_GROUP == 0 and 2 * npg >= PAGE_BUFS
    grid_spec = pltpu.PrefetchScalarGridSpec(
        num_scalar_prefetch=1,
        grid=(Bd,),
        in_specs=[
            pl.BlockSpec((1, N_HEADS, D_ATTN), lambda b, pt: (b, 0, 0)),
            pl.BlockSpec((1, 1, D_ATTN), lambda b, pt: (b, 0, 0)),
            pl.BlockSpec((1, 1, D_ATTN), lambda b, pt: (b, 0, 0)),
            pl.BlockSpec((1, npg, LANES), lambda b, pt: (b, 0, 0)),
            pl.BlockSpec((1, N_HEADS, LANES), lambda b, pt: (b, 0, 0)),
            pl.BlockSpec((N_BUCKETS, N_HEADS, LANES), lambda b, pt: (0, 0, 0)),
            pl.BlockSpec(memory_space=pl.ANY),
            pl.BlockSpec(memory_space=pl.ANY),
        ],
        out_specs=pl.BlockSpec((1, 1, D_ATTN), lambda b, pt: (b, 0, 0)),
        scratch_shapes=[
            pltpu.VMEM((PAGE_BUFS, rows, page), f32),
            pltpu.VMEM((npg, N_HEADS, LANES), f32),
            pltpu.SemaphoreType.DMA((PAGE_BUFS,)),
        ],
    )
    return pl.pallas_call(
        functools.partial(_sample_attn_kernel, npg=npg, past=past),
        grid_spec=grid_spec,
        out_shape=jax.ShapeDtypeStruct((Bd, 1, D_ATTN), f32),
        compiler_params=pltpu.CompilerParams(
            dimension_semantics=("arbitrary",), vmem_limit_bytes=VMEM_LIMIT),
        name="sample_attention",
    )(page_table, qbd, k_new_flat, v_new_flat, mask, mnew, rbt, ck_t, cv_t)


def _sample_merge_kernel(attn_ref, ga_ref, u_ref, gc_ref, st_ref, x_ref, wo_ref, dw_ref, dwb_ref,
                         lng_ref, lnb_ref, y_ref):
    acc = dwb_ref[...] + u_ref[...] * dw_ref[CONV_W - 1:CONV_W, :]
    for w in range(CONV_W - 1):
        acc = acc + st_ref[w] * dw_ref[w:w + 1, :]
    cg = (_layer_norm_swish(acc, lng_ref, lnb_ref) * gc_ref[...]).astype(bf16)
    mixa = (attn_ref[...] * ga_ref[...]).astype(bf16)
    y_ref[...] = (x_ref[...]
                  + jnp.dot(mixa, wo_ref[0:D_ATTN], preferred_element_type=f32)
                  + jnp.dot(cg, wo_ref[D_ATTN:D_ATTN + D_CONV], preferred_element_type=f32))


def _sample_merge(attn, ga, u, gc, state_t, x, wo, dw, dwb, lng, lnb):
    vm = pl.BlockSpec(memory_space=pltpu.VMEM)
    return pl.pallas_call(
        _sample_merge_kernel,
        in_specs=[vm] * 11,
        out_specs=vm,
        out_shape=jax.ShapeDtypeStruct(x.shape, f32),
        compiler_params=pltpu.CompilerParams(vmem_limit_bytes=VMEM_LIMIT),
        name="sample_merge",
    )(attn, ga, u, gc, state_t, x, wo, dw, dwb, lng, lnb)


def kernel(x_prompt, x_sample, cache_k, cache_v, cache_k_idx, state_conv, page_table, rel_bias, norm_g,
           w_in, q_norm_g, k_norm_g, dw_w, dw_b, ln_g, ln_b, w_out):
    depth = norm_g.shape[0]
    assert depth == 1, "single-layer step"
    B, S, D = x_prompt.shape
    Bd, T, _ = x_sample.shape
    assert T == 1
    n_phys, page = cache_k.shape[1], cache_k.shape[2]
    past = page_table.shape[1] * page

    w = w_in[0]
    c_ki = 2 * D_ATTN + 2 * N_HEADS * HEAD_DIM + N_IDX_HEADS * IDX_DIM
    c_conv = c_ki + IDX_DIM + N_IDX_HEADS
    pad = jnp.zeros((D, LANES - IDX_DIM - N_IDX_HEADS), w.dtype)
    w_all = jnp.concatenate([w[:, :c_ki], w[:, c_ki:c_conv], pad, w[:, c_conv:]], axis=1).astype(bf16)
    assert w_all.shape[1] == _C_END
    wo = w_out[0].astype(bf16)
    g = norm_g[0][None]
    qg = jnp.tile(q_norm_g[0], N_HEADS)[None]
    kg = jnp.tile(k_norm_g[0], N_HEADS)[None]
    hid = np.arange(D_ATTN) // HEAD_DIM
    bd = jnp.asarray((hid[:, None] == hid[None, :]) / HEAD_DIM, dtype=bf16)
    dwb, lng, lnb = dw_b[0][None], ln_g[0][None], ln_b[0][None]

    (qt, k_p, kb, v_p, vt, ga_p, qit, kw_p, wit, ki2, u_p, gc_p) = _inproj(
        x_prompt, g, w_all, qg, kg, bd, prompt=True)
    btab = _bias_tables(rel_bias)
    mixa = _prompt_attention(qt, qit, wit, kb, ki2, vt, ga_p, btab)
    dw8 = jnp.broadcast_to(dw_w[0][:, None, :], (CONV_W, SUB, D_CONV))
    y_prompt = _prompt_merge(mixa, u_p, gc_p, x_prompt, wo, dw8, dwb, lng, lnb)

    xs = x_sample.reshape(1, Bd, D)
    q_s, k_s, v_s, ga_s, qi_s, kw_s, u_s, gc_s = (a[0] for a in _inproj(
        xs, g, w_all, qg, kg, bd, prompt=False))
    ki_s = kw_s[:, :IDX_DIM]
    wrow = jnp.broadcast_to(kw_s[:, IDX_DIM:IDX_DIM + N_IDX_HEADS, None], (Bd, N_IDX_HEADS, LANES))
    cki_t = jnp.transpose(cache_k_idx[0], (0, 2, 1))
    ck_t = jnp.transpose(cache_k[0], (0, 2, 3, 1)).reshape(n_phys, D_ATTN, page)
    cv_t = jnp.transpose(cache_v[0], (0, 2, 3, 1)).reshape(n_phys, D_ATTN, page)
    scores = _sample_scores(page_table, qi_s.reshape(Bd, N_IDX_HEADS, IDX_DIM), wrow, cki_t)
    mask, mnew = _sample_select(scores, qi_s, jnp.tile(ki_s, (1, N_IDX_HEADS)), kw_s, past=past)
    rbt = jnp.broadcast_to(rel_bias[:, :, None], (N_BUCKETS, N_HEADS, LANES))
    eye = jnp.asarray(np.arange(N_HEADS)[:, None] == hid[None, :], dtype=f32)
    qbd = q_s[:, None, :] * eye[None]
    attn_s = _sample_attention(page_table, qbd, k_s[:, None, :], v_s[:, None, :],
                               mask, mnew, rbt, ck_t, cv_t, past=past)
    state_t = jnp.transpose(state_conv[0], (1, 0, 2))
    y_sample = _sample_merge(attn_s[:, 0], ga_s, u_s, gc_s, state_t, x_sample[:, 0], wo,
                             dw_w[0], dwb, lng, lnb)

    heads = (N_HEADS, HEAD_DIM)
    return (
        y_prompt,
        y_sample[:, None],
        k_p.reshape((1, B, S) + heads),
        v_p.reshape((1, B, S) + heads),
        kw_p[None, :, :, :IDX_DIM],
        u_p[None, :, S - (CONV_W - 1):],
        k_s.reshape((1, Bd, 1) + heads),
        v_s.reshape((1, Bd, 1) + heads),
        ki_s.reshape(1, Bd, 1, IDX_DIM),
        jnp.concatenate([state_conv[0][:, 1:], u_s[:, None]], axis=1)[None],
    )
```

```python
import functools
import math

import numpy as np
import jax
import jax.numpy as jnp
from jax import lax
from jax.experimental import pallas as pl
from jax.experimental.pallas import tpu as pltpu

N_HEADS = 8
HEAD_DIM = 64
N_IDX_HEADS = 8
IDX_DIM = 64
D_ATTN = N_HEADS * HEAD_DIM
D_CONV = 512
CONV_W = 31
TOPK_MAX = 256
N_BUCKETS = 32
MAX_DISTANCE = 128
EPS = 1e-6
NEG = -1e30
INT_MIN = -2 ** 31
LOG2E = math.log2(math.e)

LANES = 128
Q_TILE = 256
K_CHUNK = 256
PV_ROWS = 144
VMEM_LIMIT = 56 * 1024 * 1024

f32 = jnp.float32
bf16 = jnp.bfloat16


def _bucket_thresholds():
    n = np.arange(0, 4 * MAX_DISTANCE)
    max_exact = N_BUCKETS // 2
    nf = np.maximum(n, 1).astype(np.float32)
    large = max_exact + (np.log(nf / np.float32(max_exact)) / np.float32(math.log(MAX_DISTANCE / max_exact))
                         * np.float32(N_BUCKETS - max_exact)).astype(np.int32)
    bucket = np.where(n < max_exact, n, np.minimum(large, N_BUCKETS - 1))
    return [int(n[bucket >= i].min()) for i in range(1, N_BUCKETS)]


BUCKET_THRESH = _bucket_thresholds()
assert BUCKET_THRESH[-1] <= K_CHUNK + 1 and K_CHUNK % Q_TILE == 0
N_PAR = K_CHUNK // Q_TILE


def _silu(x):
    return x * jax.nn.sigmoid(x)


def _to_key(x):
    b = pltpu.bitcast(x, jnp.int32)
    return jnp.where(b < 0, b ^ jnp.int32(0x7FFFFFFF), b)


_C_Q, _C_K, _C_V, _C_GA, _C_QI, _C_KW, _C_A, _C_B, _C_GC, _C_END = (
    0, 512, 1024, 1536, 2048, 2560, 2688, 3200, 3712, 4224)


def _inproj_kernel(x_ref, g_ref, w_ref, qg_ref, kg_ref, bd_ref, *outs, prompt, tm):
    xf = x_ref[0]
    ms = jnp.mean(xf * xf, axis=-1, keepdims=True)
    xn = (xf * lax.rsqrt(ms + EPS) * g_ref[...]).astype(bf16)

    def proj(a, b):
        return jnp.dot(xn, w_ref[:, a:b], preferred_element_type=f32)

    def head_norm(z, gain_ref):
        msq = jnp.dot((z * z).astype(bf16), bd_ref[...], preferred_element_type=f32)
        return z * lax.rsqrt(msq + EPS) * gain_ref[...]

    q = head_norm(proj(_C_Q, _C_K), qg_ref)
    k = head_norm(proj(_C_K, _C_V), kg_ref)
    v = proj(_C_V, _C_GA)
    ga = _silu(proj(_C_GA, _C_QI))
    qi = proj(_C_QI, _C_KW)
    kw = proj(_C_KW, _C_A)
    u = proj(_C_A, _C_B) * jax.nn.sigmoid(proj(_C_B, _C_GC))
    gc = _silu(proj(_C_GC, _C_END))

    if prompt:
        q_o, k_o, kb_o, v_o, vt_o, ga_o, qi_o, kw_o, wi_o, ki2_o, u_o, gc_o = outs
        q_o[0] = (q * (HEAD_DIM ** -0.5 * LOG2E)).T.astype(bf16)
        qi_o[0] = (qi * (IDX_DIM ** -0.5)).T.astype(bf16)
        wi_o[0] = kw.T[IDX_DIM:IDX_DIM + N_IDX_HEADS, :]
        kb_o[0] = k.astype(bf16)
        lane = lax.broadcasted_iota(jnp.int32, kw.shape, 1)
        ki2_o[0] = jnp.where(lane < IDX_DIM, kw, pltpu.roll(kw, IDX_DIM, 1)).astype(bf16)
        vt = v.T.astype(bf16)
        tail = jnp.where(lax.broadcasted_iota(jnp.int32, (PV_ROWS - 2 * HEAD_DIM, K_CHUNK), 0) < 8,
                         1.0, 0.0).astype(bf16)
        for c in range(tm // K_CHUNK):
            for i in range(N_HEADS // 2):
                blk = vt[i * 2 * HEAD_DIM:(i + 1) * 2 * HEAD_DIM, c * K_CHUNK:(c + 1) * K_CHUNK]
                vt_o[0, c, i] = jnp.concatenate([blk, tail], axis=0)
    else:
        q_o, k_o, v_o, ga_o, qi_o, kw_o, u_o, gc_o = outs
        q_o[0] = q
        qi_o[0] = qi
    k_o[0] = k
    v_o[0] = v
    ga_o[0] = ga
    kw_o[0] = kw
    u_o[0] = u
    gc_o[0] = gc


def _inproj(x, norm_g, w_all, qg, kg, bd, *, prompt):
    B, S, D = x.shape
    tm = 512 if prompt else S
    assert S % tm == 0 and (not prompt or tm % K_CHUNK == 0)
    grid = (B, S // tm)
    row = lambda b, i: (b, i, 0)
    const2 = lambda b, i: (0, 0)
    col = lambda b, i: (b, 0, i)
    nck = tm // K_CHUNK
    if prompt:
        out_shape = (
            jax.ShapeDtypeStruct((B, D_ATTN, S), bf16),
            jax.ShapeDtypeStruct((B, S, D_ATTN), f32),
            jax.ShapeDtypeStruct((B, S, D_ATTN), bf16),
            jax.ShapeDtypeStruct((B, S, D_ATTN), f32),
            jax.ShapeDtypeStruct((B, S // K_CHUNK, N_HEADS // 2, PV_ROWS, K_CHUNK), bf16),
            jax.ShapeDtypeStruct((B, S, D_ATTN), f32),
            jax.ShapeDtypeStruct((B, N_IDX_HEADS * IDX_DIM, S), bf16),
            jax.ShapeDtypeStruct((B, S, LANES), f32),
            jax.ShapeDtypeStruct((B, N_IDX_HEADS, S), f32),
            jax.ShapeDtypeStruct((B, S, LANES), bf16),
            jax.ShapeDtypeStruct((B, S, D_CONV), f32),
            jax.ShapeDtypeStruct((B, S, D_CONV), f32),
        )
        out_specs = (
            pl.BlockSpec((1, D_ATTN, tm), col),
            pl.BlockSpec((1, tm, D_ATTN), row),
            pl.BlockSpec((1, tm, D_ATTN), row),
            pl.BlockSpec((1, tm, D_ATTN), row),
            pl.BlockSpec((1, nck, N_HEADS // 2, PV_ROWS, K_CHUNK), lambda b, i: (b, i, 0, 0, 0)),
            pl.BlockSpec((1, tm, D_ATTN), row),
            pl.BlockSpec((1, N_IDX_HEADS * IDX_DIM, tm), col),
            pl.BlockSpec((1, tm, LANES), row),
            pl.BlockSpec((1, N_IDX_HEADS, tm), col),
            pl.BlockSpec((1, tm, LANES), row),
            pl.BlockSpec((1, tm, D_CONV), row),
            pl.BlockSpec((1, tm, D_CONV), row),
        )
    else:
        out_shape = (
            jax.ShapeDtypeStruct((B, S, D_ATTN), f32),
            jax.ShapeDtypeStruct((B, S, D_ATTN), f32),
            jax.ShapeDtypeStruct((B, S, D_ATTN), f32),
            jax.ShapeDtypeStruct((B, S, D_ATTN), f32),
            jax.ShapeDtypeStruct((B, S, D_ATTN), f32),
            jax.ShapeDtypeStruct((B, S, LANES), f32),
            jax.ShapeDtypeStruct((B, S, D_CONV), f32),
            jax.ShapeDtypeStruct((B, S, D_CONV), f32),
        )
        out_specs = tuple(pl.BlockSpec((1, tm, s.shape[-1]), row) for s in out_shape)
    return pl.pallas_call(
        functools.partial(_inproj_kernel, prompt=prompt, tm=tm),
        grid=grid,
        in_specs=[
            pl.BlockSpec((1, tm, D), row),
            pl.BlockSpec((1, D), const2),
            pl.BlockSpec((D, _C_END), const2),
            pl.BlockSpec((1, D_ATTN), const2),
            pl.BlockSpec((1, D_ATTN), const2),
            pl.BlockSpec((D_ATTN, D_ATTN), const2),
        ],
        out_specs=out_specs,
        out_shape=out_shape,
        compiler_params=pltpu.CompilerParams(
            dimension_semantics=("parallel", "parallel"), vmem_limit_bytes=VMEM_LIMIT),
        name="inproj_prompt" if prompt else "inproj_sample",
    )(x, norm_g, w_all, qg, kg, bd)


def _bias_table_kernel(rb_ref, out_ref):
    shape = (K_CHUNK, Q_TILE)
    krow = lax.broadcasted_iota(jnp.int32, shape, 0)
    qcol = lax.broadcasted_iota(jnp.int32, shape, 1)
    for par in range(N_PAR):
        for which in range(2):
            dist = Q_TILE * par + K_CHUNK * (1 - which) + qcol - krow
            for h in range(N_HEADS):
                far = rb_ref[N_BUCKETS - 1, h]
                acc = jnp.full(shape, (rb_ref[0, h] - far) * LOG2E, f32)
                for i, t in enumerate(BUCKET_THRESH):
                    acc = jnp.where(dist >= t, (rb_ref[i + 1, h] - far) * LOG2E, acc)
                out_ref[par, which, h // 2, :, (h % 2) * Q_TILE:(h % 2 + 1) * Q_TILE] = (
                    jnp.where(dist >= 0, acc, NEG))


def _bias_tables(rel_bias):
    return pl.pallas_call(
        _bias_table_kernel,
        in_specs=[pl.BlockSpec(memory_space=pltpu.SMEM)],
        out_specs=pl.BlockSpec(memory_space=pltpu.VMEM),
        out_shape=jax.ShapeDtypeStruct((N_PAR, 2, N_HEADS // 2, K_CHUNK, 2 * Q_TILE), f32),
        name="bias_tables",
    )(rel_bias)


N_PAIRS = N_HEADS // 2
SUB = 8


COUNT_ROWS = 32
SEARCH_BITS = 4
MAX_ROWS = 16


def _fold(x, op, rows):
    return op(x.reshape(x.shape[0] // rows, rows, x.shape[1]), axis=0)


def _attn_kernel(qt_ref, qit_ref, wit_ref, k_ref, ki_ref, vt_ref, ga_ref, bt_ref, out_ref,
                 sc_ref, key_ref, lg_ref, qbd_ref, qibd_ref, o_ref, *, ksel, idx_bits):
    j = pl.program_id(1)
    nck = j // N_PAR + 1
    par = j % N_PAR
    shape = (K_CHUNK, Q_TILE)
    key_pos = lax.broadcasted_iota(jnp.int32, shape, 0)
    q_pos = j * Q_TILE + lax.broadcasted_iota(jnp.int32, shape, 1)

    top = lax.broadcasted_iota(jnp.int32, (2 * HEAD_DIM, Q_TILE), 0) < HEAD_DIM
    for i in range(N_PAIRS):
        rows = slice(i * 2 * HEAD_DIM, (i + 1) * 2 * HEAD_DIM)
        for src, dst in ((qt_ref, qbd_ref), (qit_ref, qibd_ref)):
            x = src[0, rows, :]
            zero = jnp.zeros_like(x)
            dst[i] = jnp.concatenate([jnp.where(top, x, zero), jnp.where(top, zero, x)], axis=1)
    wi = wit_ref[0] * (N_IDX_HEADS ** -0.5)
    wpair = [jnp.concatenate([wi[2 * i:2 * i + 1], wi[2 * i + 1:2 * i + 2]], axis=1)
             for i in range(N_PAIRS)]

    def chunk_rows(c):
        return pl.ds(pl.multiple_of(c * K_CHUNK, K_CHUNK), K_CHUNK)

    def for_chunk_pairs(n, body):
        def pair(cc, carry):
            body([2 * cc, 2 * cc + 1])
            return carry
        lax.fori_loop(0, n // 2, pair, 0)

        @pl.when(n % 2 == 1)
        def _():
            body([n - 1])

    def score_chunks(chunks):
        for c in chunks:
            ki2 = ki_ref[0, chunk_rows(c), :]
            acc = jnp.zeros(shape, f32)
            for i in range(N_PAIRS):
                r = jnp.maximum(jnp.dot(ki2, qibd_ref[i], preferred_element_type=f32), 0.0) * wpair[i]
                acc = acc + r[:, :Q_TILE] + r[:, Q_TILE:]
            sc_ref[c] = jnp.where(c * K_CHUNK + key_pos <= q_pos, acc, -jnp.inf)

    for_chunk_pairs(nck, score_chunks)

    def count(pred):
        def body(c, cnt):
            hit = pred(c)
            for g in range(K_CHUNK // COUNT_ROWS):
                rows = slice(g * COUNT_ROWS, (g + 1) * COUNT_ROWS)
                cnt = jnp.where(hit[rows], cnt + 1.0, cnt)
            return cnt
        cnt = lax.fori_loop(0, nck, body, jnp.zeros((COUNT_ROWS, Q_TILE), f32))
        return jnp.sum(cnt, axis=0, keepdims=True)

    def write_mask(pred):
        def body(c, carry):
            sc_ref[c] = jnp.where(pred(c), 0.0, NEG)
            return carry
        lax.fori_loop(0, nck, body, 0)

    need_search = (j + 1) * Q_TILE > ksel

    @pl.when(jnp.logical_not(need_search))
    def _():
        write_mask(lambda c: c * K_CHUNK + key_pos < (j + 1) * Q_TILE)

    @pl.when(need_search)
    def _():
        def key_body(c, carry):
            key_ref[c] = _to_key(sc_ref[c])
            return carry
        lax.fori_loop(0, nck, key_body, 0)

        def bit_body(i, carry):
            lo, n_ge = carry
            cand = lo + jnp.left_shift(jnp.int32(1), 31 - i)
            candb = jnp.broadcast_to(cand, shape)
            tot = count(lambda c: key_ref[c] >= candb)
            ok = tot >= ksel
            return jnp.where(ok, cand, lo), jnp.where(ok, tot, n_ge)

        carry = (jnp.full((1, Q_TILE), INT_MIN, jnp.int32),
                 jnp.full((1, Q_TILE), nck * K_CHUNK, jnp.int32).astype(f32))
        for first_bit in range(0, 32, SEARCH_BITS):
            carry = lax.cond(
                jnp.max(carry[1]) > ksel,
                lambda c, first_bit=first_bit: lax.fori_loop(first_bit, first_bit + SEARCH_BITS, bit_body, c),
                lambda c: c, carry)
        thr, n_ge = carry
        thrb = jnp.broadcast_to(thr, shape)
        has_tie = jnp.max(n_ge) > ksel

        @pl.when(jnp.logical_not(has_tie))
        def _():
            write_mask(lambda c: key_ref[c] >= thrb)

        @pl.when(has_tie)
        def _():
            need = ksel - count(lambda c: key_ref[c] > thrb)

            def cut_body(i, x):
                cand = x + jnp.left_shift(jnp.int32(1), idx_bits - 1 - i)
                candb = jnp.broadcast_to(cand, shape)
                g = count(lambda c: jnp.where(key_ref[c] == thrb, c * K_CHUNK + key_pos, 2 ** 30) < candb)
                return jnp.where(g < need, cand, x)

            cut = lax.fori_loop(0, idx_bits, cut_body, jnp.zeros((1, Q_TILE), jnp.int32))
            cutb = jnp.broadcast_to(cut, shape)

            def sel(c):
                kc = key_ref[c]
                eq_idx = jnp.where(kc == thrb, c * K_CHUNK + key_pos, 2 ** 30)
                return jnp.where(kc > thrb, 0, eq_idx) <= cutb
            write_mask(sel)

    def logits_chunk(c, which, ms):
        mask = sc_ref[c]
        mask2 = jnp.concatenate([mask, mask], axis=1)
        out = []
        for i in range(N_PAIRS):
            kc = k_ref[0, chunk_rows(c), i * 2 * HEAD_DIM:(i + 1) * 2 * HEAD_DIM]
            s = jnp.dot(kc, qbd_ref[i], preferred_element_type=f32) + mask2
            if which is not None:
                s = s + bt_ref[par, which, i]
            lg_ref[i, c] = s
            out.append(jnp.maximum(ms[i], _fold(s, jnp.max, MAX_ROWS)))
        return tuple(out)

    ms = tuple(jnp.full((MAX_ROWS, 2 * Q_TILE), NEG, f32) for _ in range(N_PAIRS))
    nfar = jnp.maximum(nck - 2, 0)
    ms = lax.fori_loop(
        0, nfar // 2, lambda cc, ms: logits_chunk(2 * cc + 1, None, logits_chunk(2 * cc, None, ms)), ms)
    ms = lax.cond(nfar % 2 == 1, lambda ms: logits_chunk(nfar - 1, None, ms), lambda ms: ms, ms)
    ms = lax.cond(nck >= 2,
                  lambda ms: logits_chunk(nck - 1, 1, logits_chunk(nck - 2, 0, ms)),
                  lambda ms: logits_chunk(nck - 1, 1, ms), ms)
    mrow = [jnp.max(m, axis=0, keepdims=True) for m in ms]

    o_ref[...] = jnp.zeros(o_ref.shape, f32)

    def pv_chunks(chunks):
        for i in range(N_PAIRS):
            acc = None
            for c in chunks:
                p = jnp.exp2((lg_ref[i, c] - mrow[i]).astype(bf16))
                d = jnp.dot(vt_ref[0, c, i], p, preferred_element_type=f32)
                acc = d if acc is None else acc + d
            o_ref[i] += acc
    for_chunk_pairs(nck, pv_chunks)

    pieces = []
    for i in range(N_PAIRS):
        o = o_ref[i]
        l = o[2 * HEAD_DIM:2 * HEAD_DIM + 1, :]
        pieces.append(o[0:HEAD_DIM, 0:Q_TILE] / l[:, 0:Q_TILE])
        pieces.append(o[HEAD_DIM:2 * HEAD_DIM, Q_TILE:] / l[:, Q_TILE:])
    out_ref[0] = (jnp.concatenate(pieces, axis=0).T * ga_ref[0]).astype(bf16)


def _prompt_attention(qt, qit, wit, kb, ki2, vt, ga, btab):
    B, S, _ = kb.shape
    assert S % K_CHUNK == 0
    nq = S // Q_TILE
    nc = S // K_CHUNK
    ksel = min(TOPK_MAX, S // 4)
    qcol = lambda b, j: (b, 0, j)
    perb3 = lambda b, j: (b, 0, 0)
    return pl.pallas_call(
        functools.partial(_attn_kernel, ksel=ksel, idx_bits=max(1, (S - 1).bit_length())),
        grid=(B, nq),
        in_specs=[
            pl.BlockSpec((1, D_ATTN, Q_TILE), qcol),
            pl.BlockSpec((1, N_IDX_HEADS * IDX_DIM, Q_TILE), qcol),
            pl.BlockSpec((1, N_IDX_HEADS, Q_TILE), qcol),
            pl.BlockSpec((1, S, D_ATTN), perb3),
            pl.BlockSpec((1, S, LANES), perb3),
            pl.BlockSpec((1, nc, N_PAIRS, PV_ROWS, K_CHUNK), lambda b, j: (b, 0, 0, 0, 0)),
            pl.BlockSpec((1, Q_TILE, D_ATTN), lambda b, j: (b, j, 0)),
            pl.BlockSpec((N_PAR, 2, N_PAIRS, K_CHUNK, 2 * Q_TILE), lambda b, j: (0, 0, 0, 0, 0)),
        ],
        out_specs=pl.BlockSpec((1, Q_TILE, D_ATTN), lambda b, j: (b, j, 0)),
        out_shape=jax.ShapeDtypeStruct((B, S, D_ATTN), bf16),
        scratch_shapes=[
            pltpu.VMEM((nc, K_CHUNK, Q_TILE), f32),
            pltpu.VMEM((nc, K_CHUNK, Q_TILE), jnp.int32),
            pltpu.VMEM((N_PAIRS, nc, K_CHUNK, 2 * Q_TILE), f32),
            pltpu.VMEM((N_PAIRS, 2 * HEAD_DIM, 2 * Q_TILE), bf16),
            pltpu.VMEM((N_PAIRS, 2 * IDX_DIM, 2 * Q_TILE), bf16),
            pltpu.VMEM((N_PAIRS, PV_ROWS, 2 * Q_TILE), f32),
        ],
        compiler_params=pltpu.CompilerParams(
            dimension_semantics=("parallel", "arbitrary"), vmem_limit_bytes=VMEM_LIMIT),
        name="prompt_attention",
    )(qt, qit, wit, kb, ki2, vt, ga, btab)


CONV_HALO = 32
CONV_ROWS = 32


def _layer_norm_swish(c, lng_ref, lnb_ref):
    mu = jnp.mean(c, axis=-1, keepdims=True)
    d = c - mu
    var = jnp.mean(d * d, axis=-1, keepdims=True)
    return _silu(d * lax.rsqrt(var + EPS) * lng_ref[...] + lnb_ref[...])


def _merge_kernel(mixa_ref, ucur_ref, uprev_ref, gc_ref, x_ref, wo_ref, dw_ref, dwb_ref, lng_ref,
                  lnb_ref, y_ref, ext_ref, sh_ref, cg_ref, *, tq):
    i = pl.program_id(1)
    n = CONV_HALO + tq
    ext_ref[0:CONV_HALO] = jnp.where(i > 0, uprev_ref[0], 0.0)
    ext_ref[CONV_HALO:n] = ucur_ref[0]
    ext_ref[n:n + SUB] = jnp.zeros((SUB, D_CONV), f32)
    ext = ext_ref[...]
    for r in range(1, SUB):
        sh_ref[r - 1] = pltpu.roll(ext, n + SUB - r, 0)[0:n, :]
    first = CONV_HALO - (CONV_W - 1)
    for c in range(tq // CONV_ROWS):
        acc = jnp.broadcast_to(dwb_ref[...], (CONV_ROWS, D_CONV))
        for w in range(CONV_W):
            a, r = divmod(first + w, SUB)
            start = c * CONV_ROWS + a * SUB
            src = ext_ref if r == 0 else sh_ref.at[r - 1]
            acc = acc + src[start:start + CONV_ROWS, :] * jnp.tile(dw_ref[w], (CONV_ROWS // SUB, 1))
        rows = slice(c * CONV_ROWS, (c + 1) * CONV_ROWS)
        cg_ref[rows] = (_layer_norm_swish(acc, lng_ref, lnb_ref) * gc_ref[0, rows]).astype(bf16)
    y_ref[0] = (x_ref[0]
                + jnp.dot(mixa_ref[0], wo_ref[0:D_ATTN], preferred_element_type=f32)
                + jnp.dot(cg_ref[...], wo_ref[D_ATTN:D_ATTN + D_CONV], preferred_element_type=f32))


def _prompt_merge(mixa, u, gc, x, wo, dw, dwb, lng, lnb):
    B, S, D = x.shape
    tq = 256
    assert S % tq == 0 and tq % CONV_HALO == 0
    row = lambda b, i: (b, i, 0)
    const2 = lambda b, i: (0, 0)
    halo = lambda b, i: (b, jnp.maximum(i * (tq // CONV_HALO) - 1, 0), 0)
    return pl.pallas_call(
        functools.partial(_merge_kernel, tq=tq),
        grid=(B, S // tq),
        in_specs=[
            pl.BlockSpec((1, tq, D_ATTN), row),
            pl.BlockSpec((1, tq, D_CONV), row),
            pl.BlockSpec((1, CONV_HALO, D_CONV), halo),
            pl.BlockSpec((1, tq, D_CONV), row),
            pl.BlockSpec((1, tq, D), row),
            pl.BlockSpec((D_ATTN + D_CONV, D), const2),
            pl.BlockSpec((CONV_W, SUB, D_CONV), lambda b, i: (0, 0, 0)),
            pl.BlockSpec((1, D_CONV), const2),
            pl.BlockSpec((1, D_CONV), const2),
            pl.BlockSpec((1, D_CONV), const2),
        ],
        out_specs=pl.BlockSpec((1, tq, D), row),
        out_shape=jax.ShapeDtypeStruct((B, S, D), f32),
        scratch_shapes=[
            pltpu.VMEM((CONV_HALO + tq + SUB, D_CONV), f32),
            pltpu.VMEM((SUB - 1, CONV_HALO + tq, D_CONV), f32),
            pltpu.VMEM((tq, D_CONV), bf16),
        ],
        compiler_params=pltpu.CompilerParams(
            dimension_semantics=("parallel", "arbitrary"), vmem_limit_bytes=VMEM_LIMIT),
        name="prompt_merge",
    )(mixa, u, u, gc, x, wo, dw, dwb, lng, lnb)


SCORE_UNROLL = 16


def _sample_score_kernel(pt_ref, qi_ref, w_ref, cki_ref, out_ref, buf_ref, sem_ref, *, npg):
    b = pl.program_id(0)
    nb = pl.num_programs(0)
    slot = b % 2

    def page_copy(bb, p, sl):
        return pltpu.make_async_copy(cki_ref.at[pt_ref[bb, p]], buf_ref.at[sl, p], sem_ref.at[sl])

    def start_all(bb, sl):
        def body(pp, carry):
            page_copy(bb, 2 * pp, sl).start(priority=0)
            page_copy(bb, 2 * pp + 1, sl).start(priority=1)
            return carry
        lax.fori_loop(0, npg // 2, body, 0)

    @pl.when(b == 0)
    def _():
        start_all(0, 0)

    @pl.when(b + 1 < nb)
    def _():
        start_all(b + 1, 1 - slot)

    def wait_body(p, carry):
        page_copy(b, p, slot).wait()
        return carry
    lax.fori_loop(0, npg, wait_body, 0)

    qs = (qi_ref[0] * (IDX_DIM ** -0.5)).astype(bf16)
    wcol = w_ref[0] * (N_IDX_HEADS ** -0.5)
    wcol2 = jnp.concatenate([wcol, wcol], axis=1)

    def body(i, carry):
        for u in range(0, SCORE_UNROLL, 2):
            p = i * SCORE_UNROLL + u
            pages = jnp.concatenate([buf_ref[slot, p], buf_ref[slot, p + 1]], axis=1).astype(bf16)
            s = jnp.dot(qs, pages, preferred_element_type=f32)
            r = jnp.sum(jnp.maximum(s, 0.0) * wcol2, axis=0, keepdims=True)
            out_ref[0, pl.ds(p, 1), :] = r[:, :LANES]
            out_ref[0, pl.ds(p + 1, 1), :] = r[:, LANES:]
        return carry
    lax.fori_loop(0, npg // SCORE_UNROLL, body, 0)


def _sample_scores(page_table, qi3, wrow, cki_t):
    Bd, npg = page_table.shape
    page = cki_t.shape[2]
    assert page == LANES and npg % SCORE_UNROLL == 0
    grid_spec = pltpu.PrefetchScalarGridSpec(
        num_scalar_prefetch=1,
        grid=(Bd,),
        in_specs=[
            pl.BlockSpec((1, N_IDX_HEADS, IDX_DIM), lambda b, pt: (b, 0, 0)),
            pl.BlockSpec((1, N_IDX_HEADS, LANES), lambda b, pt: (b, 0, 0)),
            pl.BlockSpec(memory_space=pl.ANY),
        ],
        out_specs=pl.BlockSpec((1, npg, page), lambda b, pt: (b, 0, 0)),
        scratch_shapes=[
            pltpu.VMEM((2, npg, IDX_DIM, page), f32),
            pltpu.SemaphoreType.DMA((2,)),
        ],
    )
    return pl.pallas_call(
        functools.partial(_sample_score_kernel, npg=npg),
        grid_spec=grid_spec,
        out_shape=jax.ShapeDtypeStruct((Bd, npg, page), f32),
        compiler_params=pltpu.CompilerParams(
            dimension_semantics=("arbitrary",), vmem_limit_bytes=VMEM_LIMIT),
        name="sample_scores",
    )(page_table, qi3, wrow, cki_t)


def _sample_select_kernel(sc_ref, qi_ref, kit_ref, kw_ref, mask_ref, mnew_ref, key_ref, *,
                          npg, ksel, past, idx_bits):
    Bd = sc_ref.shape[0]
    shape = (Bd, npg, LANES)

    qb = (qi_ref[...] * (IDX_DIM ** -0.5)).astype(bf16).astype(f32)
    kb = kit_ref[...].astype(bf16).astype(f32)
    prod = qb * kb
    grp = lax.broadcasted_iota(jnp.int32, prod.shape, 1) // IDX_DIM
    kwv = kw_ref[...]
    s_new = jnp.zeros((Bd, 1), f32)
    for h in range(N_IDX_HEADS):
        sh = jnp.sum(jnp.where(grp == h, prod, 0.0), axis=1, keepdims=True)
        s_new = s_new + jnp.maximum(sh, 0.0) * (kwv[:, IDX_DIM + h:IDX_DIM + h + 1] * (N_IDX_HEADS ** -0.5))
    key_new = _to_key(s_new).reshape(Bd, 1, 1)

    key_ref[...] = _to_key(sc_ref[...])
    pos = (lax.broadcasted_iota(jnp.int32, shape, 1) * LANES
           + lax.broadcasted_iota(jnp.int32, shape, 2))

    def count(pred, pred_new):
        x = jnp.where(pred, 1.0, 0.0)
        s = jnp.sum(jnp.sum(x, axis=1, keepdims=True), axis=2, keepdims=True)
        return s + jnp.where(pred_new, 1.0, 0.0)

    def bit_body(i, lo):
        cand = lo + jnp.left_shift(jnp.int32(1), 31 - i)
        tot = count(key_ref[...] >= cand, key_new >= cand)
        return jnp.where(tot >= ksel, cand, lo)

    thr = lax.fori_loop(0, 32, bit_body, jnp.full((Bd, 1, 1), INT_MIN, jnp.int32))
    keys = key_ref[...]
    need = ksel - count(keys > thr, key_new > thr)
    big = jnp.int32(2 ** 30)
    eq_pos = jnp.where(keys == thr, pos, big)
    eq_pos_new = jnp.where(key_new == thr, jnp.int32(past), big)

    def cut_body(i, x):
        cand = x + jnp.left_shift(jnp.int32(1), idx_bits - 1 - i)
        g = count(eq_pos < cand, eq_pos_new < cand)
        return jnp.where(g < need, cand, x)

    cut = lax.fori_loop(0, idx_bits, cut_body, jnp.zeros((Bd, 1, 1), jnp.int32))
    mask_ref[...] = jnp.where(jnp.where(keys > thr, 0, eq_pos) <= cut, 0.0, NEG)
    sel_new = jnp.where(key_new > thr, 0, eq_pos_new) <= cut
    mnew_ref[...] = jnp.broadcast_to(jnp.where(sel_new, 0.0, NEG), mnew_ref.shape)


def _sample_select(scores, qi, ki_tiled, kw, *, past):
    Bd, npg, _ = scores.shape
    ksel = min(TOPK_MAX, (past + 1) // 4)
    vm = pl.BlockSpec(memory_space=pltpu.VMEM)
    return pl.pallas_call(
        functools.partial(_sample_select_kernel, npg=npg, ksel=ksel, past=past,
                          idx_bits=past.bit_length()),
        in_specs=[vm, vm, vm, vm],
        out_specs=(vm, vm),
        out_shape=(jax.ShapeDtypeStruct((Bd, npg, LANES), f32),
                   jax.ShapeDtypeStruct((Bd, N_HEADS, LANES), f32)),
        scratch_shapes=[pltpu.VMEM((Bd, npg, LANES), jnp.int32)],
        compiler_params=pltpu.CompilerParams(vmem_limit_bytes=VMEM_LIMIT),
        name="sample_select",
    )(scores, qi, ki_tiled, kw)


PAGE_BUFS = 32
PAGE_GROUP = 4


def _sample_attn_kernel(pt_ref, qbd_ref, knew_ref, vnew_ref, mask_ref, mnew_ref, rbt_ref, ck_ref, cv_ref,
                        out_ref, buf_ref, lg_ref, sem_ref, *, npg, past):
    b = pl.program_id(0)
    nb = pl.num_programs(0)
    per_b = 2 * npg
    total = nb * per_b

    def page_copy(g, src_ref):
        phys = pt_ref[g // per_b, g % npg]
        slot = g % PAGE_BUFS
        return pltpu.make_async_copy(src_ref.at[phys], buf_ref.at[slot], sem_ref.at[slot])

    def start(g):
        is_k = g % per_b < npg

        @pl.when(is_k)
        def _():
            page_copy(g, ck_ref).start()

        @pl.when(jnp.logical_not(is_k))
        def _():
            page_copy(g, cv_ref).start()

    def wait(g):
        page_copy(g, ck_ref).wait()

    @pl.when(b == 0)
    def _():
        for g in range(PAGE_BUFS):
            start(g)

    g0 = b * per_b
    qbd = (qbd_ref[0] * (HEAD_DIM ** -0.5 * LOG2E)).astype(bf16)
    blk = (lax.broadcasted_iota(jnp.int32, (N_HEADS, D_ATTN), 0)
           == lax.broadcasted_iota(jnp.int32, (N_HEADS, D_ATTN), 1) // HEAD_DIM)
    tok = lax.broadcasted_iota(jnp.int32, (N_HEADS, LANES), 1)

    def bias_of(dist):
        far = rbt_ref[N_BUCKETS - 1]
        acc = jnp.broadcast_to(rbt_ref[0] - far, dist.shape)
        for i, t in enumerate(BUCKET_THRESH):
            acc = jnp.where(dist >= t, rbt_ref[i + 1] - far, acc)
        return acc * LOG2E

    def refill(gs):
        for g in gs:
            @pl.when(g + PAGE_BUFS < total)
            def _():
                start(g + PAGE_BUFS)

    def k_body(it, m):
        gs = [g0 + it * PAGE_GROUP + u for u in range(PAGE_GROUP)]
        for g in gs:
            wait(g)
        for u, g in enumerate(gs):
            p = it * PAGE_GROUP + u
            kp = buf_ref[g % PAGE_BUFS].astype(bf16)
            s = jnp.dot(qbd, kp, preferred_element_type=f32) + mask_ref[0, pl.ds(p, 1), :]
            lg_ref[p] = s
            m = jnp.maximum(m, s)
        refill(gs)
        return m

    m = lax.fori_loop(0, npg // PAGE_GROUP, k_body, jnp.full((N_HEADS, LANES), NEG, f32))
    last = lg_ref[npg - 1] + bias_of(past - ((npg - 1) * LANES + tok))
    lg_ref[npg - 1] = last
    m = jnp.maximum(m, last)

    s_new = (jnp.sum(qbd_ref[0] * knew_ref[0], axis=1, keepdims=True) * (HEAD_DIM ** -0.5 * LOG2E)
             + bias_of(jnp.zeros((N_HEADS, LANES), jnp.int32))[:, 0:1] + mnew_ref[0][:, 0:1])
    m_row = jnp.maximum(jnp.max(m, axis=1, keepdims=True), s_new)
    mb = jnp.broadcast_to(m_row, (N_HEADS, LANES))

    def v_body(it, carry):
        acc, l = carry
        gs = [g0 + npg + it * PAGE_GROUP + u for u in range(PAGE_GROUP)]
        for g in gs:
            wait(g)
        for u, g in enumerate(gs):
            vp = buf_ref[g % PAGE_BUFS].astype(bf16)
            pr = jnp.exp2(lg_ref[it * PAGE_GROUP + u] - mb)
            acc = acc + lax.dot_general(pr.astype(bf16), vp, (((1,), (1,)), ((), ())),
                                        preferred_element_type=f32)
            l = l + pr
        refill(gs)
        return acc, l

    acc, l = lax.fori_loop(0, npg // PAGE_GROUP, v_body, (jnp.zeros((N_HEADS, D_ATTN), f32),
                                                          jnp.zeros((N_HEADS, LANES), f32)))
    p_new = jnp.exp2(s_new - m_row)
    l_tot = jnp.sum(l, axis=1, keepdims=True) + p_new
    num = jnp.where(blk, acc + p_new * vnew_ref[0], 0.0) / l_tot
    out_ref[0] = jnp.sum(num, axis=0, keepdims=True)


def _sample_attention(page_table, qbd, k_new_flat, v_new_flat, mask, mnew, rbt, ck_t, cv_t, *, past):
    Bd, npg = page_table.shape
    rows, page = ck_t.shape[1], ck_t.shape[2]
    assert rows == D_ATTN and page == LANES
    assert npg % PAGE_GROUP == 0 and PAGE_BUFS % PAGE_GROUP == 0 and 2 * npg >= PAGE_BUFS
    grid_spec = pltpu.PrefetchScalarGridSpec(
        num_scalar_prefetch=1,
        grid=(Bd,),
        in_specs=[
            pl.BlockSpec((1, N_HEADS, D_ATTN), lambda b, pt: (b, 0, 0)),
            pl.BlockSpec((1, 1, D_ATTN), lambda b, pt: (b, 0, 0)),
            pl.BlockSpec((1, 1, D_ATTN), lambda b, pt: (b, 0, 0)),
            pl.BlockSpec((1, npg, LANES), lambda b, pt: (b, 0, 0)),
            pl.BlockSpec((1, N_HEADS, LANES), lambda b, pt: (b, 0, 0)),
            pl.BlockSpec((N_BUCKETS, N_HEADS, LANES), lambda b, pt: (0, 0, 0)),
            pl.BlockSpec(memory_space=pl.ANY),
            pl.BlockSpec(memory_space=pl.ANY),
        ],
        out_specs=pl.BlockSpec((1, 1, D_ATTN), lambda b, pt: (b, 0, 0)),
        scratch_shapes=[
            pltpu.VMEM((PAGE_BUFS, rows, page), f32),
            pltpu.VMEM((npg, N_HEADS, LANES), f32),
            pltpu.SemaphoreType.DMA((PAGE_BUFS,)),
        ],
    )
    return pl.pallas_call(
        functools.partial(_sample_attn_kernel, npg=npg, past=past),
        grid_spec=grid_spec,
        out_shape=jax.ShapeDtypeStruct((Bd, 1, D_ATTN), f32),
        compiler_params=pltpu.CompilerParams(
            dimension_semantics=("arbitrary",), vmem_limit_bytes=VMEM_LIMIT),
        name="sample_attention",
    )(page_table, qbd, k_new_flat, v_new_flat, mask, mnew, rbt, ck_t, cv_t)


def _sample_merge_kernel(attn_ref, ga_ref, u_ref, gc_ref, st_ref, x_ref, wo_ref, dw_ref, dwb_ref,
                         lng_ref, lnb_ref, y_ref):
    acc = dwb_ref[...] + u_ref[...] * dw_ref[CONV_W - 1:CONV_W, :]
    for w in range(CONV_W - 1):
        acc = acc + st_ref[w] * dw_ref[w:w + 1, :]
    cg = (_layer_norm_swish(acc, lng_ref, lnb_ref) * gc_ref[...]).astype(bf16)
    mixa = (attn_ref[...] * ga_ref[...]).astype(bf16)
    y_ref[...] = (x_ref[...]
                  + jnp.dot(mixa, wo_ref[0:D_ATTN], preferred_element_type=f32)
                  + jnp.dot(cg, wo_ref[D_ATTN:D_ATTN + D_CONV], preferred_element_type=f32))


def _sample_merge(attn, ga, u, gc, state_t, x, wo, dw, dwb, lng, lnb):
    vm = pl.BlockSpec(memory_space=pltpu.VMEM)
    return pl.pallas_call(
        _sample_merge_kernel,
        in_specs=[vm] * 11,
        out_specs=vm,
        out_shape=jax.ShapeDtypeStruct(x.shape, f32),
        compiler_params=pltpu.CompilerParams(vmem_limit_bytes=VMEM_LIMIT),
        name="sample_merge",
    )(attn, ga, u, gc, state_t, x, wo, dw, dwb, lng, lnb)


def kernel(x_prompt, x_sample, cache_k, cache_v, cache_k_idx, state_conv, page_table, rel_bias, norm_g,
           w_in, q_norm_g, k_norm_g, dw_w, dw_b, ln_g, ln_b, w_out):
    depth = norm_g.shape[0]
    assert depth == 1, "single-layer step"
    B, S, D = x_prompt.shape
    Bd, T, _ = x_sample.shape
    assert T == 1
    n_phys, page = cache_k.shape[1], cache_k.shape[2]
    past = page_table.shape[1] * page

    w = w_in[0]
    c_ki = 2 * D_ATTN + 2 * N_HEADS * HEAD_DIM + N_IDX_HEADS * IDX_DIM
    c_conv = c_ki + IDX_DIM + N_IDX_HEADS
    pad = jnp.zeros((D, LANES - IDX_DIM - N_IDX_HEADS), w.dtype)
    w_all = jnp.concatenate([w[:, :c_ki], w[:, c_ki:c_conv], pad, w[:, c_conv:]], axis=1).astype(bf16)
    assert w_all.shape[1] == _C_END
    wo = w_out[0].astype(bf16)
    g = norm_g[0][None]
    qg = jnp.tile(q_norm_g[0], N_HEADS)[None]
    kg = jnp.tile(k_norm_g[0], N_HEADS)[None]
    hid = np.arange(D_ATTN) // HEAD_DIM
    bd = jnp.asarray((hid[:, None] == hid[None, :]) / HEAD_DIM, dtype=bf16)
    dwb, lng, lnb = dw_b[0][None], ln_g[0][None], ln_b[0][None]

    (qt, k_p, kb, v_p, vt, ga_p, qit, kw_p, wit, ki2, u_p, gc_p) = _inproj(
        x_prompt, g, w_all, qg, kg, bd, prompt=True)
    btab = _bias_tables(rel_bias)
    mixa = _prompt_attention(qt, qit, wit, kb, ki2, vt, ga_p, btab)
    dw8 = jnp.broadcast_to(dw_w[0][:, None, :], (CONV_W, SUB, D_CONV))
    y_prompt = _prompt_merge(mixa, u_p, gc_p, x_prompt, wo, dw8, dwb, lng, lnb)

    xs = x_sample.reshape(1, Bd, D)
    q_s, k_s, v_s, ga_s, qi_s, kw_s, u_s, gc_s = (a[0] for a in _inproj(
        xs, g, w_all, qg, kg, bd, prompt=False))
    ki_s = kw_s[:, :IDX_DIM]
    wrow = jnp.broadcast_to(kw_s[:, IDX_DIM:IDX_DIM + N_IDX_HEADS, None], (Bd, N_IDX_HEADS, LANES))
    cki_t = jnp.transpose(cache_k_idx[0], (0, 2, 1))
    ck_t = jnp.transpose(cache_k[0], (0, 2, 3, 1)).reshape(n_phys, D_ATTN, page)
    cv_t = jnp.transpose(cache_v[0], (0, 2, 3, 1)).reshape(n_phys, D_ATTN, page)
    scores = _sample_scores(page_table, qi_s.reshape(Bd, N_IDX_HEADS, IDX_DIM), wrow, cki_t)
    mask, mnew = _sample_select(scores, qi_s, jnp.tile(ki_s, (1, N_IDX_HEADS)), kw_s, past=past)
    rbt = jnp.broadcast_to(rel_bias[:, :, None], (N_BUCKETS, N_HEADS, LANES))
    eye = jnp.asarray(np.arange(N_HEADS)[:, None] == hid[None, :], dtype=f32)
    qbd = q_s[:, None, :] * eye[None]
    attn_s = _sample_attention(page_table, qbd, k_s[:, None, :], v_s[:, None, :],
                               mask, mnew, rbt, ck_t, cv_t, past=past)
    state_t = jnp.transpose(state_conv[0], (1, 0, 2))
    y_sample = _sample_merge(attn_s[:, 0], ga_s, u_s, gc_s, state_t, x_sample[:, 0], wo,
                             dw_w[0], dwb, lng, lnb)

    heads = (N_HEADS, HEAD_DIM)
    return (
        y_prompt,
        y_sample[:, None],
        k_p.reshape((1, B, S) + heads),
        v_p.reshape((1, B, S) + heads),
        kw_p[None, :, :, :IDX_DIM],
        u_p[None, :, S - (CONV_W - 1):],
        k_s.reshape((1, Bd, 1) + heads),
        v_s.reshape((1, Bd, 1) + heads),
        ki_s.reshape(1, Bd, 1, IDX_DIM),
        jnp.concatenate([state_conv[0][:, 1:], u_s[:, None]], axis=1)[None],
    )
```

```python
import functools
import math

import numpy as np
import jax
import jax.numpy as jnp
from jax import lax
from jax.experimental import pallas as pl
from jax.experimental.pallas import tpu as pltpu

N_HEADS = 8
HEAD_DIM = 64
N_IDX_HEADS = 8
IDX_DIM = 64
D_ATTN = N_HEADS * HEAD_DIM
D_CONV = 512
CONV_W = 31
TOPK_MAX = 256
N_BUCKETS = 32
MAX_DISTANCE = 128
EPS = 1e-6
NEG = -1e30
INT_MIN = -2 ** 31
LOG2E = math.log2(math.e)

LANES = 128
Q_TILE = 256
K_CHUNK = 256
PV_ROWS = 144
VMEM_LIMIT = 56 * 1024 * 1024

f32 = jnp.float32
bf16 = jnp.bfloat16


def _bucket_thresholds():
    n = np.arange(0, 4 * MAX_DISTANCE)
    max_exact = N_BUCKETS // 2
    nf = np.maximum(n, 1).astype(np.float32)
    large = max_exact + (np.log(nf / np.float32(max_exact)) / np.float32(math.log(MAX_DISTANCE / max_exact))
                         * np.float32(N_BUCKETS - max_exact)).astype(np.int32)
    bucket = np.where(n < max_exact, n, np.minimum(large, N_BUCKETS - 1))
    return [int(n[bucket >= i].min()) for i in range(1, N_BUCKETS)]


BUCKET_THRESH = _bucket_thresholds()
assert BUCKET_THRESH[-1] <= K_CHUNK + 1 and K_CHUNK % Q_TILE == 0
N_PAR = K_CHUNK // Q_TILE


def _silu(x):
    return x * jax.nn.sigmoid(x)


def _to_key(x):
    b = pltpu.bitcast(x, jnp.int32)
    return jnp.where(b < 0, b ^ jnp.int32(0x7FFFFFFF), b)


_C_Q, _C_K, _C_V, _C_GA, _C_QI, _C_KW, _C_A, _C_B, _C_GC, _C_END = (
    0, 512, 1024, 1536, 2048, 2560, 2688, 3200, 3712, 4224)


def _inproj_kernel(x_ref, g_ref, w_ref, qg_ref, kg_ref, bd_ref, *outs, prompt, tm):
    xf = x_ref[0]
    ms = jnp.mean(xf * xf, axis=-1, keepdims=True)
    xn = (xf * lax.rsqrt(ms + EPS) * g_ref[...]).astype(bf16)

    def proj(a, b):
        return jnp.dot(xn, w_ref[:, a:b], preferred_element_type=f32)

    def head_norm(z, gain_ref):
        msq = jnp.dot((z * z).astype(bf16), bd_ref[...], preferred_element_type=f32)
        return z * lax.rsqrt(msq + EPS) * gain_ref[...]

    q = head_norm(proj(_C_Q, _C_K), qg_ref)
    k = head_norm(proj(_C_K, _C_V), kg_ref)
    v = proj(_C_V, _C_GA)
    ga = _silu(proj(_C_GA, _C_QI))
    qi = proj(_C_QI, _C_KW)
    kw = proj(_C_KW, _C_A)
    u = proj(_C_A, _C_B) * jax.nn.sigmoid(proj(_C_B, _C_GC))
    gc = _silu(proj(_C_GC, _C_END))

    if prompt:
        q_o, k_o, kb_o, v_o, vt_o, ga_o, qi_o, kw_o, wi_o, ki2_o, u_o, gc_o = outs
        q_o[0] = (q * (HEAD_DIM ** -0.5 * LOG2E)).T.astype(bf16)
        qi_o[0] = (qi * (IDX_DIM ** -0.5)).T.astype(bf16)
        wi_o[0] = kw.T[IDX_DIM:IDX_DIM + N_IDX_HEADS, :]
        kb_o[0] = k.astype(bf16)
        lane = lax.broadcasted_iota(jnp.int32, kw.shape, 1)
        ki2_o[0] = jnp.where(lane < IDX_DIM, kw, pltpu.roll(kw, IDX_DIM, 1)).astype(bf16)
        vt = v.T.astype(bf16)
        tail = jnp.where(lax.broadcasted_iota(jnp.int32, (PV_ROWS - 2 * HEAD_DIM, K_CHUNK), 0) < 8,
                         1.0, 0.0).astype(bf16)
        for c in range(tm // K_CHUNK):
            for i in range(N_HEADS // 2):
                blk = vt[i * 2 * HEAD_DIM:(i + 1) * 2 * HEAD_DIM, c * K_CHUNK:(c + 1) * K_CHUNK]
                vt_o[0, c, i] = jnp.concatenate([blk, tail], axis=0)
    else:
        q_o, k_o, v_o, ga_o, qi_o, kw_o, u_o, gc_o = outs
        q_o[0] = q
        qi_o[0] = qi
    k_o[0] = k
    v_o[0] = v
    ga_o[0] = ga
    kw_o[0] = kw
    u_o[0] = u
    gc_o[0] = gc


def _inproj(x, norm_g, w_all, qg, kg, bd, *, prompt):
    B, S, D = x.shape
    tm = 512 if prompt else S
    assert S % tm == 0 and (not prompt or tm % K_CHUNK == 0)
    grid = (B, S // tm)
    row = lambda b, i: (b, i, 0)
    const2 = lambda b, i: (0, 0)
    col = lambda b, i: (b, 0, i)
    nck = tm // K_CHUNK
    if prompt:
        out_shape = (
            jax.ShapeDtypeStruct((B, D_ATTN, S), bf16),
            jax.ShapeDtypeStruct((B, S, D_ATTN), f32),
            jax.ShapeDtypeStruct((B, S, D_ATTN), bf16),
            jax.ShapeDtypeStruct((B, S, D_ATTN), f32),
            jax.ShapeDtypeStruct((B, S // K_CHUNK, N_HEADS // 2, PV_ROWS, K_CHUNK), bf16),
            jax.ShapeDtypeStruct((B, S, D_ATTN), f32),
            jax.ShapeDtypeStruct((B, N_IDX_HEADS * IDX_DIM, S), bf16),
            jax.ShapeDtypeStruct((B, S, LANES), f32),
            jax.ShapeDtypeStruct((B, N_IDX_HEADS, S), f32),
            jax.ShapeDtypeStruct((B, S, LANES), bf16),
            jax.ShapeDtypeStruct((B, S, D_CONV), f32),
            jax.ShapeDtypeStruct((B, S, D_CONV), f32),
        )
        out_specs = (
            pl.BlockSpec((1, D_ATTN, tm), col),
            pl.BlockSpec((1, tm, D_ATTN), row),
            pl.BlockSpec((1, tm, D_ATTN), row),
            pl.BlockSpec((1, tm, D_ATTN), row),
            pl.BlockSpec((1, nck, N_HEADS // 2, PV_ROWS, K_CHUNK), lambda b, i: (b, i, 0, 0, 0)),
            pl.BlockSpec((1, tm, D_ATTN), row),
            pl.BlockSpec((1, N_IDX_HEADS * IDX_DIM, tm), col),
            pl.BlockSpec((1, tm, LANES), row),
            pl.BlockSpec((1, N_IDX_HEADS, tm), col),
            pl.BlockSpec((1, tm, LANES), row),
            pl.BlockSpec((1, tm, D_CONV), row),
            pl.BlockSpec((1, tm, D_CONV), row),
        )
    else:
        out_shape = (
            jax.ShapeDtypeStruct((B, S, D_ATTN), f32),
            jax.ShapeDtypeStruct((B, S, D_ATTN), f32),
            jax.ShapeDtypeStruct((B, S, D_ATTN), f32),
            jax.ShapeDtypeStruct((B, S, D_ATTN), f32),
            jax.ShapeDtypeStruct((B, S, D_ATTN), f32),
            jax.ShapeDtypeStruct((B, S, LANES), f32),
            jax.ShapeDtypeStruct((B, S, D_CONV), f32),
            jax.ShapeDtypeStruct((B, S, D_CONV), f32),
        )
        out_specs = tuple(pl.BlockSpec((1, tm, s.shape[-1]), row) for s in out_shape)
    return pl.pallas_call(
        functools.partial(_inproj_kernel, prompt=prompt, tm=tm),
        grid=grid,
        in_specs=[
            pl.BlockSpec((1, tm, D), row),
            pl.BlockSpec((1, D), const2),
            pl.BlockSpec((D, _C_END), const2),
            pl.BlockSpec((1, D_ATTN), const2),
            pl.BlockSpec((1, D_ATTN), const2),
            pl.BlockSpec((D_ATTN, D_ATTN), const2),
        ],
        out_specs=out_specs,
        out_shape=out_shape,
        compiler_params=pltpu.CompilerParams(
            dimension_semantics=("parallel", "parallel"), vmem_limit_bytes=VMEM_LIMIT),
        name="inproj_prompt" if prompt else "inproj_sample",
    )(x, norm_g, w_all, qg, kg, bd)


def _bias_table_kernel(rb_ref, out_ref):
    shape = (K_CHUNK, Q_TILE)
    krow = lax.broadcasted_iota(jnp.int32, shape, 0)
    qcol = lax.broadcasted_iota(jnp.int32, shape, 1)
    for par in range(N_PAR):
        for which in range(2):
            dist = Q_TILE * par + K_CHUNK * (1 - which) + qcol - krow
            for h in range(N_HEADS):
                far = rb_ref[N_BUCKETS - 1, h]
                acc = jnp.full(shape, (rb_ref[0, h] - far) * LOG2E, f32)
                for i, t in enumerate(BUCKET_THRESH):
                    acc = jnp.where(dist >= t, (rb_ref[i + 1, h] - far) * LOG2E, acc)
                out_ref[par, which, h // 2, :, (h % 2) * Q_TILE:(h % 2 + 1) * Q_TILE] = (
                    jnp.where(dist >= 0, acc, NEG))


def _bias_tables(rel_bias):
    return pl.pallas_call(
        _bias_table_kernel,
        in_specs=[pl.BlockSpec(memory_space=pltpu.SMEM)],
        out_specs=pl.BlockSpec(memory_space=pltpu.VMEM),
        out_shape=jax.ShapeDtypeStruct((N_PAR, 2, N_HEADS // 2, K_CHUNK, 2 * Q_TILE), f32),
        name="bias_tables",
    )(rel_bias)


N_PAIRS = N_HEADS // 2
SUB = 8


COUNT_ROWS = 32
MAX_ROWS = 16


def _bit_planes(words):
    a = list(words)
    j, m = 16, 0x0000FFFF
    while j:
        k = 0
        while k < 32:
            t = (a[k] ^ lax.shift_right_logical(a[k + j], jnp.int32(j))) & jnp.int32(m)
            a[k] = a[k] ^ t
            a[k + j] = a[k + j] ^ jnp.left_shift(t, jnp.int32(j))
            k = (k + j + 1) & ~j
        j >>= 1
        m = (m ^ (m << j)) & 0xFFFFFFFF
    return a


def _fold(x, op, rows):
    return op(x.reshape(x.shape[0] // rows, rows, x.shape[1]), axis=0)


def _attn_kernel(qt_ref, qit_ref, wit_ref, k_ref, ki_ref, vt_ref, ga_ref, bt_ref, out_ref,
                 sc_ref, key_ref, pl_ref, lg_ref, qbd_ref, qibd_ref, o_ref, *, ksel, idx_bits):
    j = pl.program_id(1)
    n_chunks = sc_ref.shape[0]
    nck = j // N_PAR + 1
    par = j % N_PAR
    shape = (K_CHUNK, Q_TILE)
    key_pos = lax.broadcasted_iota(jnp.int32, shape, 0)
    q_pos = j * Q_TILE + lax.broadcasted_iota(jnp.int32, shape, 1)

    top = lax.broadcasted_iota(jnp.int32, (2 * HEAD_DIM, Q_TILE), 0) < HEAD_DIM
    for i in range(N_PAIRS):
        rows = slice(i * 2 * HEAD_DIM, (i + 1) * 2 * HEAD_DIM)
        for src, dst in ((qt_ref, qbd_ref), (qit_ref, qibd_ref)):
            x = src[0, rows, :]
            zero = jnp.zeros_like(x)
            dst[i] = jnp.concatenate([jnp.where(top, x, zero), jnp.where(top, zero, x)], axis=1)
    wi = wit_ref[0] * (N_IDX_HEADS ** -0.5)
    wpair = [jnp.concatenate([wi[2 * i:2 * i + 1], wi[2 * i + 1:2 * i + 2]], axis=1)
             for i in range(N_PAIRS)]

    def chunk_rows(c):
        return pl.ds(pl.multiple_of(c * K_CHUNK, K_CHUNK), K_CHUNK)

    def for_chunk_pairs(n, body):
        def pair(cc, carry):
            body([2 * cc, 2 * cc + 1])
            return carry
        lax.fori_loop(0, n // 2, pair, 0)

        @pl.when(n % 2 == 1)
        def _():
            body([n - 1])

    def score_chunks(chunks):
        for c in chunks:
            ki2 = ki_ref[0, chunk_rows(c), :]
            acc = jnp.zeros(shape, f32)
            for i in range(N_PAIRS):
                r = jnp.maximum(jnp.dot(ki2, qibd_ref[i], preferred_element_type=f32), 0.0) * wpair[i]
                acc = acc + r[:, :Q_TILE] + r[:, Q_TILE:]
            sc_ref[c] = jnp.where(c * K_CHUNK + key_pos <= q_pos, acc, -jnp.inf)

    for_chunk_pairs(nck, score_chunks)

    def count(pred):
        def body(c, cnt):
            hit = pred(c)
            for g in range(K_CHUNK // COUNT_ROWS):
                rows = slice(g * COUNT_ROWS, (g + 1) * COUNT_ROWS)
                cnt = jnp.where(hit[rows], cnt + 1.0, cnt)
            return cnt
        cnt = lax.fori_loop(0, nck, body, jnp.zeros((COUNT_ROWS, Q_TILE), f32))
        return jnp.sum(cnt, axis=0, keepdims=True)

    def write_mask(pred):
        def body(c, carry):
            sc_ref[c] = jnp.where(pred(c), 0.0, NEG)
            return carry
        lax.fori_loop(0, nck, body, 0)

    need_search = (j + 1) * Q_TILE > ksel

    @pl.when(jnp.logical_not(need_search))
    def _():
        write_mask(lambda c: c * K_CHUNK + key_pos < (j + 1) * Q_TILE)

    @pl.when(need_search)
    def _():
        def key_body(c, carry):
            keys = _to_key(sc_ref[c])
            key_ref[c] = keys
            ukeys = keys ^ jnp.int32(INT_MIN)
            planes = _bit_planes([ukeys[t * SUB:(t + 1) * SUB, :] for t in range(32)])
            for p in range(32):
                pl_ref[c, p] = planes[p]
            return carry
        lax.fori_loop(0, nck, key_body, 0)

        def clear_body(c, carry):
            pl_ref[c] = jnp.zeros(pl_ref.shape[1:], jnp.int32)
            return carry
        lax.fori_loop(nck, n_chunks, clear_body, 0)

        def bit_body(p, carry):
            n_above, thr_u, alive = carry
            ones = [alive[c] & pl_ref[c, p] for c in range(n_chunks)]
            acc = lax.population_count(ones[0])
            for c in range(1, n_chunks):
                acc = acc + lax.population_count(ones[c])
            n_hi = n_above + jnp.sum(acc.astype(f32), axis=0, keepdims=True)
            take = n_hi >= ksel
            takeb = jnp.broadcast_to(take, (SUB, Q_TILE))
            alive = tuple(jnp.where(takeb, ones[c], alive[c] ^ ones[c]) for c in range(n_chunks))
            bit = jnp.left_shift(jnp.int32(1), 31 - p)
            return jnp.where(take, n_above, n_hi), jnp.where(take, thr_u | bit, thr_u), alive

        everything = jnp.full((SUB, Q_TILE), -1, jnp.int32)
        n_above, thr_u, alive = lax.fori_loop(
            0, 32, bit_body,
            (jnp.zeros((1, Q_TILE), f32), jnp.zeros((1, Q_TILE), jnp.int32),
             tuple(jnp.where(c < nck, everything, 0) for c in range(n_chunks))))
        n_eq = lax.population_count(alive[0])
        for c in range(1, n_chunks):
            n_eq = n_eq + lax.population_count(alive[c])
        n_ge = n_above + jnp.sum(n_eq.astype(f32), axis=0, keepdims=True)
        thr = thr_u ^ jnp.int32(INT_MIN)
        thrb = jnp.broadcast_to(thr, shape)
        has_tie = jnp.max(n_ge) > ksel

        @pl.when(jnp.logical_not(has_tie))
        def _():
            write_mask(lambda c: key_ref[c] >= thrb)

        @pl.when(has_tie)
        def _():
            need = ksel - count(lambda c: key_ref[c] > thrb)

            def cut_body(i, x):
                cand = x + jnp.left_shift(jnp.int32(1), idx_bits - 1 - i)
                candb = jnp.broadcast_to(cand, shape)
                g = count(lambda c: jnp.where(key_ref[c] == thrb, c * K_CHUNK + key_pos, 2 ** 30) < candb)
                return jnp.where(g < need, cand, x)

            cut = lax.fori_loop(0, idx_bits, cut_body, jnp.zeros((1, Q_TILE), jnp.int32))
            cutb = jnp.broadcast_to(cut, shape)

            def sel(c):
                kc = key_ref[c]
                eq_idx = jnp.where(kc == thrb, c * K_CHUNK + key_pos, 2 ** 30)
                return jnp.where(kc > thrb, 0, eq_idx) <= cutb
            write_mask(sel)

    def logits_chunk(c, which, ms):
        mask = sc_ref[c]
        mask2 = jnp.concatenate([mask, mask], axis=1)
        out = []
        for i in range(N_PAIRS):
            kc = k_ref[0, chunk_rows(c), i * 2 * HEAD_DIM:(i + 1) * 2 * HEAD_DIM]
            s = jnp.dot(kc, qbd_ref[i], preferred_element_type=f32) + mask2
            if which is not None:
                s = s + bt_ref[par, which, i]
            lg_ref[i, c] = s
            out.append(jnp.maximum(ms[i], _fold(s, jnp.max, MAX_ROWS)))
        return tuple(out)

    ms = tuple(jnp.full((MAX_ROWS, 2 * Q_TILE), NEG, f32) for _ in range(N_PAIRS))
    nfar = jnp.maximum(nck - 2, 0)
    ms = lax.fori_loop(
        0, nfar // 2, lambda cc, ms: logits_chunk(2 * cc + 1, None, logits_chunk(2 * cc, None, ms)), ms)
    ms = lax.cond(nfar % 2 == 1, lambda ms: logits_chunk(nfar - 1, None, ms), lambda ms: ms, ms)
    ms = lax.cond(nck >= 2,
                  lambda ms: logits_chunk(nck - 1, 1, logits_chunk(nck - 2, 0, ms)),
                  lambda ms: logits_chunk(nck - 1, 1, ms), ms)
    mrow = [jnp.max(m, axis=0, keepdims=True) for m in ms]

    o_ref[...] = jnp.zeros(o_ref.shape, f32)

    def pv_chunks(chunks):
        for i in range(N_PAIRS):
            acc = None
            for c in chunks:
                p = jnp.exp2((lg_ref[i, c] - mrow[i]).astype(bf16))
                d = jnp.dot(vt_ref[0, c, i], p, preferred_element_type=f32)
                acc = d if acc is None else acc + d
            o_ref[i] += acc
    for_chunk_pairs(nck, pv_chunks)

    pieces = []
    for i in range(N_PAIRS):
        o = o_ref[i]
        l = o[2 * HEAD_DIM:2 * HEAD_DIM + 1, :]
        pieces.append(o[0:HEAD_DIM, 0:Q_TILE] / l[:, 0:Q_TILE])
        pieces.append(o[HEAD_DIM:2 * HEAD_DIM, Q_TILE:] / l[:, Q_TILE:])
    out_ref[0] = (jnp.concatenate(pieces, axis=0).T * ga_ref[0]).astype(bf16)


def _prompt_attention(qt, qit, wit, kb, ki2, vt, ga, btab):
    B, S, _ = kb.shape
    assert S % K_CHUNK == 0
    nq = S // Q_TILE
    nc = S // K_CHUNK
    ksel = min(TOPK_MAX, S // 4)
    qcol = lambda b, j: (b, 0, j)
    perb3 = lambda b, j: (b, 0, 0)
    return pl.pallas_call(
        functools.partial(_attn_kernel, ksel=ksel, idx_bits=max(1, (S - 1).bit_length())),
        grid=(B, nq),
        in_specs=[
            pl.BlockSpec((1, D_ATTN, Q_TILE), qcol),
            pl.BlockSpec((1, N_IDX_HEADS * IDX_DIM, Q_TILE), qcol),
            pl.BlockSpec((1, N_IDX_HEADS, Q_TILE), qcol),
            pl.BlockSpec((1, S, D_ATTN), perb3),
            pl.BlockSpec((1, S, LANES), perb3),
            pl.BlockSpec((1, nc, N_PAIRS, PV_ROWS, K_CHUNK), lambda b, j: (b, 0, 0, 0, 0)),
            pl.BlockSpec((1, Q_TILE, D_ATTN), lambda b, j: (b, j, 0)),
            pl.BlockSpec((N_PAR, 2, N_PAIRS, K_CHUNK, 2 * Q_TILE), lambda b, j: (0, 0, 0, 0, 0)),
        ],
        out_specs=pl.BlockSpec((1, Q_TILE, D_ATTN), lambda b, j: (b, j, 0)),
        out_shape=jax.ShapeDtypeStruct((B, S, D_ATTN), bf16),
        scratch_shapes=[
            pltpu.VMEM((nc, K_CHUNK, Q_TILE), f32),
            pltpu.VMEM((nc, K_CHUNK, Q_TILE), jnp.int32),
            pltpu.VMEM((nc, 32, SUB, Q_TILE), jnp.int32),
            pltpu.VMEM((N_PAIRS, nc, K_CHUNK, 2 * Q_TILE), f32),
            pltpu.VMEM((N_PAIRS, 2 * HEAD_DIM, 2 * Q_TILE), bf16),
            pltpu.VMEM((N_PAIRS, 2 * IDX_DIM, 2 * Q_TILE), bf16),
            pltpu.VMEM((N_PAIRS, PV_ROWS, 2 * Q_TILE), f32),
        ],
        compiler_params=pltpu.CompilerParams(
            dimension_semantics=("parallel", "arbitrary"), vmem_limit_bytes=VMEM_LIMIT),
        name="prompt_attention",
    )(qt, qit, wit, kb, ki2, vt, ga, btab)


CONV_HALO = 32
CONV_ROWS = 32


def _layer_norm_swish(c, lng_ref, lnb_ref):
    mu = jnp.mean(c, axis=-1, keepdims=True)
    d = c - mu
    var = jnp.mean(d * d, axis=-1, keepdims=True)
    return _silu(d * lax.rsqrt(var + EPS) * lng_ref[...] + lnb_ref[...])


def _merge_kernel(mixa_ref, ucur_ref, uprev_ref, gc_ref, x_ref, wo_ref, dw_ref, dwb_ref, lng_ref,
                  lnb_ref, y_ref, ext_ref, sh_ref, cg_ref, *, tq):
    i = pl.program_id(1)
    n = CONV_HALO + tq
    ext_ref[0:CONV_HALO] = jnp.where(i > 0, uprev_ref[0], 0.0)
    ext_ref[CONV_HALO:n] = ucur_ref[0]
    ext_ref[n:n + SUB] = jnp.zeros((SUB, D_CONV), f32)
    ext = ext_ref[...]
    for r in range(1, SUB):
        sh_ref[r - 1] = pltpu.roll(ext, n + SUB - r, 0)[0:n, :]
    first = CONV_HALO - (CONV_W - 1)
    for c in range(tq // CONV_ROWS):
        acc = jnp.broadcast_to(dwb_ref[...], (CONV_ROWS, D_CONV))
        for w in range(CONV_W):
            a, r = divmod(first + w, SUB)
            start = c * CONV_ROWS + a * SUB
            src = ext_ref if r == 0 else sh_ref.at[r - 1]
            acc = acc + src[start:start + CONV_ROWS, :] * jnp.tile(dw_ref[w], (CONV_ROWS // SUB, 1))
        rows = slice(c * CONV_ROWS, (c + 1) * CONV_ROWS)
        cg_ref[rows] = (_layer_norm_swish(acc, lng_ref, lnb_ref) * gc_ref[0, rows]).astype(bf16)
    y_ref[0] = (x_ref[0]
                + jnp.dot(mixa_ref[0], wo_ref[0:D_ATTN], preferred_element_type=f32)
                + jnp.dot(cg_ref[...], wo_ref[D_ATTN:D_ATTN + D_CONV], preferred_element_type=f32))


def _prompt_merge(mixa, u, gc, x, wo, dw, dwb, lng, lnb):
    B, S, D = x.shape
    tq = 256
    assert S % tq == 0 and tq % CONV_HALO == 0
    row = lambda b, i: (b, i, 0)
    const2 = lambda b, i: (0, 0)
    halo = lambda b, i: (b, jnp.maximum(i * (tq // CONV_HALO) - 1, 0), 0)
    return pl.pallas_call(
        functools.partial(_merge_kernel, tq=tq),
        grid=(B, S // tq),
        in_specs=[
            pl.BlockSpec((1, tq, D_ATTN), row),
            pl.BlockSpec((1, tq, D_CONV), row),
            pl.BlockSpec((1, CONV_HALO, D_CONV), halo),
            pl.BlockSpec((1, tq, D_CONV), row),
            pl.BlockSpec((1, tq, D), row),
            pl.BlockSpec((D_ATTN + D_CONV, D), const2),
            pl.BlockSpec((CONV_W, SUB, D_CONV), lambda b, i: (0, 0, 0)),
            pl.BlockSpec((1, D_CONV), const2),
            pl.BlockSpec((1, D_CONV), const2),
            pl.BlockSpec((1, D_CONV), const2),
        ],
        out_specs=pl.BlockSpec((1, tq, D), row),
        out_shape=jax.ShapeDtypeStruct((B, S, D), f32),
        scratch_shapes=[
            pltpu.VMEM((CONV_HALO + tq + SUB, D_CONV), f32),
            pltpu.VMEM((SUB - 1, CONV_HALO + tq, D_CONV), f32),
            pltpu.VMEM((tq, D_CONV), bf16),
        ],
        compiler_params=pltpu.CompilerParams(
            dimension_semantics=("parallel", "arbitrary"), vmem_limit_bytes=VMEM_LIMIT),
        name="prompt_merge",
    )(mixa, u, u, gc, x, wo, dw, dwb, lng, lnb)


SCORE_UNROLL = 16


def _sample_score_kernel(pt_ref, qi_ref, w_ref, cki_ref, out_ref, buf_ref, sem_ref, *, npg):
    b = pl.program_id(0)
    nb = pl.num_programs(0)
    slot = b % 2

    def page_copy(bb, p, sl):
        return pltpu.make_async_copy(cki_ref.at[pt_ref[bb, p]], buf_ref.at[sl, p], sem_ref.at[sl])

    def start_all(bb, sl):
        def body(pp, carry):
            page_copy(bb, 2 * pp, sl).start(priority=0)
            page_copy(bb, 2 * pp + 1, sl).start(priority=1)
            return carry
        lax.fori_loop(0, npg // 2, body, 0)

    @pl.when(b == 0)
    def _():
        start_all(0, 0)

    @pl.when(b + 1 < nb)
    def _():
        start_all(b + 1, 1 - slot)

    def wait_body(p, carry):
        page_copy(b, p, slot).wait()
        return carry
    lax.fori_loop(0, npg, wait_body, 0)

    qs = (qi_ref[0] * (IDX_DIM ** -0.5)).astype(bf16)
    wcol = w_ref[0] * (N_IDX_HEADS ** -0.5)
    wcol2 = jnp.concatenate([wcol, wcol], axis=1)

    def body(i, carry):
        for u in range(0, SCORE_UNROLL, 2):
            p = i * SCORE_UNROLL + u
            pages = jnp.concatenate([buf_ref[slot, p], buf_ref[slot, p + 1]], axis=1).astype(bf16)
            s = jnp.dot(qs, pages, preferred_element_type=f32)
            r = jnp.sum(jnp.maximum(s, 0.0) * wcol2, axis=0, keepdims=True)
            out_ref[0, pl.ds(p, 1), :] = r[:, :LANES]
            out_ref[0, pl.ds(p + 1, 1), :] = r[:, LANES:]
        return carry
    lax.fori_loop(0, npg // SCORE_UNROLL, body, 0)


def _sample_scores(page_table, qi3, wrow, cki_t):
    Bd, npg = page_table.shape
    page = cki_t.shape[2]
    assert page == LANES and npg % SCORE_UNROLL == 0
    grid_spec = pltpu.PrefetchScalarGridSpec(
        num_scalar_prefetch=1,
        grid=(Bd,),
        in_specs=[
            pl.BlockSpec((1, N_IDX_HEADS, IDX_DIM), lambda b, pt: (b, 0, 0)),
            pl.BlockSpec((1, N_IDX_HEADS, LANES), lambda b, pt: (b, 0, 0)),
            pl.BlockSpec(memory_space=pl.ANY),
        ],
        out_specs=pl.BlockSpec((1, npg, page), lambda b, pt: (b, 0, 0)),
        scratch_shapes=[
            pltpu.VMEM((2, npg, IDX_DIM, page), f32),
            pltpu.SemaphoreType.DMA((2,)),
        ],
    )
    return pl.pallas_call(
        functools.partial(_sample_score_kernel, npg=npg),
        grid_spec=grid_spec,
        out_shape=jax.ShapeDtypeStruct((Bd, npg, page), f32),
        compiler_params=pltpu.CompilerParams(
            dimension_semantics=("arbitrary",), vmem_limit_bytes=VMEM_LIMIT),
        name="sample_scores",
    )(page_table, qi3, wrow, cki_t)


def _sample_select_kernel(sc_ref, qi_ref, kit_ref, kw_ref, mask_ref, mnew_ref, key_ref, *,
                          npg, ksel, past, idx_bits):
    Bd = sc_ref.shape[0]
    shape = (Bd, npg, LANES)

    qb = (qi_ref[...] * (IDX_DIM ** -0.5)).astype(bf16).astype(f32)
    kb = kit_ref[...].astype(bf16).astype(f32)
    prod = qb * kb
    grp = lax.broadcasted_iota(jnp.int32, prod.shape, 1) // IDX_DIM
    kwv = kw_ref[...]
    s_new = jnp.zeros((Bd, 1), f32)
    for h in range(N_IDX_HEADS):
        sh = jnp.sum(jnp.where(grp == h, prod, 0.0), axis=1, keepdims=True)
        s_new = s_new + jnp.maximum(sh, 0.0) * (kwv[:, IDX_DIM + h:IDX_DIM + h + 1] * (N_IDX_HEADS ** -0.5))
    key_new = _to_key(s_new).reshape(Bd, 1, 1)

    key_ref[...] = _to_key(sc_ref[...])
    pos = (lax.broadcasted_iota(jnp.int32, shape, 1) * LANES
           + lax.broadcasted_iota(jnp.int32, shape, 2))

    def count(pred, pred_new):
        x = jnp.where(pred, 1.0, 0.0)
        s = jnp.sum(jnp.sum(x, axis=1, keepdims=True), axis=2, keepdims=True)
        return s + jnp.where(pred_new, 1.0, 0.0)

    def bit_body(i, lo):
        cand = lo + jnp.left_shift(jnp.int32(1), 31 - i)
        tot = count(key_ref[...] >= cand, key_new >= cand)
        return jnp.where(tot >= ksel, cand, lo)

    thr = lax.fori_loop(0, 32, bit_body, jnp.full((Bd, 1, 1), INT_MIN, jnp.int32))
    keys = key_ref[...]
    need = ksel - count(keys > thr, key_new > thr)
    big = jnp.int32(2 ** 30)
    eq_pos = jnp.where(keys == thr, pos, big)
    eq_pos_new = jnp.where(key_new == thr, jnp.int32(past), big)

    def cut_body(i, x):
        cand = x + jnp.left_shift(jnp.int32(1), idx_bits - 1 - i)
        g = count(eq_pos < cand, eq_pos_new < cand)
        return jnp.where(g < need, cand, x)

    cut = lax.fori_loop(0, idx_bits, cut_body, jnp.zeros((Bd, 1, 1), jnp.int32))
    mask_ref[...] = jnp.where(jnp.where(keys > thr, 0, eq_pos) <= cut, 0.0, NEG)
    sel_new = jnp.where(key_new > thr, 0, eq_pos_new) <= cut
    mnew_ref[...] = jnp.broadcast_to(jnp.where(sel_new, 0.0, NEG), mnew_ref.shape)


def _sample_select(scores, qi, ki_tiled, kw, *, past):
    Bd, npg, _ = scores.shape
    ksel = min(TOPK_MAX, (past + 1) // 4)
    vm = pl.BlockSpec(memory_space=pltpu.VMEM)
    return pl.pallas_call(
        functools.partial(_sample_select_kernel, npg=npg, ksel=ksel, past=past,
                          idx_bits=past.bit_length()),
        in_specs=[vm, vm, vm, vm],
        out_specs=(vm, vm),
        out_shape=(jax.ShapeDtypeStruct((Bd, npg, LANES), f32),
                   jax.ShapeDtypeStruct((Bd, N_HEADS, LANES), f32)),
        scratch_shapes=[pltpu.VMEM((Bd, npg, LANES), jnp.int32)],
        compiler_params=pltpu.CompilerParams(vmem_limit_bytes=VMEM_LIMIT),
        name="sample_select",
    )(scores, qi, ki_tiled, kw)


PAGE_BUFS = 32
PAGE_GROUP = 4


def _sample_attn_kernel(pt_ref, qbd_ref, knew_ref, vnew_ref, mask_ref, mnew_ref, rbt_ref, ck_ref, cv_ref,
                        out_ref, buf_ref, lg_ref, sem_ref, *, npg, past):
    b = pl.program_id(0)
    nb = pl.num_programs(0)
    per_b = 2 * npg
    total = nb * per_b

    def page_copy(g, src_ref):
        phys = pt_ref[g // per_b, g % npg]
        slot = g % PAGE_BUFS
        return pltpu.make_async_copy(src_ref.at[phys], buf_ref.at[slot], sem_ref.at[slot])

    def start(g):
        is_k = g % per_b < npg

        @pl.when(is_k)
        def _():
            page_copy(g, ck_ref).start()

        @pl.when(jnp.logical_not(is_k))
        def _():
            page_copy(g, cv_ref).start()

    def wait(g):
        page_copy(g, ck_ref).wait()

    @pl.when(b == 0)
    def _():
        for g in range(PAGE_BUFS):
            start(g)

    g0 = b * per_b
    qbd = (qbd_ref[0] * (HEAD_DIM ** -0.5 * LOG2E)).astype(bf16)
    blk = (lax.broadcasted_iota(jnp.int32, (N_HEADS, D_ATTN), 0)
           == lax.broadcasted_iota(jnp.int32, (N_HEADS, D_ATTN), 1) // HEAD_DIM)
    tok = lax.broadcasted_iota(jnp.int32, (N_HEADS, LANES), 1)

    def bias_of(dist):
        far = rbt_ref[N_BUCKETS - 1]
        acc = jnp.broadcast_to(rbt_ref[0] - far, dist.shape)
        for i, t in enumerate(BUCKET_THRESH):
            acc = jnp.where(dist >= t, rbt_ref[i + 1] - far, acc)
        return acc * LOG2E

    def refill(gs):
        for g in gs:
            @pl.when(g + PAGE_BUFS < total)
            def _():
                start(g + PAGE_BUFS)

    def k_body(it, m):
        gs = [g0 + it * PAGE_GROUP + u for u in range(PAGE_GROUP)]
        for g in gs:
            wait(g)
        for u, g in enumerate(gs):
            p = it * PAGE_GROUP + u
            kp = buf_ref[g % PAGE_BUFS].astype(bf16)
            s = jnp.dot(qbd, kp, preferred_element_type=f32) + mask_ref[0, pl.ds(p, 1), :]
            lg_ref[p] = s
            m = jnp.maximum(m, s)
        refill(gs)
        return m

    m = lax.fori_loop(0, npg // PAGE_GROUP, k_body, jnp.full((N_HEADS, LANES), NEG, f32))
    last = lg_ref[npg - 1] + bias_of(past - ((npg - 1) * LANES + tok))
    lg_ref[npg - 1] = last
    m = jnp.maximum(m, last)

    s_new = (jnp.sum(qbd_ref[0] * knew_ref[0], axis=1, keepdims=True) * (HEAD_DIM ** -0.5 * LOG2E)
             + bias_of(jnp.zeros((N_HEADS, LANES), jnp.int32))[:, 0:1] + mnew_ref[0][:, 0:1])
    m_row = jnp.maximum(jnp.max(m, axis=1, keepdims=True), s_new)
    mb = jnp.broadcast_to(m_row, (N_HEADS, LANES))

    def v_body(it, carry):
        acc, l = carry
        gs = [g0 + npg + it * PAGE_GROUP + u for u in range(PAGE_GROUP)]
        for g in gs:
            wait(g)
        for u, g in enumerate(gs):
            vp = buf_ref[g % PAGE_BUFS].astype(bf16)
            pr = jnp.exp2(lg_ref[it * PAGE_GROUP + u] - mb)
            acc = acc + lax.dot_general(pr.astype(bf16), vp, (((1,), (1,)), ((), ())),
                                        preferred_element_type=f32)
            l = l + pr
        refill(gs)
        return acc, l

    acc, l = lax.fori_loop(0, npg // PAGE_GROUP, v_body, (jnp.zeros((N_HEADS, D_ATTN), f32),
                                                          jnp.zeros((N_HEADS, LANES), f32)))
    p_new = jnp.exp2(s_new - m_row)
    l_tot = jnp.sum(l, axis=1, keepdims=True) + p_new
    num = jnp.where(blk, acc + p_new * vnew_ref[0], 0.0) / l_tot
    out_ref[0] = jnp.sum(num, axis=0, keepdims=True)


def _sample_attention(page_table, qbd, k_new_flat, v_new_flat, mask, mnew, rbt, ck_t, cv_t, *, past):
    Bd, npg = page_table.shape
    rows, page = ck_t.shape[1], ck_t.shape[2]
    assert rows == D_ATTN and page == LANES
    assert npg % PAGE_GROUP == 0 and PAGE_BUFS % PAGE_GROUP == 0 and 2 * npg >= PAGE_BUFS
    grid_spec = pltpu.PrefetchScalarGridSpec(
        num_scalar_prefetch=1,
        grid=(Bd,),
        in_specs=[
            pl.BlockSpec((1, N_HEADS, D_ATTN), lambda b, pt: (b, 0, 0)),
            pl.BlockSpec((1, 1, D_ATTN), lambda b, pt: (b, 0, 0)),
            pl.BlockSpec((1, 1, D_ATTN), lambda b, pt: (b, 0, 0)),
            pl.BlockSpec((1, npg, LANES), lambda b, pt: (b, 0, 0)),
            pl.BlockSpec((1, N_HEADS, LANES), lambda b, pt: (b, 0, 0)),
            pl.BlockSpec((N_BUCKETS, N_HEADS, LANES), lambda b, pt: (0, 0, 0)),
            pl.BlockSpec(memory_space=pl.ANY),
            pl.BlockSpec(memory_space=pl.ANY),
        ],
        out_specs=pl.BlockSpec((1, 1, D_ATTN), lambda b, pt: (b, 0, 0)),
        scratch_shapes=[
            pltpu.VMEM((PAGE_BUFS, rows, page), f32),
            pltpu.VMEM((npg, N_HEADS, LANES), f32),
            pltpu.SemaphoreType.DMA((PAGE_BUFS,)),
        ],
    )
    return pl.pallas_call(
        functools.partial(_sample_attn_kernel, npg=npg, past=past),
        grid_spec=grid_spec,
        out_shape=jax.ShapeDtypeStruct((Bd, 1, D_ATTN), f32),
        compiler_params=pltpu.CompilerParams(
            dimension_semantics=("arbitrary",), vmem_limit_bytes=VMEM_LIMIT),
        name="sample_attention",
    )(page_table, qbd, k_new_flat, v_new_flat, mask, mnew, rbt, ck_t, cv_t)


def _sample_merge_kernel(attn_ref, ga_ref, u_ref, gc_ref, st_ref, x_ref, wo_ref, dw_ref, dwb_ref,
                         lng_ref, lnb_ref, y_ref):
    acc = dwb_ref[...] + u_ref[...] * dw_ref[CONV_W - 1:CONV_W, :]
    for w in range(CONV_W - 1):
        acc = acc + st_ref[w] * dw_ref[w:w + 1, :]
    cg = (_layer_norm_swish(acc, lng_ref, lnb_ref) * gc_ref[...]).astype(bf16)
    mixa = (attn_ref[...] * ga_ref[...]).astype(bf16)
    y_ref[...] = (x_ref[...]
                  + jnp.dot(mixa, wo_ref[0:D_ATTN], preferred_element_type=f32)
                  + jnp.dot(cg, wo_ref[D_ATTN:D_ATTN + D_CONV], preferred_element_type=f32))


def _sample_merge(attn, ga, u, gc, state_t, x, wo, dw, dwb, lng, lnb):
    vm = pl.BlockSpec(memory_space=pltpu.VMEM)
    return pl.pallas_call(
        _sample_merge_kernel,
        in_specs=[vm] * 11,
        out_specs=vm,
        out_shape=jax.ShapeDtypeStruct(x.shape, f32),
        compiler_params=pltpu.CompilerParams(vmem_limit_bytes=VMEM_LIMIT),
        name="sample_merge",
    )(attn, ga, u, gc, state_t, x, wo, dw, dwb, lng, lnb)


def kernel(x_prompt, x_sample, cache_k, cache_v, cache_k_idx, state_conv, page_table, rel_bias, norm_g,
           w_in, q_norm_g, k_norm_g, dw_w, dw_b, ln_g, ln_b, w_out):
    depth = norm_g.shape[0]
    assert depth == 1, "single-layer step"
    B, S, D = x_prompt.shape
    Bd, T, _ = x_sample.shape
    assert T == 1
    n_phys, page = cache_k.shape[1], cache_k.shape[2]
    past = page_table.shape[1] * page

    w = w_in[0]
    c_ki = 2 * D_ATTN + 2 * N_HEADS * HEAD_DIM + N_IDX_HEADS * IDX_DIM
    c_conv = c_ki + IDX_DIM + N_IDX_HEADS
    pad = jnp.zeros((D, LANES - IDX_DIM - N_IDX_HEADS), w.dtype)
    w_all = jnp.concatenate([w[:, :c_ki], w[:, c_ki:c_conv], pad, w[:, c_conv:]], axis=1).astype(bf16)
    assert w_all.shape[1] == _C_END
    wo = w_out[0].astype(bf16)
    g = norm_g[0][None]
    qg = jnp.tile(q_norm_g[0], N_HEADS)[None]
    kg = jnp.tile(k_norm_g[0], N_HEADS)[None]
    hid = np.arange(D_ATTN) // HEAD_DIM
    bd = jnp.asarray((hid[:, None] == hid[None, :]) / HEAD_DIM, dtype=bf16)
    dwb, lng, lnb = dw_b[0][None], ln_g[0][None], ln_b[0][None]

    (qt, k_p, kb, v_p, vt, ga_p, qit, kw_p, wit, ki2, u_p, gc_p) = _inproj(
        x_prompt, g, w_all, qg, kg, bd, prompt=True)
    btab = _bias_tables(rel_bias)
    mixa = _prompt_attention(qt, qit, wit, kb, ki2, vt, ga_p, btab)
    dw8 = jnp.broadcast_to(dw_w[0][:, None, :], (CONV_W, SUB, D_CONV))
    y_prompt = _prompt_merge(mixa, u_p, gc_p, x_prompt, wo, dw8, dwb, lng, lnb)

    xs = x_sample.reshape(1, Bd, D)
    q_s, k_s, v_s, ga_s, qi_s, kw_s, u_s, gc_s = (a[0] for a in _inproj(
        xs, g, w_all, qg, kg, bd, prompt=False))
    ki_s = kw_s[:, :IDX_DIM]
    wrow = jnp.broadcast_to(kw_s[:, IDX_DIM:IDX_DIM + N_IDX_HEADS, None], (Bd, N_IDX_HEADS, LANES))
    cki_t = jnp.transpose(cache_k_idx[0], (0, 2, 1))
    ck_t = jnp.transpose(cache_k[0], (0, 2, 3, 1)).reshape(n_phys, D_ATTN, page)
    cv_t = jnp.transpose(cache_v[0], (0, 2, 3, 1)).reshape(n_phys, D_ATTN, page)
    scores = _sample_scores(page_table, qi_s.reshape(Bd, N_IDX_HEADS, IDX_DIM), wrow, cki_t)
    mask, mnew = _sample_select(scores, qi_s, jnp.tile(ki_s, (1, N_IDX_HEADS)), kw_s, past=past)
    rbt = jnp.broadcast_to(rel_bias[:, :, None], (N_BUCKETS, N_HEADS, LANES))
    eye = jnp.asarray(np.arange(N_HEADS)[:, None] == hid[None, :], dtype=f32)
    qbd = q_s[:, None, :] * eye[None]
    attn_s = _sample_attention(page_table, qbd, k_s[:, None, :], v_s[:, None, :],
                               mask, mnew, rbt, ck_t, cv_t, past=past)
    state_t = jnp.transpose(state_conv[0], (1, 0, 2))
    y_sample = _sample_merge(attn_s[:, 0], ga_s, u_s, gc_s, state_t, x_sample[:, 0], wo,
                             dw_w[0], dwb, lng, lnb)

    heads = (N_HEADS, HEAD_DIM)
    return (
        y_prompt,
        y_sample[:, None],
        k_p.reshape((1, B, S) + heads),
        v_p.reshape((1, B, S) + heads),
        kw_p[None, :, :, :IDX_DIM],
        u_p[None, :, S - (CONV_W - 1):],
        k_s.reshape((1, Bd, 1) + heads),
        v_s.reshape((1, Bd, 1) + heads),
        ki_s.reshape(1, Bd, 1, IDX_DIM),
        jnp.concatenate([state_conv[0][:, 1:], u_s[:, None]], axis=1)[None],
    )
```

```python
import functools
import math

import numpy as np
import jax
import jax.numpy as jnp
from jax import lax
from jax.experimental import pallas as pl
from jax.experimental.pallas import tpu as pltpu

N_HEADS = 8
HEAD_DIM = 64
N_IDX_HEADS = 8
IDX_DIM = 64
D_ATTN = N_HEADS * HEAD_DIM
D_CONV = 512
CONV_W = 31
TOPK_MAX = 256
N_BUCKETS = 32
MAX_DISTANCE = 128
EPS = 1e-6
NEG = -1e30
INT_MIN = -2 ** 31
LOG2E = math.log2(math.e)

LANES = 128
Q_TILE = 256
K_CHUNK = 256
PV_ROWS = 144
VMEM_LIMIT = 56 * 1024 * 1024

f32 = jnp.float32
bf16 = jnp.bfloat16


def _bucket_thresholds():
    n = np.arange(0, 4 * MAX_DISTANCE)
    max_exact = N_BUCKETS // 2
    nf = np.maximum(n, 1).astype(np.float32)
    large = max_exact + (np.log(nf / np.float32(max_exact)) / np.float32(math.log(MAX_DISTANCE / max_exact))
                         * np.float32(N_BUCKETS - max_exact)).astype(np.int32)
    bucket = np.where(n < max_exact, n, np.minimum(large, N_BUCKETS - 1))
    return [int(n[bucket >= i].min()) for i in range(1, N_BUCKETS)]


BUCKET_THRESH = _bucket_thresholds()
assert BUCKET_THRESH[-1] <= K_CHUNK + 1 and K_CHUNK % Q_TILE == 0
N_PAR = K_CHUNK // Q_TILE


def _silu(x):
    return x * jax.nn.sigmoid(x)


def _to_key(x):
    b = pltpu.bitcast(x, jnp.int32)
    return jnp.where(b < 0, b ^ jnp.int32(0x7FFFFFFF), b)


_C_Q, _C_K, _C_V, _C_GA, _C_QI, _C_KW, _C_A, _C_B, _C_GC, _C_END = (
    0, 512, 1024, 1536, 2048, 2560, 2688, 3200, 3712, 4224)


def _inproj_kernel(x_ref, g_ref, w_ref, qg_ref, kg_ref, bd_ref, *outs, prompt, tm):
    xf = x_ref[0]
    ms = jnp.mean(xf * xf, axis=-1, keepdims=True)
    xn = (xf * lax.rsqrt(ms + EPS) * g_ref[...]).astype(bf16)

    def proj(a, b):
        return jnp.dot(xn, w_ref[:, a:b], preferred_element_type=f32)

    def head_norm(z, gain_ref):
        msq = jnp.dot((z * z).astype(bf16), bd_ref[...], preferred_element_type=f32)
        return z * lax.rsqrt(msq + EPS) * gain_ref[...]

    q = head_norm(proj(_C_Q, _C_K), qg_ref)
    k = head_norm(proj(_C_K, _C_V), kg_ref)
    v = proj(_C_V, _C_GA)
    ga = _silu(proj(_C_GA, _C_QI))
    qi = proj(_C_QI, _C_KW)
    kw = proj(_C_KW, _C_A)
    u = proj(_C_A, _C_B) * jax.nn.sigmoid(proj(_C_B, _C_GC))
    gc = _silu(proj(_C_GC, _C_END))

    if prompt:
        q_o, k_o, kb_o, v_o, vt_o, ga_o, qi_o, kw_o, wi_o, ki2_o, u_o, gc_o = outs
        q_o[0] = (q * (HEAD_DIM ** -0.5 * LOG2E)).T.astype(bf16)
        qi_o[0] = (qi * (IDX_DIM ** -0.5)).T.astype(bf16)
        wi_o[0] = kw.T[IDX_DIM:IDX_DIM + N_IDX_HEADS, :]
        kb_o[0] = k.astype(bf16)
        lane = lax.broadcasted_iota(jnp.int32, kw.shape, 1)
        ki2_o[0] = jnp.where(lane < IDX_DIM, kw, pltpu.roll(kw, IDX_DIM, 1)).astype(bf16)
        vt = v.T.astype(bf16)
        tail = jnp.where(lax.broadcasted_iota(jnp.int32, (PV_ROWS - 2 * HEAD_DIM, K_CHUNK), 0) < 8,
                         1.0, 0.0).astype(bf16)
        for c in range(tm // K_CHUNK):
            for i in range(N_HEADS // 2):
                blk = vt[i * 2 * HEAD_DIM:(i + 1) * 2 * HEAD_DIM, c * K_CHUNK:(c + 1) * K_CHUNK]
                vt_o[0, c, i] = jnp.concatenate([blk, tail], axis=0)
    else:
        q_o, k_o, v_o, ga_o, qi_o, kw_o, u_o, gc_o = outs
        q_o[0] = q
        qi_o[0] = qi
    k_o[0] = k
    v_o[0] = v
    ga_o[0] = ga
    kw_o[0] = kw
    u_o[0] = u
    gc_o[0] = gc


def _inproj(x, norm_g, w_all, qg, kg, bd, *, prompt):
    B, S, D = x.shape
    tm = 512 if prompt else S
    assert S % tm == 0 and (not prompt or tm % K_CHUNK == 0)
    grid = (B, S // tm)
    row = lambda b, i: (b, i, 0)
    const2 = lambda b, i: (0, 0)
    col = lambda b, i: (b, 0, i)
    nck = tm // K_CHUNK
    if prompt:
        out_shape = (
            jax.ShapeDtypeStruct((B, D_ATTN, S), bf16),
            jax.ShapeDtypeStruct((B, S, D_ATTN), f32),
            jax.ShapeDtypeStruct((B, S, D_ATTN), bf16),
            jax.ShapeDtypeStruct((B, S, D_ATTN), f32),
            jax.ShapeDtypeStruct((B, S // K_CHUNK, N_HEADS // 2, PV_ROWS, K_CHUNK), bf16),
            jax.ShapeDtypeStruct((B, S, D_ATTN), f32),
            jax.ShapeDtypeStruct((B, N_IDX_HEADS * IDX_DIM, S), bf16),
            jax.ShapeDtypeStruct((B, S, LANES), f32),
            jax.ShapeDtypeStruct((B, N_IDX_HEADS, S), f32),
            jax.ShapeDtypeStruct((B, S, LANES), bf16),
            jax.ShapeDtypeStruct((B, S, D_CONV), f32),
            jax.ShapeDtypeStruct((B, S, D_CONV), f32),
        )
        out_specs = (
            pl.BlockSpec((1, D_ATTN, tm), col),
            pl.BlockSpec((1, tm, D_ATTN), row),
            pl.BlockSpec((1, tm, D_ATTN), row),
            pl.BlockSpec((1, tm, D_ATTN), row),
            pl.BlockSpec((1, nck, N_HEADS // 2, PV_ROWS, K_CHUNK), lambda b, i: (b, i, 0, 0, 0)),
            pl.BlockSpec((1, tm, D_ATTN), row),
            pl.BlockSpec((1, N_IDX_HEADS * IDX_DIM, tm), col),
            pl.BlockSpec((1, tm, LANES), row),
            pl.BlockSpec((1, N_IDX_HEADS, tm), col),
            pl.BlockSpec((1, tm, LANES), row),
            pl.BlockSpec((1, tm, D_CONV), row),
            pl.BlockSpec((1, tm, D_CONV), row),
        )
    else:
        out_shape = (
            jax.ShapeDtypeStruct((B, S, D_ATTN), f32),
            jax.ShapeDtypeStruct((B, S, D_ATTN), f32),
            jax.ShapeDtypeStruct((B, S, D_ATTN), f32),
            jax.ShapeDtypeStruct((B, S, D_ATTN), f32),
            jax.ShapeDtypeStruct((B, S, D_ATTN), f32),
            jax.ShapeDtypeStruct((B, S, LANES), f32),
            jax.ShapeDtypeStruct((B, S, D_CONV), f32),
            jax.ShapeDtypeStruct((B, S, D_CONV), f32),
        )
        out_specs = tuple(pl.BlockSpec((1, tm, s.shape[-1]), row) for s in out_shape)
    return pl.pallas_call(
        functools.partial(_inproj_kernel, prompt=prompt, tm=tm),
        grid=grid,
        in_specs=[
            pl.BlockSpec((1, tm, D), row),
            pl.BlockSpec((1, D), const2),
            pl.BlockSpec((D, _C_END), const2),
            pl.BlockSpec((1, D_ATTN), const2),
            pl.BlockSpec((1, D_ATTN), const2),
            pl.BlockSpec((D_ATTN, D_ATTN), const2),
        ],
        out_specs=out_specs,
        out_shape=out_shape,
        compiler_params=pltpu.CompilerParams(
            dimension_semantics=("parallel", "parallel"), vmem_limit_bytes=VMEM_LIMIT),
        name="inproj_prompt" if prompt else "inproj_sample",
    )(x, norm_g, w_all, qg, kg, bd)


def _bias_table_kernel(rb_ref, out_ref):
    shape = (K_CHUNK, Q_TILE)
    krow = lax.broadcasted_iota(jnp.int32, shape, 0)
    qcol = lax.broadcasted_iota(jnp.int32, shape, 1)
    for par in range(N_PAR):
        for which in range(2):
            dist = Q_TILE * par + K_CHUNK * (1 - which) + qcol - krow
            for h in range(N_HEADS):
                far = rb_ref[N_BUCKETS - 1, h]
                acc = jnp.full(shape, (rb_ref[0, h] - far) * LOG2E, f32)
                for i, t in enumerate(BUCKET_THRESH):
                    acc = jnp.where(dist >= t, (rb_ref[i + 1, h] - far) * LOG2E, acc)
                out_ref[par, which, h // 2, :, (h % 2) * Q_TILE:(h % 2 + 1) * Q_TILE] = (
                    jnp.where(dist >= 0, acc, NEG))


def _bias_tables(rel_bias):
    return pl.pallas_call(
        _bias_table_kernel,
        in_specs=[pl.BlockSpec(memory_space=pltpu.SMEM)],
        out_specs=pl.BlockSpec(memory_space=pltpu.VMEM),
        out_shape=jax.ShapeDtypeStruct((N_PAR, 2, N_HEADS // 2, K_CHUNK, 2 * Q_TILE), f32),
        name="bias_tables",
    )(rel_bias)


N_PAIRS = N_HEADS // 2
SUB = 8


COUNT_ROWS = 32
MAX_ROWS = 16


def _bit_planes(words):
    a = list(words)
    j, m = 16, 0x0000FFFF
    while j:
        k = 0
        while k < 32:
            t = (a[k] ^ lax.shift_right_logical(a[k + j], jnp.int32(j))) & jnp.int32(m)
            a[k] = a[k] ^ t
            a[k + j] = a[k + j] ^ jnp.left_shift(t, jnp.int32(j))
            k = (k + j + 1) & ~j
        j >>= 1
        m = (m ^ (m << j)) & 0xFFFFFFFF
    return a


def _fold(x, op, rows):
    return op(x.reshape(x.shape[0] // rows, rows, x.shape[1]), axis=0)


def _attn_kernel(qt_ref, qit_ref, wit_ref, k_ref, ki_ref, vt_ref, ga_ref, bt_ref, out_ref,
                 sc_ref, key_ref, pl_ref, lg_ref, qbd_ref, qibd_ref, o_ref, *, ksel, idx_bits):
    j = pl.program_id(1)
    n_chunks = sc_ref.shape[0]
    nck = j // N_PAR + 1
    par = j % N_PAR
    shape = (K_CHUNK, Q_TILE)
    key_pos = lax.broadcasted_iota(jnp.int32, shape, 0)
    q_pos = j * Q_TILE + lax.broadcasted_iota(jnp.int32, shape, 1)

    top = lax.broadcasted_iota(jnp.int32, (2 * HEAD_DIM, Q_TILE), 0) < HEAD_DIM
    for i in range(N_PAIRS):
        rows = slice(i * 2 * HEAD_DIM, (i + 1) * 2 * HEAD_DIM)
        for src, dst in ((qt_ref, qbd_ref), (qit_ref, qibd_ref)):
            x = src[0, rows, :]
            zero = jnp.zeros_like(x)
            dst[i] = jnp.concatenate([jnp.where(top, x, zero), jnp.where(top, zero, x)], axis=1)
    wi = wit_ref[0] * (N_IDX_HEADS ** -0.5)
    wpair = [jnp.concatenate([wi[2 * i:2 * i + 1], wi[2 * i + 1:2 * i + 2]], axis=1)
             for i in range(N_PAIRS)]

    def chunk_rows(c):
        return pl.ds(pl.multiple_of(c * K_CHUNK, K_CHUNK), K_CHUNK)

    def for_chunk_pairs(n, body):
        def pair(cc, carry):
            body([2 * cc, 2 * cc + 1])
            return carry
        lax.fori_loop(0, n // 2, pair, 0)

        @pl.when(n % 2 == 1)
        def _():
            body([n - 1])

    def score_chunks(chunks):
        for c in chunks:
            ki2 = ki_ref[0, chunk_rows(c), :]
            acc = jnp.zeros(shape, f32)
            for i in range(N_PAIRS):
                r = jnp.maximum(jnp.dot(ki2, qibd_ref[i], preferred_element_type=f32), 0.0) * wpair[i]
                acc = acc + r[:, :Q_TILE] + r[:, Q_TILE:]
            sc_ref[c] = jnp.where(c * K_CHUNK + key_pos <= q_pos, acc, -jnp.inf)

    for_chunk_pairs(nck, score_chunks)

    def count(pred):
        def body(c, cnt):
            hit = pred(c)
            for g in range(K_CHUNK // COUNT_ROWS):
                rows = slice(g * COUNT_ROWS, (g + 1) * COUNT_ROWS)
                cnt = jnp.where(hit[rows], cnt + 1.0, cnt)
            return cnt
        cnt = lax.fori_loop(0, nck, body, jnp.zeros((COUNT_ROWS, Q_TILE), f32))
        return jnp.sum(cnt, axis=0, keepdims=True)

    def write_mask(pred):
        def body(c, carry):
            sc_ref[c] = jnp.where(pred(c), 0.0, NEG)
            return carry
        lax.fori_loop(0, nck, body, 0)

    need_search = (j + 1) * Q_TILE > ksel

    @pl.when(jnp.logical_not(need_search))
    def _():
        write_mask(lambda c: c * K_CHUNK + key_pos < (j + 1) * Q_TILE)

    @pl.when(need_search)
    def _():
        def key_body(c, carry):
            keys = _to_key(sc_ref[c])
            key_ref[c] = keys
            ukeys = keys ^ jnp.int32(INT_MIN)
            planes = _bit_planes([ukeys[t * SUB:(t + 1) * SUB, :] for t in range(32)])
            for p in range(32):
                pl_ref[c, p] = planes[p]
            return carry
        lax.fori_loop(0, nck, key_body, 0)

        def clear_body(c, carry):
            pl_ref[c] = jnp.zeros(pl_ref.shape[1:], jnp.int32)
            return carry
        lax.fori_loop(nck, n_chunks, clear_body, 0)

        def bit_body(p, carry):
            n_above, thr_u, alive = carry
            ones = [alive[c] & pl_ref[c, p] for c in range(n_chunks)]
            acc = lax.population_count(ones[0])
            for c in range(1, n_chunks):
                acc = acc + lax.population_count(ones[c])
            n_hi = n_above + jnp.sum(acc.astype(f32), axis=0, keepdims=True)
            take = n_hi >= ksel
            takeb = jnp.broadcast_to(take, (SUB, Q_TILE))
            alive = tuple(jnp.where(takeb, ones[c], alive[c] ^ ones[c]) for c in range(n_chunks))
            bit = jnp.left_shift(jnp.int32(1), 31 - p)
            return jnp.where(take, n_above, n_hi), jnp.where(take, thr_u | bit, thr_u), alive

        everything = jnp.full((SUB, Q_TILE), -1, jnp.int32)
        n_above, thr_u, alive = lax.fori_loop(
            0, 32, bit_body,
            (jnp.zeros((1, Q_TILE), f32), jnp.zeros((1, Q_TILE), jnp.int32),
             tuple(jnp.where(c < nck, everything, 0) for c in range(n_chunks))))
        n_eq = lax.population_count(alive[0])
        for c in range(1, n_chunks):
            n_eq = n_eq + lax.population_count(alive[c])
        n_ge = n_above + jnp.sum(n_eq.astype(f32), axis=0, keepdims=True)
        thr = thr_u ^ jnp.int32(INT_MIN)
        thrb = jnp.broadcast_to(thr, shape)
        has_tie = jnp.max(n_ge) > ksel

        @pl.when(jnp.logical_not(has_tie))
        def _():
            write_mask(lambda c: key_ref[c] >= thrb)

        @pl.when(has_tie)
        def _():
            need = ksel - count(lambda c: key_ref[c] > thrb)

            def cut_body(i, x):
                cand = x + jnp.left_shift(jnp.int32(1), idx_bits - 1 - i)
                candb = jnp.broadcast_to(cand, shape)
                g = count(lambda c: jnp.where(key_ref[c] == thrb, c * K_CHUNK + key_pos, 2 ** 30) < candb)
                return jnp.where(g < need, cand, x)

            cut = lax.fori_loop(0, idx_bits, cut_body, jnp.zeros((1, Q_TILE), jnp.int32))
            cutb = jnp.broadcast_to(cut, shape)

            def sel(c):
                kc = key_ref[c]
                eq_idx = jnp.where(kc == thrb, c * K_CHUNK + key_pos, 2 ** 30)
                return jnp.where(kc > thrb, 0, eq_idx) <= cutb
            write_mask(sel)

    def logits_chunk(c, which, ms):
        mask = sc_ref[c]
        mask2 = jnp.concatenate([mask, mask], axis=1)
        out = []
        for i in range(N_PAIRS):
            kc = k_ref[0, chunk_rows(c), i * 2 * HEAD_DIM:(i + 1) * 2 * HEAD_DIM]
            s = jnp.dot(kc, qbd_ref[i], preferred_element_type=f32) + mask2
            if which is not None:
                s = s + bt_ref[par, which, i]
            lg_ref[i, c] = s
            out.append(jnp.maximum(ms[i], _fold(s, jnp.max, MAX_ROWS)))
        return tuple(out)

    ms = tuple(jnp.full((MAX_ROWS, 2 * Q_TILE), NEG, f32) for _ in range(N_PAIRS))
    nfar = jnp.maximum(nck - 2, 0)
    ms = lax.fori_loop(
        0, nfar // 2, lambda cc, ms: logits_chunk(2 * cc + 1, None, logits_chunk(2 * cc, None, ms)), ms)
    ms = lax.cond(nfar % 2 == 1, lambda ms: logits_chunk(nfar - 1, None, ms), lambda ms: ms, ms)
    ms = lax.cond(nck >= 2,
                  lambda ms: logits_chunk(nck - 1, 1, logits_chunk(nck - 2, 0, ms)),
                  lambda ms: logits_chunk(nck - 1, 1, ms), ms)
    mrow = [jnp.max(m, axis=0, keepdims=True) for m in ms]

    o_ref[...] = jnp.zeros(o_ref.shape, f32)

    def pv_chunks(chunks):
        for i in range(N_PAIRS):
            acc = None
            for c in chunks:
                p = jnp.exp2((lg_ref[i, c] - mrow[i]).astype(bf16))
                d = jnp.dot(vt_ref[0, c, i], p, preferred_element_type=f32)
                acc = d if acc is None else acc + d
            o_ref[i] += acc
    for_chunk_pairs(nck, pv_chunks)

    pieces = []
    for i in range(N_PAIRS):
        o = o_ref[i]
        l = o[2 * HEAD_DIM:2 * HEAD_DIM + 1, :]
        pieces.append(o[0:HEAD_DIM, 0:Q_TILE] / l[:, 0:Q_TILE])
        pieces.append(o[HEAD_DIM:2 * HEAD_DIM, Q_TILE:] / l[:, Q_TILE:])
    out_ref[0] = (jnp.concatenate(pieces, axis=0).T * ga_ref[0]).astype(bf16)


def _prompt_attention(qt, qit, wit, kb, ki2, vt, ga, btab):
    B, S, _ = kb.shape
    assert S % K_CHUNK == 0
    nq = S // Q_TILE
    nc = S // K_CHUNK
    ksel = min(TOPK_MAX, S // 4)
    qcol = lambda b, j: (b, 0, j)
    perb3 = lambda b, j: (b, 0, 0)
    return pl.pallas_call(
        functools.partial(_attn_kernel, ksel=ksel, idx_bits=max(1, (S - 1).bit_length())),
        grid=(B, nq),
        in_specs=[
            pl.BlockSpec((1, D_ATTN, Q_TILE), qcol),
            pl.BlockSpec((1, N_IDX_HEADS * IDX_DIM, Q_TILE), qcol),
            pl.BlockSpec((1, N_IDX_HEADS, Q_TILE), qcol),
            pl.BlockSpec((1, S, D_ATTN), perb3),
            pl.BlockSpec((1, S, LANES), perb3),
            pl.BlockSpec((1, nc, N_PAIRS, PV_ROWS, K_CHUNK), lambda b, j: (b, 0, 0, 0, 0)),
            pl.BlockSpec((1, Q_TILE, D_ATTN), lambda b, j: (b, j, 0)),
            pl.BlockSpec((N_PAR, 2, N_PAIRS, K_CHUNK, 2 * Q_TILE), lambda b, j: (0, 0, 0, 0, 0)),
        ],
        out_specs=pl.BlockSpec((1, Q_TILE, D_ATTN), lambda b, j: (b, j, 0)),
        out_shape=jax.ShapeDtypeStruct((B, S, D_ATTN), bf16),
        scratch_shapes=[
            pltpu.VMEM((nc, K_CHUNK, Q_TILE), f32),
            pltpu.VMEM((nc, K_CHUNK, Q_TILE), jnp.int32),
            pltpu.VMEM((nc, 32, SUB, Q_TILE), jnp.int32),
            pltpu.VMEM((N_PAIRS, nc, K_CHUNK, 2 * Q_TILE), f32),
            pltpu.VMEM((N_PAIRS, 2 * HEAD_DIM, 2 * Q_TILE), bf16),
            pltpu.VMEM((N_PAIRS, 2 * IDX_DIM, 2 * Q_TILE), bf16),
            pltpu.VMEM((N_PAIRS, PV_ROWS, 2 * Q_TILE), f32),
        ],
        compiler_params=pltpu.CompilerParams(
            dimension_semantics=("parallel", "arbitrary"), vmem_limit_bytes=VMEM_LIMIT),
        name="prompt_attention",
    )(qt, qit, wit, kb, ki2, vt, ga, btab)


CONV_HALO = 32
CONV_ROWS = 32


def _layer_norm_swish(c, lng_ref, lnb_ref):
    mu = jnp.mean(c, axis=-1, keepdims=True)
    d = c - mu
    var = jnp.mean(d * d, axis=-1, keepdims=True)
    return _silu(d * lax.rsqrt(var + EPS) * lng_ref[...] + lnb_ref[...])


def _merge_kernel(mixa_ref, ucur_ref, uprev_ref, gc_ref, x_ref, wo_ref, dw_ref, dwb_ref, lng_ref,
                  lnb_ref, y_ref, ext_ref, sh_ref, cg_ref, *, tq):
    i = pl.program_id(1)
    n = CONV_HALO + tq
    ext_ref[0:CONV_HALO] = jnp.where(i > 0, uprev_ref[0], 0.0)
    ext_ref[CONV_HALO:n] = ucur_ref[0]
    ext_ref[n:n + SUB] = jnp.zeros((SUB, D_CONV), f32)
    ext = ext_ref[...]
    for r in range(1, SUB):
        sh_ref[r - 1] = pltpu.roll(ext, n + SUB - r, 0)[0:n, :]
    first = CONV_HALO - (CONV_W - 1)
    for c in range(tq // CONV_ROWS):
        acc = jnp.broadcast_to(dwb_ref[...], (CONV_ROWS, D_CONV))
        for w in range(CONV_W):
            a, r = divmod(first + w, SUB)
            start = c * CONV_ROWS + a * SUB
            src = ext_ref if r == 0 else sh_ref.at[r - 1]
            acc = acc + src[start:start + CONV_ROWS, :] * jnp.tile(dw_ref[w], (CONV_ROWS // SUB, 1))
        rows = slice(c * CONV_ROWS, (c + 1) * CONV_ROWS)
        cg_ref[rows] = (_layer_norm_swish(acc, lng_ref, lnb_ref) * gc_ref[0, rows]).astype(bf16)
    y_ref[0] = (x_ref[0]
                + jnp.dot(mixa_ref[0], wo_ref[0:D_ATTN], preferred_element_type=f32)
                + jnp.dot(cg_ref[...], wo_ref[D_ATTN:D_ATTN + D_CONV], preferred_element_type=f32))


def _prompt_merge(mixa, u, gc, x, wo, dw, dwb, lng, lnb):
    B, S, D = x.shape
    tq = 256
    assert S % tq == 0 and tq % CONV_HALO == 0
    row = lambda b, i: (b, i, 0)
    const2 = lambda b, i: (0, 0)
    halo = lambda b, i: (b, jnp.maximum(i * (tq // CONV_HALO) - 1, 0), 0)
    return pl.pallas_call(
        functools.partial(_merge_kernel, tq=tq),
        grid=(B, S // tq),
        in_specs=[
            pl.BlockSpec((1, tq, D_ATTN), row),
            pl.BlockSpec((1, tq, D_CONV), row),
            pl.BlockSpec((1, CONV_HALO, D_CONV), halo),
            pl.BlockSpec((1, tq, D_CONV), row),
            pl.BlockSpec((1, tq, D), row),
            pl.BlockSpec((D_ATTN + D_CONV, D), const2),
            pl.BlockSpec((CONV_W, SUB, D_CONV), lambda b, i: (0, 0, 0)),
            pl.BlockSpec((1, D_CONV), const2),
            pl.BlockSpec((1, D_CONV), const2),
            pl.BlockSpec((1, D_CONV), const2),
        ],
        out_specs=pl.BlockSpec((1, tq, D), row),
        out_shape=jax.ShapeDtypeStruct((B, S, D), f32),
        scratch_shapes=[
            pltpu.VMEM((CONV_HALO + tq + SUB, D_CONV), f32),
            pltpu.VMEM((SUB - 1, CONV_HALO + tq, D_CONV), f32),
            pltpu.VMEM((tq, D_CONV), bf16),
        ],
        compiler_params=pltpu.CompilerParams(
            dimension_semantics=("parallel", "arbitrary"), vmem_limit_bytes=VMEM_LIMIT),
        name="prompt_merge",
    )(mixa, u, u, gc, x, wo, dw, dwb, lng, lnb)


SCORE_UNROLL = 16


def _sample_score_kernel(pt_ref, qi_ref, w_ref, cki_ref, out_ref, buf_ref, sem_ref, *, npg):
    b = pl.program_id(0)
    nb = pl.num_programs(0)
    slot = b % 2

    def page_copy(bb, p, sl):
        return pltpu.make_async_copy(cki_ref.at[pt_ref[bb, p]], buf_ref.at[sl, p], sem_ref.at[sl])

    def start_all(bb, sl):
        def body(pp, carry):
            page_copy(bb, 2 * pp, sl).start(priority=0)
            page_copy(bb, 2 * pp + 1, sl).start(priority=1)
            return carry
        lax.fori_loop(0, npg // 2, body, 0)

    @pl.when(b == 0)
    def _():
        start_all(0, 0)

    @pl.when(b + 1 < nb)
    def _():
        start_all(b + 1, 1 - slot)

    def wait_body(p, carry):
        page_copy(b, p, slot).wait()
        return carry
    lax.fori_loop(0, npg, wait_body, 0)

    qs = (qi_ref[0] * (IDX_DIM ** -0.5)).astype(bf16)
    wcol = w_ref[0] * (N_IDX_HEADS ** -0.5)
    wcol2 = jnp.concatenate([wcol, wcol], axis=1)

    def body(i, carry):
        for u in range(0, SCORE_UNROLL, 2):
            p = i * SCORE_UNROLL + u
            pages = jnp.concatenate([buf_ref[slot, p], buf_ref[slot, p + 1]], axis=1).astype(bf16)
            s = jnp.dot(qs, pages, preferred_element_type=f32)
            r = jnp.sum(jnp.maximum(s, 0.0) * wcol2, axis=0, keepdims=True)
            out_ref[0, pl.ds(p, 1), :] = r[:, :LANES]
            out_ref[0, pl.ds(p + 1, 1), :] = r[:, LANES:]
        return carry
    lax.fori_loop(0, npg // SCORE_UNROLL, body, 0)


def _sample_scores(page_table, qi3, wrow, cki_t):
    Bd, npg = page_table.shape
    page = cki_t.shape[2]
    assert page == LANES and npg % SCORE_UNROLL == 0
    grid_spec = pltpu.PrefetchScalarGridSpec(
        num_scalar_prefetch=1,
        grid=(Bd,),
        in_specs=[
            pl.BlockSpec((1, N_IDX_HEADS, IDX_DIM), lambda b, pt: (b, 0, 0)),
            pl.BlockSpec((1, N_IDX_HEADS, LANES), lambda b, pt: (b, 0, 0)),
            pl.BlockSpec(memory_space=pl.ANY),
        ],
        out_specs=pl.BlockSpec((1, npg, page), lambda b, pt: (b, 0, 0)),
        scratch_shapes=[
            pltpu.VMEM((2, npg, IDX_DIM, page), f32),
            pltpu.SemaphoreType.DMA((2,)),
        ],
    )
    return pl.pallas_call(
        functools.partial(_sample_score_kernel, npg=npg),
        grid_spec=grid_spec,
        out_shape=jax.ShapeDtypeStruct((Bd, npg, page), f32),
        compiler_params=pltpu.CompilerParams(
            dimension_semantics=("arbitrary",), vmem_limit_bytes=VMEM_LIMIT),
        name="sample_scores",
    )(page_table, qi3, wrow, cki_t)


def _sample_select_kernel(sc_ref, qi_ref, kit_ref, kw_ref, mask_ref, mnew_ref, pany_ref, key_ref, *,
                          npg, ksel, past, idx_bits):
    Bd = sc_ref.shape[0]
    shape = (Bd, npg, LANES)

    qb = (qi_ref[...] * (IDX_DIM ** -0.5)).astype(bf16).astype(f32)
    kb = kit_ref[...].astype(bf16).astype(f32)
    prod = qb * kb
    grp = lax.broadcasted_iota(jnp.int32, prod.shape, 1) // IDX_DIM
    kwv = kw_ref[...]
    s_new = jnp.zeros((Bd, 1), f32)
    for h in range(N_IDX_HEADS):
        sh = jnp.sum(jnp.where(grp == h, prod, 0.0), axis=1, keepdims=True)
        s_new = s_new + jnp.maximum(sh, 0.0) * (kwv[:, IDX_DIM + h:IDX_DIM + h + 1] * (N_IDX_HEADS ** -0.5))
    key_new = _to_key(s_new).reshape(Bd, 1, 1)

    key_ref[...] = _to_key(sc_ref[...])
    pos = (lax.broadcasted_iota(jnp.int32, shape, 1) * LANES
           + lax.broadcasted_iota(jnp.int32, shape, 2))

    def count(pred, pred_new):
        x = jnp.where(pred, 1.0, 0.0)
        s = jnp.sum(jnp.sum(x, axis=1, keepdims=True), axis=2, keepdims=True)
        return s + jnp.where(pred_new, 1.0, 0.0)

    def bit_body(i, lo):
        cand = lo + jnp.left_shift(jnp.int32(1), 31 - i)
        tot = count(key_ref[...] >= cand, key_new >= cand)
        return jnp.where(tot >= ksel, cand, lo)

    thr = lax.fori_loop(0, 32, bit_body, jnp.full((Bd, 1, 1), INT_MIN, jnp.int32))
    keys = key_ref[...]
    need = ksel - count(keys > thr, key_new > thr)
    big = jnp.int32(2 ** 30)
    eq_pos = jnp.where(keys == thr, pos, big)
    eq_pos_new = jnp.where(key_new == thr, jnp.int32(past), big)

    def cut_body(i, x):
        cand = x + jnp.left_shift(jnp.int32(1), idx_bits - 1 - i)
        g = count(eq_pos < cand, eq_pos_new < cand)
        return jnp.where(g < need, cand, x)

    cut = lax.fori_loop(0, idx_bits, cut_body, jnp.zeros((Bd, 1, 1), jnp.int32))
    sel = jnp.where(keys > thr, 0, eq_pos) <= cut
    mask_ref[...] = jnp.where(sel, 0.0, NEG)
    pany_ref[...] = jnp.max(jnp.where(sel, 1.0, 0.0), axis=2, keepdims=True).astype(jnp.int32)
    sel_new = jnp.where(key_new > thr, 0, eq_pos_new) <= cut
    mnew_ref[...] = jnp.broadcast_to(jnp.where(sel_new, 0.0, NEG), mnew_ref.shape)


def _sample_select(scores, qi, ki_tiled, kw, *, past):
    Bd, npg, _ = scores.shape
    ksel = min(TOPK_MAX, (past + 1) // 4)
    vm = pl.BlockSpec(memory_space=pltpu.VMEM)
    return pl.pallas_call(
        functools.partial(_sample_select_kernel, npg=npg, ksel=ksel, past=past,
                          idx_bits=past.bit_length()),
        in_specs=[vm, vm, vm, vm],
        out_specs=(vm, vm, vm),
        out_shape=(jax.ShapeDtypeStruct((Bd, npg, LANES), f32),
                   jax.ShapeDtypeStruct((Bd, N_HEADS, LANES), f32),
                   jax.ShapeDtypeStruct((Bd, npg, 1), jnp.int32)),
        scratch_shapes=[pltpu.VMEM((Bd, npg, LANES), jnp.int32)],
        compiler_params=pltpu.CompilerParams(vmem_limit_bytes=VMEM_LIMIT),
        name="sample_select",
    )(scores, qi, ki_tiled, kw)


PAGE_BUFS = 32
PAGE_GROUP = 4


def _sample_attn_kernel(pt_ref, pany_ref, qbd_ref, knew_ref, vnew_ref, mask_ref, mnew_ref, rbt_ref, ck_ref, cv_ref,
                        out_ref, buf_ref, lg_ref, sem_ref, *, npg, past):
    b = pl.program_id(0)
    nb = pl.num_programs(0)
    per_b = 2 * npg
    total = nb * per_b

    def page_copy(g, src_ref):
        phys = pt_ref[g // per_b, g % npg]
        slot = g % PAGE_BUFS
        return pltpu.make_async_copy(src_ref.at[phys], buf_ref.at[slot], sem_ref.at[slot])

    def needed(g):
        return pany_ref[g // per_b, g % npg] != 0

    def start(g):
        is_k = g % per_b < npg

        @pl.when(jnp.logical_and(needed(g), is_k))
        def _():
            page_copy(g, ck_ref).start()

        @pl.when(jnp.logical_and(needed(g), jnp.logical_not(is_k)))
        def _():
            page_copy(g, cv_ref).start()

    def wait(g):
        @pl.when(needed(g))
        def _():
            page_copy(g, ck_ref).wait()

    @pl.when(b == 0)
    def _():
        for slot in range(PAGE_BUFS):
            buf_ref[slot] = jnp.zeros(buf_ref.shape[1:], f32)
        for g in range(PAGE_BUFS):
            start(g)

    g0 = b * per_b
    qbd = (qbd_ref[0] * (HEAD_DIM ** -0.5 * LOG2E)).astype(bf16)
    blk = (lax.broadcasted_iota(jnp.int32, (N_HEADS, D_ATTN), 0)
           == lax.broadcasted_iota(jnp.int32, (N_HEADS, D_ATTN), 1) // HEAD_DIM)
    tok = lax.broadcasted_iota(jnp.int32, (N_HEADS, LANES), 1)

    def bias_of(dist):
        far = rbt_ref[N_BUCKETS - 1]
        acc = jnp.broadcast_to(rbt_ref[0] - far, dist.shape)
        for i, t in enumerate(BUCKET_THRESH):
            acc = jnp.where(dist >= t, rbt_ref[i + 1] - far, acc)
        return acc * LOG2E

    def refill(gs):
        for g in gs:
            @pl.when(g + PAGE_BUFS < total)
            def _():
                start(g + PAGE_BUFS)

    def k_body(it, m):
        gs = [g0 + it * PAGE_GROUP + u for u in range(PAGE_GROUP)]
        for g in gs:
            wait(g)
        for u, g in enumerate(gs):
            p = it * PAGE_GROUP + u
            kp = buf_ref[g % PAGE_BUFS].astype(bf16)
            s = jnp.dot(qbd, kp, preferred_element_type=f32) + mask_ref[0, pl.ds(p, 1), :]
            lg_ref[p] = s
            m = jnp.maximum(m, s)
        refill(gs)
        return m

    m = lax.fori_loop(0, npg // PAGE_GROUP, k_body, jnp.full((N_HEADS, LANES), NEG, f32))
    last = lg_ref[npg - 1] + bias_of(past - ((npg - 1) * LANES + tok))
    lg_ref[npg - 1] = last
    m = jnp.maximum(m, last)

    s_new = (jnp.sum(qbd_ref[0] * knew_ref[0], axis=1, keepdims=True) * (HEAD_DIM ** -0.5 * LOG2E)
             + bias_of(jnp.zeros((N_HEADS, LANES), jnp.int32))[:, 0:1] + mnew_ref[0][:, 0:1])
    m_row = jnp.maximum(jnp.max(m, axis=1, keepdims=True), s_new)
    mb = jnp.broadcast_to(m_row, (N_HEADS, LANES))

    def v_body(it, carry):
        acc, l = carry
        gs = [g0 + npg + it * PAGE_GROUP + u for u in range(PAGE_GROUP)]
        for g in gs:
            wait(g)
        for u, g in enumerate(gs):
            vp = buf_ref[g % PAGE_BUFS].astype(bf16)
            pr = jnp.exp2(lg_ref[it * PAGE_GROUP + u] - mb)
            acc = acc + lax.dot_general(pr.astype(bf16), vp, (((1,), (1,)), ((), ())),
                                        preferred_element_type=f32)
            l = l + pr
        refill(gs)
        return acc, l

    acc, l = lax.fori_loop(0, npg // PAGE_GROUP, v_body, (jnp.zeros((N_HEADS, D_ATTN), f32),
                                                          jnp.zeros((N_HEADS, LANES), f32)))
    p_new = jnp.exp2(s_new - m_row)
    l_tot = jnp.sum(l, axis=1, keepdims=True) + p_new
    num = jnp.where(blk, acc + p_new * vnew_ref[0], 0.0) / l_tot
    out_ref[0] = jnp.sum(num, axis=0, keepdims=True)


def _sample_attention(page_table, pany, qbd, k_new_flat, v_new_flat, mask, mnew, rbt, ck_t, cv_t, *, past):
    Bd, npg = page_table.shape
    rows, page = ck_t.shape[1], ck_t.shape[2]
    assert rows == D_ATTN and page == LANES
    assert npg % PAGE_GROUP == 0 and PAGE_BUFS % PAGE_GROUP == 0 and 2 * npg >= PAGE_BUFS
    grid_spec = pltpu.PrefetchScalarGridSpec(
        num_scalar_prefetch=2,
        grid=(Bd,),
        in_specs=[
            pl.BlockSpec((1, N_HEADS, D_ATTN), lambda b, pt, pa: (b, 0, 0)),
            pl.BlockSpec((1, 1, D_ATTN), lambda b, pt, pa: (b, 0, 0)),
            pl.BlockSpec((1, 1, D_ATTN), lambda b, pt, pa: (b, 0, 0)),
            pl.BlockSpec((1, npg, LANES), lambda b, pt, pa: (b, 0, 0)),
            pl.BlockSpec((1, N_HEADS, LANES), lambda b, pt, pa: (b, 0, 0)),
            pl.BlockSpec((N_BUCKETS, N_HEADS, LANES), lambda b, pt, pa: (0, 0, 0)),
            pl.BlockSpec(memory_space=pl.ANY),
            pl.BlockSpec(memory_space=pl.ANY),
        ],
        out_specs=pl.BlockSpec((1, 1, D_ATTN), lambda b, pt, pa: (b, 0, 0)),
        scratch_shapes=[
            pltpu.VMEM((PAGE_BUFS, rows, page), f32),
            pltpu.VMEM((npg, N_HEADS, LANES), f32),
            pltpu.SemaphoreType.DMA((PAGE_BUFS,)),
        ],
    )
    return pl.pallas_call(
        functools.partial(_sample_attn_kernel, npg=npg, past=past),
        grid_spec=grid_spec,
        out_shape=jax.ShapeDtypeStruct((Bd, 1, D_ATTN), f32),
        compiler_params=pltpu.CompilerParams(
            dimension_semantics=("arbitrary",), vmem_limit_bytes=VMEM_LIMIT),
        name="sample_attention",
    )(page_table, pany, qbd, k_new_flat, v_new_flat, mask, mnew, rbt, ck_t, cv_t)


def _sample_merge_kernel(attn_ref, ga_ref, u_ref, gc_ref, st_ref, x_ref, wo_ref, dw_ref, dwb_ref,
                         lng_ref, lnb_ref, y_ref):
    acc = dwb_ref[...] + u_ref[...] * dw_ref[CONV_W - 1:CONV_W, :]
    for w in range(CONV_W - 1):
        acc = acc + st_ref[w] * dw_ref[w:w + 1, :]
    cg = (_layer_norm_swish(acc, lng_ref, lnb_ref) * gc_ref[...]).astype(bf16)
    mixa = (attn_ref[...] * ga_ref[...]).astype(bf16)
    y_ref[...] = (x_ref[...]
                  + jnp.dot(mixa, wo_ref[0:D_ATTN], preferred_element_type=f32)
                  + jnp.dot(cg, wo_ref[D_ATTN:D_ATTN + D_CONV], preferred_element_type=f32))


def _sample_merge(attn, ga, u, gc, state_t, x, wo, dw, dwb, lng, lnb):
    vm = pl.BlockSpec(memory_space=pltpu.VMEM)
    return pl.pallas_call(
        _sample_merge_kernel,
        in_specs=[vm] * 11,
        out_specs=vm,
        out_shape=jax.ShapeDtypeStruct(x.shape, f32),
        compiler_params=pltpu.CompilerParams(vmem_limit_bytes=VMEM_LIMIT),
        name="sample_merge",
    )(attn, ga, u, gc, state_t, x, wo, dw, dwb, lng, lnb)


def kernel(x_prompt, x_sample, cache_k, cache_v, cache_k_idx, state_conv, page_table, rel_bias, norm_g,
           w_in, q_norm_g, k_norm_g, dw_w, dw_b, ln_g, ln_b, w_out):
    depth = norm_g.shape[0]
    assert depth == 1, "single-layer step"
    B, S, D = x_prompt.shape
    Bd, T, _ = x_sample.shape
    assert T == 1
    n_phys, page = cache_k.shape[1], cache_k.shape[2]
    past = page_table.shape[1] * page

    w = w_in[0]
    c_ki = 2 * D_ATTN + 2 * N_HEADS * HEAD_DIM + N_IDX_HEADS * IDX_DIM
    c_conv = c_ki + IDX_DIM + N_IDX_HEADS
    pad = jnp.zeros((D, LANES - IDX_DIM - N_IDX_HEADS), w.dtype)
    w_all = jnp.concatenate([w[:, :c_ki], w[:, c_ki:c_conv], pad, w[:, c_conv:]], axis=1).astype(bf16)
    assert w_all.shape[1] == _C_END
    wo = w_out[0].astype(bf16)
    g = norm_g[0][None]
    qg = jnp.tile(q_norm_g[0], N_HEADS)[None]
    kg = jnp.tile(k_norm_g[0], N_HEADS)[None]
    hid = np.arange(D_ATTN) // HEAD_DIM
    bd = jnp.asarray((hid[:, None] == hid[None, :]) / HEAD_DIM, dtype=bf16)
    dwb, lng, lnb = dw_b[0][None], ln_g[0][None], ln_b[0][None]

    (qt, k_p, kb, v_p, vt, ga_p, qit, kw_p, wit, ki2, u_p, gc_p) = _inproj(
        x_prompt, g, w_all, qg, kg, bd, prompt=True)
    btab = _bias_tables(rel_bias)
    mixa = _prompt_attention(qt, qit, wit, kb, ki2, vt, ga_p, btab)
    dw8 = jnp.broadcast_to(dw_w[0][:, None, :], (CONV_W, SUB, D_CONV))
    y_prompt = _prompt_merge(mixa, u_p, gc_p, x_prompt, wo, dw8, dwb, lng, lnb)

    xs = x_sample.reshape(1, Bd, D)
    q_s, k_s, v_s, ga_s, qi_s, kw_s, u_s, gc_s = (a[0] for a in _inproj(
        xs, g, w_all, qg, kg, bd, prompt=False))
    ki_s = kw_s[:, :IDX_DIM]
    wrow = jnp.broadcast_to(kw_s[:, IDX_DIM:IDX_DIM + N_IDX_HEADS, None], (Bd, N_IDX_HEADS, LANES))
    cki_t = jnp.transpose(cache_k_idx[0], (0, 2, 1))
    ck_t = jnp.transpose(cache_k[0], (0, 2, 3, 1)).reshape(n_phys, D_ATTN, page)
    cv_t = jnp.transpose(cache_v[0], (0, 2, 3, 1)).reshape(n_phys, D_ATTN, page)
    scores = _sample_scores(page_table, qi_s.reshape(Bd, N_IDX_HEADS, IDX_DIM), wrow, cki_t)
    mask, mnew, pany = _sample_select(scores, qi_s, jnp.tile(ki_s, (1, N_IDX_HEADS)), kw_s, past=past)
    rbt = jnp.broadcast_to(rel_bias[:, :, None], (N_BUCKETS, N_HEADS, LANES))
    eye = jnp.asarray(np.arange(N_HEADS)[:, None] == hid[None, :], dtype=f32)
    qbd = q_s[:, None, :] * eye[None]
    attn_s = _sample_attention(page_table, pany[:, :, 0], qbd, k_s[:, None, :], v_s[:, None, :],
                               mask, mnew, rbt, ck_t, cv_t, past=past)
    state_t = jnp.transpose(state_conv[0], (1, 0, 2))
    y_sample = _sample_merge(attn_s[:, 0], ga_s, u_s, gc_s, state_t, x_sample[:, 0], wo,
                             dw_w[0], dwb, lng, lnb)

    heads = (N_HEADS, HEAD_DIM)
    return (
        y_prompt,
        y_sample[:, None],
        k_p.reshape((1, B, S) + heads),
        v_p.reshape((1, B, S) + heads),
        kw_p[None, :, :, :IDX_DIM],
        u_p[None, :, S - (CONV_W - 1):],
        k_s.reshape((1, Bd, 1) + heads),
        v_s.reshape((1, Bd, 1) + heads),
        ki_s.reshape(1, Bd, 1, IDX_DIM),
        jnp.concatenate([state_conv[0][:, 1:], u_s[:, None]], axis=1)[None],
    )
```

```python
import functools
import math

import numpy as np
import jax
import jax.numpy as jnp
from jax import lax
from jax.experimental import pallas as pl
from jax.experimental.pallas import tpu as pltpu

N_HEADS = 8
HEAD_DIM = 64
N_IDX_HEADS = 8
IDX_DIM = 64
D_ATTN = N_HEADS * HEAD_DIM
D_CONV = 512
CONV_W = 31
TOPK_MAX = 256
N_BUCKETS = 32
MAX_DISTANCE = 128
EPS = 1e-6
NEG = -1e30
INT_MIN = -2 ** 31
LOG2E = math.log2(math.e)

LANES = 128
Q_TILE = 256
K_CHUNK = 256
PV_ROWS = 144
VMEM_LIMIT = 56 * 1024 * 1024

f32 = jnp.float32
bf16 = jnp.bfloat16


def _bucket_thresholds():
    n = np.arange(0, 4 * MAX_DISTANCE)
    max_exact = N_BUCKETS // 2
    nf = np.maximum(n, 1).astype(np.float32)
    large = max_exact + (np.log(nf / np.float32(max_exact)) / np.float32(math.log(MAX_DISTANCE / max_exact))
                         * np.float32(N_BUCKETS - max_exact)).astype(np.int32)
    bucket = np.where(n < max_exact, n, np.minimum(large, N_BUCKETS - 1))
    return [int(n[bucket >= i].min()) for i in range(1, N_BUCKETS)]


BUCKET_THRESH = _bucket_thresholds()
assert BUCKET_THRESH[-1] <= K_CHUNK + 1 and K_CHUNK % Q_TILE == 0
N_PAR = K_CHUNK // Q_TILE


def _silu(x):
    return x * jax.nn.sigmoid(x)


def _to_key(x):
    b = pltpu.bitcast(x, jnp.int32)
    return jnp.where(b < 0, b ^ jnp.int32(0x7FFFFFFF), b)


_C_Q, _C_K, _C_V, _C_GA, _C_QI, _C_KW, _C_A, _C_B, _C_GC, _C_END = (
    0, 512, 1024, 1536, 2048, 2560, 2688, 3200, 3712, 4224)


def _inproj_kernel(x_ref, g_ref, wa_ref, wb_ref, wc_ref, qg_ref, kg_ref, bd_ref, *outs, prompt, tm):
    xf = x_ref[0]
    ms = jnp.mean(xf * xf, axis=-1, keepdims=True)
    xn = (xf * lax.rsqrt(ms + EPS) * g_ref[...]).astype(bf16)

    def proj(a, b):
        if b <= _C_KW:
            w = wa_ref[:, a:b]
        elif b <= _C_A:
            w = wb_ref[:, a - _C_KW:b - _C_KW]
        else:
            w = wc_ref[:, a - _C_A:b - _C_A]
        return jnp.dot(xn, w, preferred_element_type=f32)

    def head_norm(z, gain_ref):
        msq = jnp.dot((z * z).astype(bf16), bd_ref[...], preferred_element_type=f32)
        return z * lax.rsqrt(msq + EPS) * gain_ref[...]

    q = head_norm(proj(_C_Q, _C_K), qg_ref)
    k = head_norm(proj(_C_K, _C_V), kg_ref)
    v = proj(_C_V, _C_GA)
    ga = _silu(proj(_C_GA, _C_QI))
    qi = proj(_C_QI, _C_KW)
    kw = proj(_C_KW, _C_A)
    u = proj(_C_A, _C_B) * jax.nn.sigmoid(proj(_C_B, _C_GC))
    gc = _silu(proj(_C_GC, _C_END))

    if prompt:
        q_o, k_o, kb_o, v_o, vt_o, ga_o, qi_o, kw_o, wi_o, ki2_o, u_o, gc_o = outs
        q_o[0] = (q * (HEAD_DIM ** -0.5 * LOG2E)).T.astype(bf16)
        qi_o[0] = (qi * (IDX_DIM ** -0.5)).T.astype(bf16)
        wi_o[0] = kw.T[IDX_DIM:IDX_DIM + N_IDX_HEADS, :]
        kb_o[0] = k.astype(bf16)
        lane = lax.broadcasted_iota(jnp.int32, kw.shape, 1)
        ki2_o[0] = jnp.where(lane < IDX_DIM, kw, pltpu.roll(kw, IDX_DIM, 1)).astype(bf16)
        vt = v.T.astype(bf16)
        tail = jnp.where(lax.broadcasted_iota(jnp.int32, (PV_ROWS - 2 * HEAD_DIM, K_CHUNK), 0) < 8,
                         1.0, 0.0).astype(bf16)
        for c in range(tm // K_CHUNK):
            for i in range(N_HEADS // 2):
                blk = vt[i * 2 * HEAD_DIM:(i + 1) * 2 * HEAD_DIM, c * K_CHUNK:(c + 1) * K_CHUNK]
                vt_o[0, c, i] = jnp.concatenate([blk, tail], axis=0)
    else:
        q_o, k_o, v_o, ga_o, qi_o, kw_o, u_o, gc_o = outs
        q_o[0] = q
        qi_o[0] = qi
    k_o[0] = k
    v_o[0] = v
    ga_o[0] = ga
    kw_o[0] = kw
    u_o[0] = u
    gc_o[0] = gc


def _inproj(x, norm_g, w_groups, qg, kg, bd, *, prompt):
    B, S, D = x.shape
    tm = 512 if prompt else S
    assert S % tm == 0 and (not prompt or tm % K_CHUNK == 0)
    grid = (B, S // tm)
    row = lambda b, i: (b, i, 0)
    const2 = lambda b, i: (0, 0)
    col = lambda b, i: (b, 0, i)
    nck = tm // K_CHUNK
    if prompt:
        out_shape = (
            jax.ShapeDtypeStruct((B, D_ATTN, S), bf16),
            jax.ShapeDtypeStruct((B, S, D_ATTN), f32),
            jax.ShapeDtypeStruct((B, S, D_ATTN), bf16),
            jax.ShapeDtypeStruct((B, S, D_ATTN), f32),
            jax.ShapeDtypeStruct((B, S // K_CHUNK, N_HEADS // 2, PV_ROWS, K_CHUNK), bf16),
            jax.ShapeDtypeStruct((B, S, D_ATTN), f32),
            jax.ShapeDtypeStruct((B, N_IDX_HEADS * IDX_DIM, S), bf16),
            jax.ShapeDtypeStruct((B, S, LANES), f32),
            jax.ShapeDtypeStruct((B, N_IDX_HEADS, S), f32),
            jax.ShapeDtypeStruct((B, S, LANES), bf16),
            jax.ShapeDtypeStruct((B, S, D_CONV), f32),
            jax.ShapeDtypeStruct((B, S, D_CONV), f32),
        )
        out_specs = (
            pl.BlockSpec((1, D_ATTN, tm), col),
            pl.BlockSpec((1, tm, D_ATTN), row),
            pl.BlockSpec((1, tm, D_ATTN), row),
            pl.BlockSpec((1, tm, D_ATTN), row),
            pl.BlockSpec((1, nck, N_HEADS // 2, PV_ROWS, K_CHUNK), lambda b, i: (b, i, 0, 0, 0)),
            pl.BlockSpec((1, tm, D_ATTN), row),
            pl.BlockSpec((1, N_IDX_HEADS * IDX_DIM, tm), col),
            pl.BlockSpec((1, tm, LANES), row),
            pl.BlockSpec((1, N_IDX_HEADS, tm), col),
            pl.BlockSpec((1, tm, LANES), row),
            pl.BlockSpec((1, tm, D_CONV), row),
            pl.BlockSpec((1, tm, D_CONV), row),
        )
    else:
        out_shape = (
            jax.ShapeDtypeStruct((B, S, D_ATTN), f32),
            jax.ShapeDtypeStruct((B, S, D_ATTN), f32),
            jax.ShapeDtypeStruct((B, S, D_ATTN), f32),
            jax.ShapeDtypeStruct((B, S, D_ATTN), f32),
            jax.ShapeDtypeStruct((B, S, D_ATTN), f32),
            jax.ShapeDtypeStruct((B, S, LANES), f32),
            jax.ShapeDtypeStruct((B, S, D_CONV), f32),
            jax.ShapeDtypeStruct((B, S, D_CONV), f32),
        )
        out_specs = tuple(pl.BlockSpec((1, tm, s.shape[-1]), row) for s in out_shape)
    return pl.pallas_call(
        functools.partial(_inproj_kernel, prompt=prompt, tm=tm),
        grid=grid,
        in_specs=[
            pl.BlockSpec((1, tm, D), row),
            pl.BlockSpec((1, D), const2),
            pl.BlockSpec((D, _C_KW), const2),
            pl.BlockSpec((D, _C_A - _C_KW), const2),
            pl.BlockSpec((D, _C_END - _C_A), const2),
            pl.BlockSpec((1, D_ATTN), const2),
            pl.BlockSpec((1, D_ATTN), const2),
            pl.BlockSpec((D_ATTN, D_ATTN), const2),
        ],
        out_specs=out_specs,
        out_shape=out_shape,
        compiler_params=pltpu.CompilerParams(
            dimension_semantics=("parallel", "parallel"), vmem_limit_bytes=VMEM_LIMIT),
        name="inproj_prompt" if prompt else "inproj_sample",
    )(x, norm_g, *w_groups, qg, kg, bd)


def _bias_table_kernel(rb_ref, out_ref):
    shape = (K_CHUNK, Q_TILE)
    krow = lax.broadcasted_iota(jnp.int32, shape, 0)
    qcol = lax.broadcasted_iota(jnp.int32, shape, 1)
    for par in range(N_PAR):
        for which in range(2):
            dist = Q_TILE * par + K_CHUNK * (1 - which) + qcol - krow
            for h in range(N_HEADS):
                far = rb_ref[N_BUCKETS - 1, h]
                acc = jnp.full(shape, (rb_ref[0, h] - far) * LOG2E, f32)
                for i, t in enumerate(BUCKET_THRESH):
                    acc = jnp.where(dist >= t, (rb_ref[i + 1, h] - far) * LOG2E, acc)
                out_ref[par, which, h // 2, :, (h % 2) * Q_TILE:(h % 2 + 1) * Q_TILE] = (
                    jnp.where(dist >= 0, acc, NEG))


def _bias_tables(rel_bias):
    return pl.pallas_call(
        _bias_table_kernel,
        in_specs=[pl.BlockSpec(memory_space=pltpu.SMEM)],
        out_specs=pl.BlockSpec(memory_space=pltpu.VMEM),
        out_shape=jax.ShapeDtypeStruct((N_PAR, 2, N_HEADS // 2, K_CHUNK, 2 * Q_TILE), f32),
        name="bias_tables",
    )(rel_bias)


N_PAIRS = N_HEADS // 2
SUB = 8


COUNT_ROWS = 32
MAX_ROWS = 16


def _bit_planes(words):
    a = list(words)
    j, m = 16, 0x0000FFFF
    while j:
        k = 0
        while k < 32:
            t = (a[k] ^ lax.shift_right_logical(a[k + j], jnp.int32(j))) & jnp.int32(m)
            a[k] = a[k] ^ t
            a[k + j] = a[k + j] ^ jnp.left_shift(t, jnp.int32(j))
            k = (k + j + 1) & ~j
        j >>= 1
        m = (m ^ (m << j)) & 0xFFFFFFFF
    return a


def _fold(x, op, rows):
    return op(x.reshape(x.shape[0] // rows, rows, x.shape[1]), axis=0)


def _attn_kernel(qt_ref, qit_ref, wit_ref, k_ref, ki_ref, vt_ref, ga_ref, bt_ref, out_ref,
                 sc_ref, key_ref, pl_ref, lg_ref, qbd_ref, qibd_ref, o_ref, *, ksel, idx_bits):
    j = pl.program_id(1)
    n_chunks = sc_ref.shape[0]
    nck = j // N_PAR + 1
    par = j % N_PAR
    shape = (K_CHUNK, Q_TILE)
    key_pos = lax.broadcasted_iota(jnp.int32, shape, 0)
    q_pos = j * Q_TILE + lax.broadcasted_iota(jnp.int32, shape, 1)

    top = lax.broadcasted_iota(jnp.int32, (2 * HEAD_DIM, Q_TILE), 0) < HEAD_DIM
    for i in range(N_PAIRS):
        rows = slice(i * 2 * HEAD_DIM, (i + 1) * 2 * HEAD_DIM)
        for src, dst in ((qt_ref, qbd_ref), (qit_ref, qibd_ref)):
            x = src[0, rows, :]
            zero = jnp.zeros_like(x)
            dst[i] = jnp.concatenate([jnp.where(top, x, zero), jnp.where(top, zero, x)], axis=1)
    wi = wit_ref[0] * (N_IDX_HEADS ** -0.5)
    wpair = [jnp.concatenate([wi[2 * i:2 * i + 1], wi[2 * i + 1:2 * i + 2]], axis=1)
             for i in range(N_PAIRS)]

    def chunk_rows(c):
        return pl.ds(pl.multiple_of(c * K_CHUNK, K_CHUNK), K_CHUNK)

    def for_chunk_pairs(n, body):
        def pair(cc, carry):
            body([2 * cc, 2 * cc + 1])
            return carry
        lax.fori_loop(0, n // 2, pair, 0)

        @pl.when(n % 2 == 1)
        def _():
            body([n - 1])

    def score_chunks(chunks):
        for c in chunks:
            ki2 = ki_ref[0, chunk_rows(c), :]
            acc = jnp.zeros(shape, f32)
            for i in range(N_PAIRS):
                r = jnp.maximum(jnp.dot(ki2, qibd_ref[i], preferred_element_type=f32), 0.0) * wpair[i]
                acc = acc + r[:, :Q_TILE] + r[:, Q_TILE:]
            sc_ref[c] = jnp.where(c * K_CHUNK + key_pos <= q_pos, acc, -jnp.inf)

    for_chunk_pairs(nck, score_chunks)

    def count(pred):
        def body(c, cnt):
            hit = pred(c)
            for g in range(K_CHUNK // COUNT_ROWS):
                rows = slice(g * COUNT_ROWS, (g + 1) * COUNT_ROWS)
                cnt = jnp.where(hit[rows], cnt + 1.0, cnt)
            return cnt
        cnt = lax.fori_loop(0, nck, body, jnp.zeros((COUNT_ROWS, Q_TILE), f32))
        return jnp.sum(cnt, axis=0, keepdims=True)

    def write_mask(pred):
        def body(c, carry):
            sc_ref[c] = jnp.where(pred(c), 0.0, NEG)
            return carry
        lax.fori_loop(0, nck, body, 0)

    need_search = (j + 1) * Q_TILE > ksel

    @pl.when(jnp.logical_not(need_search))
    def _():
        write_mask(lambda c: c * K_CHUNK + key_pos < (j + 1) * Q_TILE)

    @pl.when(need_search)
    def _():
        def key_body(c, carry):
            keys = _to_key(sc_ref[c])
            key_ref[c] = keys
            ukeys = keys ^ jnp.int32(INT_MIN)
            planes = _bit_planes([ukeys[t * SUB:(t + 1) * SUB, :] for t in range(32)])
            for p in range(32):
                pl_ref[c, p] = planes[p]
            return carry
        lax.fori_loop(0, nck, key_body, 0)

        def clear_body(c, carry):
            pl_ref[c] = jnp.zeros(pl_ref.shape[1:], jnp.int32)
            return carry
        lax.fori_loop(nck, n_chunks, clear_body, 0)

        def bit_body(p, carry):
            n_above, thr_u, alive = carry
            ones = [alive[c] & pl_ref[c, p] for c in range(n_chunks)]
            acc = lax.population_count(ones[0])
            for c in range(1, n_chunks):
                acc = acc + lax.population_count(ones[c])
            n_hi = n_above + jnp.sum(acc.astype(f32), axis=0, keepdims=True)
            take = n_hi >= ksel
            takeb = jnp.broadcast_to(take, (SUB, Q_TILE))
            alive = tuple(jnp.where(takeb, ones[c], alive[c] ^ ones[c]) for c in range(n_chunks))
            bit = jnp.left_shift(jnp.int32(1), 31 - p)
            return jnp.where(take, n_above, n_hi), jnp.where(take, thr_u | bit, thr_u), alive

        everything = jnp.full((SUB, Q_TILE), -1, jnp.int32)
        n_above, thr_u, alive = lax.fori_loop(
            0, 32, bit_body,
            (jnp.zeros((1, Q_TILE), f32), jnp.zeros((1, Q_TILE), jnp.int32),
             tuple(jnp.where(c < nck, everything, 0) for c in range(n_chunks))))
        n_eq = lax.population_count(alive[0])
        for c in range(1, n_chunks):
            n_eq = n_eq + lax.population_count(alive[c])
        n_ge = n_above + jnp.sum(n_eq.astype(f32), axis=0, keepdims=True)
        thr = thr_u ^ jnp.int32(INT_MIN)
        thrb = jnp.broadcast_to(thr, shape)
        has_tie = jnp.max(n_ge) > ksel

        @pl.when(jnp.logical_not(has_tie))
        def _():
            write_mask(lambda c: key_ref[c] >= thrb)

        @pl.when(has_tie)
        def _():
            need = ksel - count(lambda c: key_ref[c] > thrb)

            def cut_body(i, x):
                cand = x + jnp.left_shift(jnp.int32(1), idx_bits - 1 - i)
                candb = jnp.broadcast_to(cand, shape)
                g = count(lambda c: jnp.where(key_ref[c] == thrb, c * K_CHUNK + key_pos, 2 ** 30) < candb)
                return jnp.where(g < need, cand, x)

            cut = lax.fori_loop(0, idx_bits, cut_body, jnp.zeros((1, Q_TILE), jnp.int32))
            cutb = jnp.broadcast_to(cut, shape)

            def sel(c):
                kc = key_ref[c]
                eq_idx = jnp.where(kc == thrb, c * K_CHUNK + key_pos, 2 ** 30)
                return jnp.where(kc > thrb, 0, eq_idx) <= cutb
            write_mask(sel)

    def logits_chunk(c, which, ms):
        mask = sc_ref[c]
        mask2 = jnp.concatenate([mask, mask], axis=1)
        out = []
        for i in range(N_PAIRS):
            kc = k_ref[0, chunk_rows(c), i * 2 * HEAD_DIM:(i + 1) * 2 * HEAD_DIM]
            s = jnp.dot(kc, qbd_ref[i], preferred_element_type=f32) + mask2
            if which is not None:
                s = s + bt_ref[par, which, i]
            lg_ref[i, c] = s
            out.append(jnp.maximum(ms[i], _fold(s, jnp.max, MAX_ROWS)))
        return tuple(out)

    ms = tuple(jnp.full((MAX_ROWS, 2 * Q_TILE), NEG, f32) for _ in range(N_PAIRS))
    nfar = jnp.maximum(nck - 2, 0)
    ms = lax.fori_loop(
        0, nfar // 2, lambda cc, ms: logits_chunk(2 * cc + 1, None, logits_chunk(2 * cc, None, ms)), ms)
    ms = lax.cond(nfar % 2 == 1, lambda ms: logits_chunk(nfar - 1, None, ms), lambda ms: ms, ms)
    ms = lax.cond(nck >= 2,
                  lambda ms: logits_chunk(nck - 1, 1, logits_chunk(nck - 2, 0, ms)),
                  lambda ms: logits_chunk(nck - 1, 1, ms), ms)
    mrow = [jnp.max(m, axis=0, keepdims=True) for m in ms]

    o_ref[...] = jnp.zeros(o_ref.shape, f32)

    def pv_chunks(chunks):
        for i in range(N_PAIRS):
            acc = None
            for c in chunks:
                p = jnp.exp2((lg_ref[i, c] - mrow[i]).astype(bf16))
                d = jnp.dot(vt_ref[0, c, i], p, preferred_element_type=f32)
                acc = d if acc is None else acc + d
            o_ref[i] += acc
    for_chunk_pairs(nck, pv_chunks)

    pieces = []
    for i in range(N_PAIRS):
        o = o_ref[i]
        l = o[2 * HEAD_DIM:2 * HEAD_DIM + 1, :]
        pieces.append(o[0:HEAD_DIM, 0:Q_TILE] / l[:, 0:Q_TILE])
        pieces.append(o[HEAD_DIM:2 * HEAD_DIM, Q_TILE:] / l[:, Q_TILE:])
    out_ref[0] = (jnp.concatenate(pieces, axis=0).T * ga_ref[0]).astype(bf16)


def _prompt_attention(qt, qit, wit, kb, ki2, vt, ga, btab):
    B, S, _ = kb.shape
    assert S % K_CHUNK == 0
    nq = S // Q_TILE
    nc = S // K_CHUNK
    ksel = min(TOPK_MAX, S // 4)
    qcol = lambda b, j: (b, 0, j)
    perb3 = lambda b, j: (b, 0, 0)
    return pl.pallas_call(
        functools.partial(_attn_kernel, ksel=ksel, idx_bits=max(1, (S - 1).bit_length())),
        grid=(B, nq),
        in_specs=[
            pl.BlockSpec((1, D_ATTN, Q_TILE), qcol),
            pl.BlockSpec((1, N_IDX_HEADS * IDX_DIM, Q_TILE), qcol),
            pl.BlockSpec((1, N_IDX_HEADS, Q_TILE), qcol),
            pl.BlockSpec((1, S, D_ATTN), perb3),
            pl.BlockSpec((1, S, LANES), perb3),
            pl.BlockSpec((1, nc, N_PAIRS, PV_ROWS, K_CHUNK), lambda b, j: (b, 0, 0, 0, 0)),
            pl.BlockSpec((1, Q_TILE, D_ATTN), lambda b, j: (b, j, 0)),
            pl.BlockSpec((N_PAR, 2, N_PAIRS, K_CHUNK, 2 * Q_TILE), lambda b, j: (0, 0, 0, 0, 0)),
        ],
        out_specs=pl.BlockSpec((1, Q_TILE, D_ATTN), lambda b, j: (b, j, 0)),
        out_shape=jax.ShapeDtypeStruct((B, S, D_ATTN), bf16),
        scratch_shapes=[
            pltpu.VMEM((nc, K_CHUNK, Q_TILE), f32),
            pltpu.VMEM((nc, K_CHUNK, Q_TILE), jnp.int32),
            pltpu.VMEM((nc, 32, SUB, Q_TILE), jnp.int32),
            pltpu.VMEM((N_PAIRS, nc, K_CHUNK, 2 * Q_TILE), f32),
            pltpu.VMEM((N_PAIRS, 2 * HEAD_DIM, 2 * Q_TILE), bf16),
            pltpu.VMEM((N_PAIRS, 2 * IDX_DIM, 2 * Q_TILE), bf16),
            pltpu.VMEM((N_PAIRS, PV_ROWS, 2 * Q_TILE), f32),
        ],
        compiler_params=pltpu.CompilerParams(
            dimension_semantics=("parallel", "arbitrary"), vmem_limit_bytes=VMEM_LIMIT),
        name="prompt_attention",
    )(qt, qit, wit, kb, ki2, vt, ga, btab)


CONV_HALO = 32
CONV_ROWS = 32


def _layer_norm_swish(c, lng_ref, lnb_ref):
    mu = jnp.mean(c, axis=-1, keepdims=True)
    d = c - mu
    var = jnp.mean(d * d, axis=-1, keepdims=True)
    return _silu(d * lax.rsqrt(var + EPS) * lng_ref[...] + lnb_ref[...])


def _merge_kernel(mixa_ref, ucur_ref, uprev_ref, gc_ref, x_ref, wo_ref, dw_ref, dwb_ref, lng_ref,
                  lnb_ref, y_ref, ext_ref, sh_ref, cg_ref, *, tq):
    i = pl.program_id(1)
    n = CONV_HALO + tq
    ext_ref[0:CONV_HALO] = jnp.where(i > 0, uprev_ref[0], 0.0)
    ext_ref[CONV_HALO:n] = ucur_ref[0]
    ext_ref[n:n + SUB] = jnp.zeros((SUB, D_CONV), f32)
    ext = ext_ref[...]
    for r in range(1, SUB):
        sh_ref[r - 1] = pltpu.roll(ext, n + SUB - r, 0)[0:n, :]
    first = CONV_HALO - (CONV_W - 1)
    for c in range(tq // CONV_ROWS):
        acc = jnp.broadcast_to(dwb_ref[...], (CONV_ROWS, D_CONV))
        for w in range(CONV_W):
            a, r = divmod(first + w, SUB)
            start = c * CONV_ROWS + a * SUB
            src = ext_ref if r == 0 else sh_ref.at[r - 1]
            acc = acc + src[start:start + CONV_ROWS, :] * jnp.tile(dw_ref[w], (CONV_ROWS // SUB, 1))
        rows = slice(c * CONV_ROWS, (c + 1) * CONV_ROWS)
        cg_ref[rows] = (_layer_norm_swish(acc, lng_ref, lnb_ref) * gc_ref[0, rows]).astype(bf16)
    y_ref[0] = (x_ref[0]
                + jnp.dot(mixa_ref[0], wo_ref[0:D_ATTN], preferred_element_type=f32)
                + jnp.dot(cg_ref[...], wo_ref[D_ATTN:D_ATTN + D_CONV], preferred_element_type=f32))


def _prompt_merge(mixa, u, gc, x, wo, dw, dwb, lng, lnb):
    B, S, D = x.shape
    tq = 256
    assert S % tq == 0 and tq % CONV_HALO == 0
    row = lambda b, i: (b, i, 0)
    const2 = lambda b, i: (0, 0)
    halo = lambda b, i: (b, jnp.maximum(i * (tq // CONV_HALO) - 1, 0), 0)
    return pl.pallas_call(
        functools.partial(_merge_kernel, tq=tq),
        grid=(B, S // tq),
        in_specs=[
            pl.BlockSpec((1, tq, D_ATTN), row),
            pl.BlockSpec((1, tq, D_CONV), row),
            pl.BlockSpec((1, CONV_HALO, D_CONV), halo),
            pl.BlockSpec((1, tq, D_CONV), row),
            pl.BlockSpec((1, tq, D), row),
            pl.BlockSpec((D_ATTN + D_CONV, D), const2),
            pl.BlockSpec((CONV_W, SUB, D_CONV), lambda b, i: (0, 0, 0)),
            pl.BlockSpec((1, D_CONV), const2),
            pl.BlockSpec((1, D_CONV), const2),
            pl.BlockSpec((1, D_CONV), const2),
        ],
        out_specs=pl.BlockSpec((1, tq, D), row),
        out_shape=jax.ShapeDtypeStruct((B, S, D), f32),
        scratch_shapes=[
            pltpu.VMEM((CONV_HALO + tq + SUB, D_CONV), f32),
            pltpu.VMEM((SUB - 1, CONV_HALO + tq, D_CONV), f32),
            pltpu.VMEM((tq, D_CONV), bf16),
        ],
        compiler_params=pltpu.CompilerParams(
            dimension_semantics=("parallel", "arbitrary"), vmem_limit_bytes=VMEM_LIMIT),
        name="prompt_merge",
    )(mixa, u, u, gc, x, wo, dw, dwb, lng, lnb)


SCORE_UNROLL = 16


def _sample_score_kernel(pt_ref, qi_ref, w_ref, cki_ref, out_ref, buf_ref, sem_ref, *, npg):
    b = pl.program_id(0)
    nb = pl.num_programs(0)
    slot = b % 2

    def page_copy(bb, p, sl):
        return pltpu.make_async_copy(cki_ref.at[pt_ref[bb, p]], buf_ref.at[sl, p],
                                     sem_ref.at[sl, p // SCORE_UNROLL])

    def start_all(bb, sl):
        def body(pp, carry):
            page_copy(bb, 2 * pp, sl).start(priority=0)
            page_copy(bb, 2 * pp + 1, sl).start(priority=1)
            return carry
        lax.fori_loop(0, npg // 2, body, 0)

    @pl.when(b == 0)
    def _():
        start_all(0, 0)

    @pl.when(b + 1 < nb)
    def _():
        start_all(b + 1, 1 - slot)

    qs = (qi_ref[0] * (IDX_DIM ** -0.5)).astype(bf16)
    wcol = w_ref[0] * (N_IDX_HEADS ** -0.5)
    wcol2 = jnp.concatenate([wcol, wcol], axis=1)

    def body(i, carry):
        for u in range(SCORE_UNROLL):
            page_copy(b, i * SCORE_UNROLL + u, slot).wait()
        for u in range(0, SCORE_UNROLL, 2):
            p = i * SCORE_UNROLL + u
            pages = jnp.concatenate([buf_ref[slot, p], buf_ref[slot, p + 1]], axis=1).astype(bf16)
            s = jnp.dot(qs, pages, preferred_element_type=f32)
            r = jnp.sum(jnp.maximum(s, 0.0) * wcol2, axis=0, keepdims=True)
            out_ref[0, pl.ds(p, 1), :] = r[:, :LANES]
            out_ref[0, pl.ds(p + 1, 1), :] = r[:, LANES:]
        return carry
    lax.fori_loop(0, npg // SCORE_UNROLL, body, 0)


def _sample_scores(page_table, qi3, wrow, cki_t):
    Bd, npg = page_table.shape
    page = cki_t.shape[2]
    assert page == LANES and npg % SCORE_UNROLL == 0
    grid_spec = pltpu.PrefetchScalarGridSpec(
        num_scalar_prefetch=1,
        grid=(Bd,),
        in_specs=[
            pl.BlockSpec((1, N_IDX_HEADS, IDX_DIM), lambda b, pt: (b, 0, 0)),
            pl.BlockSpec((1, N_IDX_HEADS, LANES), lambda b, pt: (b, 0, 0)),
            pl.BlockSpec(memory_space=pl.ANY),
        ],
        out_specs=pl.BlockSpec((1, npg, page), lambda b, pt: (b, 0, 0)),
        scratch_shapes=[
            pltpu.VMEM((2, npg, IDX_DIM, page), f32),
            pltpu.SemaphoreType.DMA((2, npg // SCORE_UNROLL)),
        ],
    )
    return pl.pallas_call(
        functools.partial(_sample_score_kernel, npg=npg),
        grid_spec=grid_spec,
        out_shape=jax.ShapeDtypeStruct((Bd, npg, page), f32),
        compiler_params=pltpu.CompilerParams(
            dimension_semantics=("arbitrary",), vmem_limit_bytes=VMEM_LIMIT),
        name="sample_scores",
    )(page_table, qi3, wrow, cki_t)


def _sample_select_kernel(sc_ref, qi_ref, kit_ref, kw_ref, mask_ref, mnew_ref, pany_ref, key_ref, *,
                          npg, ksel, past, idx_bits):
    Bd = sc_ref.shape[0]
    shape = (Bd, npg, LANES)

    qb = (qi_ref[...] * (IDX_DIM ** -0.5)).astype(bf16).astype(f32)
    kb = kit_ref[...].astype(bf16).astype(f32)
    prod = qb * kb
    grp = lax.broadcasted_iota(jnp.int32, prod.shape, 1) // IDX_DIM
    kwv = kw_ref[...]
    s_new = jnp.zeros((Bd, 1), f32)
    for h in range(N_IDX_HEADS):
        sh = jnp.sum(jnp.where(grp == h, prod, 0.0), axis=1, keepdims=True)
        s_new = s_new + jnp.maximum(sh, 0.0) * (kwv[:, IDX_DIM + h:IDX_DIM + h + 1] * (N_IDX_HEADS ** -0.5))
    key_new = _to_key(s_new).reshape(Bd, 1, 1)

    key_ref[...] = _to_key(sc_ref[...])
    pos = (lax.broadcasted_iota(jnp.int32, shape, 1) * LANES
           + lax.broadcasted_iota(jnp.int32, shape, 2))

    def count(pred, pred_new):
        x = jnp.where(pred, 1.0, 0.0)
        s = jnp.sum(jnp.sum(x, axis=1, keepdims=True), axis=2, keepdims=True)
        return s + jnp.where(pred_new, 1.0, 0.0)

    def bit_body(i, lo):
        cand = lo + jnp.left_shift(jnp.int32(1), 31 - i)
        tot = count(key_ref[...] >= cand, key_new >= cand)
        return jnp.where(tot >= ksel, cand, lo)

    thr = lax.fori_loop(0, 32, bit_body, jnp.full((Bd, 1, 1), INT_MIN, jnp.int32))
    keys = key_ref[...]
    need = ksel - count(keys > thr, key_new > thr)
    big = jnp.int32(2 ** 30)
    eq_pos = jnp.where(keys == thr, pos, big)
    eq_pos_new = jnp.where(key_new == thr, jnp.int32(past), big)

    def cut_body(i, x):
        cand = x + jnp.left_shift(jnp.int32(1), idx_bits - 1 - i)
        g = count(eq_pos < cand, eq_pos_new < cand)
        return jnp.where(g < need, cand, x)

    cut = lax.fori_loop(0, idx_bits, cut_body, jnp.zeros((Bd, 1, 1), jnp.int32))
    sel = jnp.where(keys > thr, 0, eq_pos) <= cut
    mask_ref[...] = jnp.where(sel, 0.0, NEG)
    pany_ref[...] = jnp.max(jnp.where(sel, 1.0, 0.0), axis=2, keepdims=True).astype(jnp.int32)
    sel_new = jnp.where(key_new > thr, 0, eq_pos_new) <= cut
    mnew_ref[...] = jnp.broadcast_to(jnp.where(sel_new, 0.0, NEG), mnew_ref.shape)


def _sample_select(scores, qi, ki_tiled, kw, *, past):
    Bd, npg, _ = scores.shape
    ksel = min(TOPK_MAX, (past + 1) // 4)
    vm = pl.BlockSpec(memory_space=pltpu.VMEM)
    return pl.pallas_call(
        functools.partial(_sample_select_kernel, npg=npg, ksel=ksel, past=past,
                          idx_bits=past.bit_length()),
        in_specs=[vm, vm, vm, vm],
        out_specs=(vm, vm, vm),
        out_shape=(jax.ShapeDtypeStruct((Bd, npg, LANES), f32),
                   jax.ShapeDtypeStruct((Bd, N_HEADS, LANES), f32),
                   jax.ShapeDtypeStruct((Bd, npg, 1), jnp.int32)),
        scratch_shapes=[pltpu.VMEM((Bd, npg, LANES), jnp.int32)],
        compiler_params=pltpu.CompilerParams(vmem_limit_bytes=VMEM_LIMIT),
        name="sample_select",
    )(scores, qi, ki_tiled, kw)


PAGE_BUFS = 32
PAGE_GROUP = 4


def _sample_attn_kernel(pt_ref, pany_ref, qbd_ref, knew_ref, vnew_ref, mask_ref, mnew_ref, rbt_ref, ck_ref, cv_ref,
                        out_ref, buf_ref, lg_ref, sem_ref, *, npg, past):
    b = pl.program_id(0)
    nb = pl.num_programs(0)
    per_b = 2 * npg
    total = nb * per_b

    def page_copy(g, src_ref):
        phys = pt_ref[g // per_b, g % npg]
        slot = g % PAGE_BUFS
        return pltpu.make_async_copy(src_ref.at[phys], buf_ref.at[slot], sem_ref.at[slot])

    def needed(g):
        return pany_ref[g // per_b, g % npg] != 0

    def start(g):
        is_k = g % per_b < npg

        @pl.when(jnp.logical_and(needed(g), is_k))
        def _():
            page_copy(g, ck_ref).start()

        @pl.when(jnp.logical_and(needed(g), jnp.logical_not(is_k)))
        def _():
            page_copy(g, cv_ref).start()

    def wait(g):
        @pl.when(needed(g))
        def _():
            page_copy(g, ck_ref).wait()

    @pl.when(b == 0)
    def _():
        for slot in range(PAGE_BUFS):
            buf_ref[slot] = jnp.zeros(buf_ref.shape[1:], f32)
        for g in range(PAGE_BUFS):
            start(g)

    g0 = b * per_b
    qbd = (qbd_ref[0] * (HEAD_DIM ** -0.5 * LOG2E)).astype(bf16)
    blk = (lax.broadcasted_iota(jnp.int32, (N_HEADS, D_ATTN), 0)
           == lax.broadcasted_iota(jnp.int32, (N_HEADS, D_ATTN), 1) // HEAD_DIM)
    tok = lax.broadcasted_iota(jnp.int32, (N_HEADS, LANES), 1)

    def bias_of(dist):
        far = rbt_ref[N_BUCKETS - 1]
        acc = jnp.broadcast_to(rbt_ref[0] - far, dist.shape)
        for i, t in enumerate(BUCKET_THRESH):
            acc = jnp.where(dist >= t, rbt_ref[i + 1] - far, acc)
        return acc * LOG2E

    def refill(gs):
        for g in gs:
            @pl.when(g + PAGE_BUFS < total)
            def _():
                start(g + PAGE_BUFS)

    def k_body(it, m):
        gs = [g0 + it * PAGE_GROUP + u for u in range(PAGE_GROUP)]
        for g in gs:
            wait(g)
        for u, g in enumerate(gs):
            p = it * PAGE_GROUP + u
            kp = buf_ref[g % PAGE_BUFS].astype(bf16)
            s = jnp.dot(qbd, kp, preferred_element_type=f32) + mask_ref[0, pl.ds(p, 1), :]
            lg_ref[p] = s
            m = jnp.maximum(m, s)
        refill(gs)
        return m

    m = lax.fori_loop(0, npg // PAGE_GROUP, k_body, jnp.full((N_HEADS, LANES), NEG, f32))
    last = lg_ref[npg - 1] + bias_of(past - ((npg - 1) * LANES + tok))
    lg_ref[npg - 1] = last
    m = jnp.maximum(m, last)

    s_new = (jnp.sum(qbd_ref[0] * knew_ref[0], axis=1, keepdims=True) * (HEAD_DIM ** -0.5 * LOG2E)
             + bias_of(jnp.zeros((N_HEADS, LANES), jnp.int32))[:, 0:1] + mnew_ref[0][:, 0:1])
    m_row = jnp.maximum(jnp.max(m, axis=1, keepdims=True), s_new)
    mb = jnp.broadcast_to(m_row, (N_HEADS, LANES))

    def v_body(it, carry):
        acc, l = carry
        gs = [g0 + npg + it * PAGE_GROUP + u for u in range(PAGE_GROUP)]
        for g in gs:
            wait(g)
        for u, g in enumerate(gs):
            vp = buf_ref[g % PAGE_BUFS].astype(bf16)
            pr = jnp.exp2(lg_ref[it * PAGE_GROUP + u] - mb)
            acc = acc + lax.dot_general(pr.astype(bf16), vp, (((1,), (1,)), ((), ())),
                                        preferred_element_type=f32)
            l = l + pr
        refill(gs)
        return acc, l

    acc, l = lax.fori_loop(0, npg // PAGE_GROUP, v_body, (jnp.zeros((N_HEADS, D_ATTN), f32),
                                                          jnp.zeros((N_HEADS, LANES), f32)))
    p_new = jnp.exp2(s_new - m_row)
    l_tot = jnp.sum(l, axis=1, keepdims=True) + p_new
    num = jnp.where(blk, acc + p_new * vnew_ref[0], 0.0) / l_tot
    out_ref[0] = jnp.sum(num, axis=0, keepdims=True)


def _sample_attention(page_table, pany, qbd, k_new_flat, v_new_flat, mask, mnew, rbt, ck_t, cv_t, *, past):
    Bd, npg = page_table.shape
    rows, page = ck_t.shape[1], ck_t.shape[2]
    assert rows == D_ATTN and page == LANES
    assert npg % PAGE_GROUP == 0 and PAGE_BUFS % PAGE_GROUP == 0 and 2 * npg >= PAGE_BUFS
    grid_spec = pltpu.PrefetchScalarGridSpec(
        num_scalar_prefetch=2,
        grid=(Bd,),
        in_specs=[
            pl.BlockSpec((1, N_HEADS, D_ATTN), lambda b, pt, pa: (b, 0, 0)),
            pl.BlockSpec((1, 1, D_ATTN), lambda b, pt, pa: (b, 0, 0)),
            pl.BlockSpec((1, 1, D_ATTN), lambda b, pt, pa: (b, 0, 0)),
            pl.BlockSpec((1, npg, LANES), lambda b, pt, pa: (b, 0, 0)),
            pl.BlockSpec((1, N_HEADS, LANES), lambda b, pt, pa: (b, 0, 0)),
            pl.BlockSpec((N_BUCKETS, N_HEADS, LANES), lambda b, pt, pa: (0, 0, 0)),
            pl.BlockSpec(memory_space=pl.ANY),
            pl.BlockSpec(memory_space=pl.ANY),
        ],
        out_specs=pl.BlockSpec((1, 1, D_ATTN), lambda b, pt, pa: (b, 0, 0)),
        scratch_shapes=[
            pltpu.VMEM((PAGE_BUFS, rows, page), f32),
            pltpu.VMEM((npg, N_HEADS, LANES), f32),
            pltpu.SemaphoreType.DMA((PAGE_BUFS,)),
        ],
    )
    return pl.pallas_call(
        functools.partial(_sample_attn_kernel, npg=npg, past=past),
        grid_spec=grid_spec,
        out_shape=jax.ShapeDtypeStruct((Bd, 1, D_ATTN), f32),
        compiler_params=pltpu.CompilerParams(
            dimension_semantics=("arbitrary",), vmem_limit_bytes=VMEM_LIMIT),
        name="sample_attention",
    )(page_table, pany, qbd, k_new_flat, v_new_flat, mask, mnew, rbt, ck_t, cv_t)


def _sample_merge_kernel(attn_ref, ga_ref, u_ref, gc_ref, st_ref, x_ref, wo_ref, dw_ref, dwb_ref,
                         lng_ref, lnb_ref, y_ref):
    acc = dwb_ref[...] + u_ref[...] * dw_ref[CONV_W - 1:CONV_W, :]
    for w in range(CONV_W - 1):
        acc = acc + st_ref[w] * dw_ref[w:w + 1, :]
    cg = (_layer_norm_swish(acc, lng_ref, lnb_ref) * gc_ref[...]).astype(bf16)
    mixa = (attn_ref[...] * ga_ref[...]).astype(bf16)
    y_ref[...] = (x_ref[...]
                  + jnp.dot(mixa, wo_ref[0:D_ATTN], preferred_element_type=f32)
                  + jnp.dot(cg, wo_ref[D_ATTN:D_ATTN + D_CONV], preferred_element_type=f32))


def _sample_merge(attn, ga, u, gc, state_t, x, wo, dw, dwb, lng, lnb):
    vm = pl.BlockSpec(memory_space=pltpu.VMEM)
    return pl.pallas_call(
        _sample_merge_kernel,
        in_specs=[vm] * 11,
        out_specs=vm,
        out_shape=jax.ShapeDtypeStruct(x.shape, f32),
        compiler_params=pltpu.CompilerParams(vmem_limit_bytes=VMEM_LIMIT),
        name="sample_merge",
    )(attn, ga, u, gc, state_t, x, wo, dw, dwb, lng, lnb)


def kernel(x_prompt, x_sample, cache_k, cache_v, cache_k_idx, state_conv, page_table, rel_bias, norm_g,
           w_in, q_norm_g, k_norm_g, dw_w, dw_b, ln_g, ln_b, w_out):
    depth = norm_g.shape[0]
    assert depth == 1, "single-layer step"
    B, S, D = x_prompt.shape
    Bd, T, _ = x_sample.shape
    assert T == 1
    n_phys, page = cache_k.shape[1], cache_k.shape[2]
    past = page_table.shape[1] * page

    w = w_in[0]
    c_ki = 2 * D_ATTN + 2 * N_HEADS * HEAD_DIM + N_IDX_HEADS * IDX_DIM
    c_conv = c_ki + IDX_DIM + N_IDX_HEADS
    w_groups = (w[:, :c_ki].astype(bf16),
                jnp.pad(w[:, c_ki:c_conv], ((0, 0), (0, LANES - IDX_DIM - N_IDX_HEADS))).astype(bf16),
                w[:, c_conv:].astype(bf16))
    assert [g.shape[1] for g in w_groups] == [_C_KW, _C_A - _C_KW, _C_END - _C_A]
    wo = w_out[0].astype(bf16)
    g = norm_g[0][None]
    qg = jnp.tile(q_norm_g[0], N_HEADS)[None]
    kg = jnp.tile(k_norm_g[0], N_HEADS)[None]
    hid = np.arange(D_ATTN) // HEAD_DIM
    bd = jnp.asarray((hid[:, None] == hid[None, :]) / HEAD_DIM, dtype=bf16)
    dwb, lng, lnb = dw_b[0][None], ln_g[0][None], ln_b[0][None]

    (qt, k_p, kb, v_p, vt, ga_p, qit, kw_p, wit, ki2, u_p, gc_p) = _inproj(
        x_prompt, g, w_groups, qg, kg, bd, prompt=True)
    btab = _bias_tables(rel_bias)
    mixa = _prompt_attention(qt, qit, wit, kb, ki2, vt, ga_p, btab)
    dw8 = jnp.broadcast_to(dw_w[0][:, None, :], (CONV_W, SUB, D_CONV))
    y_prompt = _prompt_merge(mixa, u_p, gc_p, x_prompt, wo, dw8, dwb, lng, lnb)

    xs = x_sample.reshape(1, Bd, D)
    q_s, k_s, v_s, ga_s, qi_s, kw_s, u_s, gc_s = (a[0] for a in _inproj(
        xs, g, w_groups, qg, kg, bd, prompt=False))
    ki_s = kw_s[:, :IDX_DIM]
    wrow = jnp.broadcast_to(kw_s[:, IDX_DIM:IDX_DIM + N_IDX_HEADS, None], (Bd, N_IDX_HEADS, LANES))
    cki_t = jnp.transpose(cache_k_idx[0], (0, 2, 1))
    ck_t = jnp.transpose(cache_k[0], (0, 2, 3, 1)).reshape(n_phys, D_ATTN, page)
    cv_t = jnp.transpose(cache_v[0], (0, 2, 3, 1)).reshape(n_phys, D_ATTN, page)
    scores = _sample_scores(page_table, qi_s.reshape(Bd, N_IDX_HEADS, IDX_DIM), wrow, cki_t)
    mask, mnew, pany = _sample_select(scores, qi_s, jnp.tile(ki_s, (1, N_IDX_HEADS)), kw_s, past=past)
    rbt = jnp.broadcast_to(rel_bias[:, :, None], (N_BUCKETS, N_HEADS, LANES))
    eye = jnp.asarray(np.arange(N_HEADS)[:, None] == hid[None, :], dtype=f32)
    qbd = q_s[:, None, :] * eye[None]
    attn_s = _sample_attention(page_table, pany[:, :, 0], qbd, k_s[:, None, :], v_s[:, None, :],
                               mask, mnew, rbt, ck_t, cv_t, past=past)
    state_t = jnp.transpose(state_conv[0], (1, 0, 2))
    y_sample = _sample_merge(attn_s[:, 0], ga_s, u_s, gc_s, state_t, x_sample[:, 0], wo,
                             dw_w[0], dwb, lng, lnb)

    heads = (N_HEADS, HEAD_DIM)
    return (
        y_prompt,
        y_sample[:, None],
        k_p.reshape((1, B, S) + heads),
        v_p.reshape((1, B, S) + heads),
        kw_p[None, :, :, :IDX_DIM],
        u_p[None, :, S - (CONV_W - 1):],
        k_s.reshape((1, Bd, 1) + heads),
        v_s.reshape((1, Bd, 1) + heads),
        ki_s.reshape(1, Bd, 1, IDX_DIM),
        jnp.concatenate([state_conv[0][:, 1:], u_s[:, None]], axis=1)[None],
    )
```

```python
import functools
import math

import numpy as np
import jax
import jax.numpy as jnp
from jax import lax
from jax.experimental import pallas as pl
from jax.experimental.pallas import tpu as pltpu

N_HEADS = 8
HEAD_DIM = 64
N_IDX_HEADS = 8
IDX_DIM = 64
D_ATTN = N_HEADS * HEAD_DIM
D_CONV = 512
CONV_W = 31
TOPK_MAX = 256
N_BUCKETS = 32
MAX_DISTANCE = 128
EPS = 1e-6
NEG = -1e30
INT_MIN = -2 ** 31
LOG2E = math.log2(math.e)

LANES = 128
Q_TILE = 256
K_CHUNK = 256
PV_ROWS = 144
VMEM_LIMIT = 56 * 1024 * 1024

f32 = jnp.float32
bf16 = jnp.bfloat16


def _bucket_thresholds():
    n = np.arange(0, 4 * MAX_DISTANCE)
    max_exact = N_BUCKETS // 2
    nf = np.maximum(n, 1).astype(np.float32)
    large = max_exact + (np.log(nf / np.float32(max_exact)) / np.float32(math.log(MAX_DISTANCE / max_exact))
                         * np.float32(N_BUCKETS - max_exact)).astype(np.int32)
    bucket = np.where(n < max_exact, n, np.minimum(large, N_BUCKETS - 1))
    return [int(n[bucket >= i].min()) for i in range(1, N_BUCKETS)]


BUCKET_THRESH = _bucket_thresholds()
assert BUCKET_THRESH[-1] <= K_CHUNK + 1 and K_CHUNK % Q_TILE == 0
N_PAR = K_CHUNK // Q_TILE


def _silu(x):
    return x * jax.nn.sigmoid(x)


def _to_key(x):
    b = pltpu.bitcast(x, jnp.int32)
    return jnp.where(b < 0, b ^ jnp.int32(0x7FFFFFFF), b)


_C_Q, _C_K, _C_V, _C_GA, _C_QI, _C_KW, _C_A, _C_B, _C_GC, _C_END = (
    0, 512, 1024, 1536, 2048, 2560, 2688, 3200, 3712, 4224)


def _inproj_kernel(x_ref, g_ref, wa_ref, wb_ref, wc_ref, qg_ref, kg_ref, bd_ref, *outs, prompt, tm):
    xf = x_ref[0]
    ms = jnp.mean(xf * xf, axis=-1, keepdims=True)
    xn = (xf * lax.rsqrt(ms + EPS) * g_ref[...]).astype(bf16)

    def proj(a, b):
        if b <= _C_KW:
            w = wa_ref[:, a:b]
        elif b <= _C_A:
            w = wb_ref[:, a - _C_KW:b - _C_KW]
        else:
            w = wc_ref[:, a - _C_A:b - _C_A]
        return jnp.dot(xn, w, preferred_element_type=f32)

    def head_norm(z, gain_ref):
        msq = jnp.dot((z * z).astype(bf16), bd_ref[...], preferred_element_type=f32)
        return z * lax.rsqrt(msq + EPS) * gain_ref[...]

    q = head_norm(proj(_C_Q, _C_K), qg_ref)
    k = head_norm(proj(_C_K, _C_V), kg_ref)
    v = proj(_C_V, _C_GA)
    ga = _silu(proj(_C_GA, _C_QI))
    qi = proj(_C_QI, _C_KW)
    kw = proj(_C_KW, _C_A)
    u = proj(_C_A, _C_B) * jax.nn.sigmoid(proj(_C_B, _C_GC))
    gc = _silu(proj(_C_GC, _C_END))

    if prompt:
        q_o, k_o, kb_o, v_o, vt_o, ga_o, qi_o, kw_o, wi_o, ki2_o, u_o, gc_o = outs
        q_o[0] = (q * (HEAD_DIM ** -0.5 * LOG2E)).T.astype(bf16)
        qi_o[0] = (qi * (IDX_DIM ** -0.5)).T.astype(bf16)
        wi_o[0] = kw.T[IDX_DIM:IDX_DIM + N_IDX_HEADS, :]
        kb_o[0] = k.astype(bf16)
        lane = lax.broadcasted_iota(jnp.int32, kw.shape, 1)
        ki2_o[0] = jnp.where(lane < IDX_DIM, kw, pltpu.roll(kw, IDX_DIM, 1)).astype(bf16)
        vt = v.T.astype(bf16)
        tail = jnp.where(lax.broadcasted_iota(jnp.int32, (PV_ROWS - 2 * HEAD_DIM, K_CHUNK), 0) < 8,
                         1.0, 0.0).astype(bf16)
        for c in range(tm // K_CHUNK):
            for i in range(N_HEADS // 2):
                blk = vt[i * 2 * HEAD_DIM:(i + 1) * 2 * HEAD_DIM, c * K_CHUNK:(c + 1) * K_CHUNK]
                vt_o[0, c, i] = jnp.concatenate([blk, tail], axis=0)
    else:
        q_o, k_o, v_o, ga_o, qi_o, kw_o, u_o, gc_o = outs
        q_o[0] = q
        qi_o[0] = qi
    k_o[0] = k
    v_o[0] = v
    ga_o[0] = ga
    kw_o[0] = kw
    u_o[0] = u
    gc_o[0] = gc


def _inproj(x, norm_g, w_groups, qg, kg, bd, *, prompt):
    B, S, D = x.shape
    tm = 512 if prompt else S
    assert S % tm == 0 and (not prompt or tm % K_CHUNK == 0)
    grid = (B, S // tm)
    row = lambda b, i: (b, i, 0)
    const2 = lambda b, i: (0, 0)
    col = lambda b, i: (b, 0, i)
    nck = tm // K_CHUNK
    if prompt:
        out_shape = (
            jax.ShapeDtypeStruct((B, D_ATTN, S), bf16),
            jax.ShapeDtypeStruct((B, S, D_ATTN), f32),
            jax.ShapeDtypeStruct((B, S, D_ATTN), bf16),
            jax.ShapeDtypeStruct((B, S, D_ATTN), f32),
            jax.ShapeDtypeStruct((B, S // K_CHUNK, N_HEADS // 2, PV_ROWS, K_CHUNK), bf16),
            jax.ShapeDtypeStruct((B, S, D_ATTN), f32),
            jax.ShapeDtypeStruct((B, N_IDX_HEADS * IDX_DIM, S), bf16),
            jax.ShapeDtypeStruct((B, S, LANES), f32),
            jax.ShapeDtypeStruct((B, N_IDX_HEADS, S), f32),
            jax.ShapeDtypeStruct((B, S, LANES), bf16),
            jax.ShapeDtypeStruct((B, S, D_CONV), f32),
            jax.ShapeDtypeStruct((B, S, D_CONV), f32),
        )
        out_specs = (
            pl.BlockSpec((1, D_ATTN, tm), col),
            pl.BlockSpec((1, tm, D_ATTN), row),
            pl.BlockSpec((1, tm, D_ATTN), row),
            pl.BlockSpec((1, tm, D_ATTN), row),
            pl.BlockSpec((1, nck, N_HEADS // 2, PV_ROWS, K_CHUNK), lambda b, i: (b, i, 0, 0, 0)),
            pl.BlockSpec((1, tm, D_ATTN), row),
            pl.BlockSpec((1, N_IDX_HEADS * IDX_DIM, tm), col),
            pl.BlockSpec((1, tm, LANES), row),
            pl.BlockSpec((1, N_IDX_HEADS, tm), col),
            pl.BlockSpec((1, tm, LANES), row),
            pl.BlockSpec((1, tm, D_CONV), row),
            pl.BlockSpec((1, tm, D_CONV), row),
        )
    else:
        out_shape = (
            jax.ShapeDtypeStruct((B, S, D_ATTN), f32),
            jax.ShapeDtypeStruct((B, S, D_ATTN), f32),
            jax.ShapeDtypeStruct((B, S, D_ATTN), f32),
            jax.ShapeDtypeStruct((B, S, D_ATTN), f32),
            jax.ShapeDtypeStruct((B, S, D_ATTN), f32),
            jax.ShapeDtypeStruct((B, S, LANES), f32),
            jax.ShapeDtypeStruct((B, S, D_CONV), f32),
            jax.ShapeDtypeStruct((B, S, D_CONV), f32),
        )
        out_specs = tuple(pl.BlockSpec((1, tm, s.shape[-1]), row) for s in out_shape)
    return pl.pallas_call(
        functools.partial(_inproj_kernel, prompt=prompt, tm=tm),
        grid=grid,
        in_specs=[
            pl.BlockSpec((1, tm, D), row),
            pl.BlockSpec((1, D), const2),
            pl.BlockSpec((D, _C_KW), const2),
            pl.BlockSpec((D, _C_A - _C_KW), const2),
            pl.BlockSpec((D, _C_END - _C_A), const2),
            pl.BlockSpec((1, D_ATTN), const2),
            pl.BlockSpec((1, D_ATTN), const2),
            pl.BlockSpec((D_ATTN, D_ATTN), const2),
        ],
        out_specs=out_specs,
        out_shape=out_shape,
        compiler_params=pltpu.CompilerParams(
            dimension_semantics=("parallel", "parallel"), vmem_limit_bytes=VMEM_LIMIT),
        name="inproj_prompt" if prompt else "inproj_sample",
    )(x, norm_g, *w_groups, qg, kg, bd)


def _bias_table_kernel(rb_ref, out_ref):
    shape = (K_CHUNK, Q_TILE)
    krow = lax.broadcasted_iota(jnp.int32, shape, 0)
    qcol = lax.broadcasted_iota(jnp.int32, shape, 1)
    for par in range(N_PAR):
        for which in range(2):
            dist = Q_TILE * par + K_CHUNK * (1 - which) + qcol - krow
            for h in range(N_HEADS):
                far = rb_ref[N_BUCKETS - 1, h]
                acc = jnp.full(shape, (rb_ref[0, h] - far) * LOG2E, f32)
                for i, t in enumerate(BUCKET_THRESH):
                    acc = jnp.where(dist >= t, (rb_ref[i + 1, h] - far) * LOG2E, acc)
                out_ref[par, which, h // 2, :, (h % 2) * Q_TILE:(h % 2 + 1) * Q_TILE] = (
                    jnp.where(dist >= 0, acc, NEG))


def _bias_tables(rel_bias):
    return pl.pallas_call(
        _bias_table_kernel,
        in_specs=[pl.BlockSpec(memory_space=pltpu.SMEM)],
        out_specs=pl.BlockSpec(memory_space=pltpu.VMEM),
        out_shape=jax.ShapeDtypeStruct((N_PAR, 2, N_HEADS // 2, K_CHUNK, 2 * Q_TILE), f32),
        name="bias_tables",
    )(rel_bias)


N_PAIRS = N_HEADS // 2
SUB = 8


COUNT_ROWS = 32
MAX_ROWS = 16


def _bit_planes(words):
    a = list(words)
    j, m = 16, 0x0000FFFF
    while j:
        k = 0
        while k < 32:
            t = (a[k] ^ lax.shift_right_logical(a[k + j], jnp.int32(j))) & jnp.int32(m)
            a[k] = a[k] ^ t
            a[k + j] = a[k + j] ^ jnp.left_shift(t, jnp.int32(j))
            k = (k + j + 1) & ~j
        j >>= 1
        m = (m ^ (m << j)) & 0xFFFFFFFF
    return a


def _fold(x, op, rows):
    return op(x.reshape(x.shape[0] // rows, rows, x.shape[1]), axis=0)


def _attn_kernel(qt_ref, qit_ref, wit_ref, k_ref, ki_ref, vt_ref, ga_ref, bt_ref, out_ref,
                 sc_ref, key_ref, pl_ref, lg_ref, qbd_ref, qibd_ref, o_ref, *, ksel, idx_bits):
    j = pl.program_id(1)
    n_chunks = sc_ref.shape[0]
    nck = j // N_PAR + 1
    par = j % N_PAR
    shape = (K_CHUNK, Q_TILE)
    key_pos = lax.broadcasted_iota(jnp.int32, shape, 0)
    q_pos = j * Q_TILE + lax.broadcasted_iota(jnp.int32, shape, 1)

    top = lax.broadcasted_iota(jnp.int32, (2 * HEAD_DIM, Q_TILE), 0) < HEAD_DIM
    for i in range(N_PAIRS):
        rows = slice(i * 2 * HEAD_DIM, (i + 1) * 2 * HEAD_DIM)
        for src, dst in ((qt_ref, qbd_ref), (qit_ref, qibd_ref)):
            x = src[0, rows, :]
            zero = jnp.zeros_like(x)
            dst[i] = jnp.concatenate([jnp.where(top, x, zero), jnp.where(top, zero, x)], axis=1)
    wi = wit_ref[0] * (N_IDX_HEADS ** -0.5)
    wpair = [jnp.concatenate([wi[2 * i:2 * i + 1], wi[2 * i + 1:2 * i + 2]], axis=1)
             for i in range(N_PAIRS)]

    def chunk_rows(c):
        return pl.ds(pl.multiple_of(c * K_CHUNK, K_CHUNK), K_CHUNK)

    def for_chunk_pairs(n, body):
        def pair(cc, carry):
            body([2 * cc, 2 * cc + 1])
            return carry
        lax.fori_loop(0, n // 2, pair, 0)

        @pl.when(n % 2 == 1)
        def _():
            body([n - 1])

    def score_chunks(chunks):
        for c in chunks:
            ki2 = ki_ref[0, chunk_rows(c), :]
            acc = jnp.zeros(shape, f32)
            for i in range(N_PAIRS):
                r = jnp.maximum(jnp.dot(ki2, qibd_ref[i], preferred_element_type=f32), 0.0) * wpair[i]
                acc = acc + r[:, :Q_TILE] + r[:, Q_TILE:]
            keys = _to_key(jnp.where(c * K_CHUNK + key_pos <= q_pos, acc, -jnp.inf))
            key_ref[c] = keys
            ukeys = keys ^ jnp.int32(INT_MIN)
            planes = _bit_planes([ukeys[t * SUB:(t + 1) * SUB, :] for t in range(32)])
            for p in range(32):
                pl_ref[c, p] = planes[p]

    for_chunk_pairs(nck, score_chunks)

    def count(pred):
        def body(c, cnt):
            hit = pred(c)
            for g in range(K_CHUNK // COUNT_ROWS):
                rows = slice(g * COUNT_ROWS, (g + 1) * COUNT_ROWS)
                cnt = jnp.where(hit[rows], cnt + 1.0, cnt)
            return cnt
        cnt = lax.fori_loop(0, nck, body, jnp.zeros((COUNT_ROWS, Q_TILE), f32))
        return jnp.sum(cnt, axis=0, keepdims=True)

    def write_mask(pred):
        def body(c, carry):
            sc_ref[c] = jnp.where(pred(c), 0.0, NEG)
            return carry
        lax.fori_loop(0, nck, body, 0)

    need_search = (j + 1) * Q_TILE > ksel

    @pl.when(jnp.logical_not(need_search))
    def _():
        write_mask(lambda c: c * K_CHUNK + key_pos < (j + 1) * Q_TILE)

    @pl.when(need_search)
    def _():
        def clear_body(c, carry):
            pl_ref[c] = jnp.zeros(pl_ref.shape[1:], jnp.int32)
            return carry
        lax.fori_loop(nck, n_chunks, clear_body, 0)

        def bit_body(p, carry):
            n_above, thr_u, alive = carry
            ones = [alive[c] & pl_ref[c, p] for c in range(n_chunks)]
            acc = lax.population_count(ones[0])
            for c in range(1, n_chunks):
                acc = acc + lax.population_count(ones[c])
            n_hi = n_above + jnp.sum(acc.astype(f32), axis=0, keepdims=True)
            take = n_hi >= ksel
            takeb = jnp.broadcast_to(take, (SUB, Q_TILE))
            alive = tuple(jnp.where(takeb, ones[c], alive[c] ^ ones[c]) for c in range(n_chunks))
            bit = jnp.left_shift(jnp.int32(1), 31 - p)
            return jnp.where(take, n_above, n_hi), jnp.where(take, thr_u | bit, thr_u), alive

        everything = jnp.full((SUB, Q_TILE), -1, jnp.int32)
        n_above, thr_u, alive = lax.fori_loop(
            0, 32, bit_body,
            (jnp.zeros((1, Q_TILE), f32), jnp.zeros((1, Q_TILE), jnp.int32),
             tuple(jnp.where(c < nck, everything, 0) for c in range(n_chunks))))
        n_eq = lax.population_count(alive[0])
        for c in range(1, n_chunks):
            n_eq = n_eq + lax.population_count(alive[c])
        n_ge = n_above + jnp.sum(n_eq.astype(f32), axis=0, keepdims=True)
        thr = thr_u ^ jnp.int32(INT_MIN)
        thrb = jnp.broadcast_to(thr, shape)
        has_tie = jnp.max(n_ge) > ksel

        @pl.when(jnp.logical_not(has_tie))
        def _():
            write_mask(lambda c: key_ref[c] >= thrb)

        @pl.when(has_tie)
        def _():
            need = ksel - count(lambda c: key_ref[c] > thrb)

            def cut_body(i, x):
                cand = x + jnp.left_shift(jnp.int32(1), idx_bits - 1 - i)
                candb = jnp.broadcast_to(cand, shape)
                g = count(lambda c: jnp.where(key_ref[c] == thrb, c * K_CHUNK + key_pos, 2 ** 30) < candb)
                return jnp.where(g < need, cand, x)

            cut = lax.fori_loop(0, idx_bits, cut_body, jnp.zeros((1, Q_TILE), jnp.int32))
            cutb = jnp.broadcast_to(cut, shape)

            def sel(c):
                kc = key_ref[c]
                eq_idx = jnp.where(kc == thrb, c * K_CHUNK + key_pos, 2 ** 30)
                return jnp.where(kc > thrb, 0, eq_idx) <= cutb
            write_mask(sel)

    def logits_chunk(c, which, ms):
        mask = sc_ref[c]
        mask2 = jnp.concatenate([mask, mask], axis=1)
        out = []
        for i in range(N_PAIRS):
            kc = k_ref[0, chunk_rows(c), i * 2 * HEAD_DIM:(i + 1) * 2 * HEAD_DIM]
            s = jnp.dot(kc, qbd_ref[i], preferred_element_type=f32) + mask2
            if which is not None:
                s = s + bt_ref[par, which, i]
            lg_ref[i, c] = s
            out.append(jnp.maximum(ms[i], _fold(s, jnp.max, MAX_ROWS)))
        return tuple(out)

    ms = tuple(jnp.full((MAX_ROWS, 2 * Q_TILE), NEG, f32) for _ in range(N_PAIRS))
    nfar = jnp.maximum(nck - 2, 0)
    ms = lax.fori_loop(
        0, nfar // 2, lambda cc, ms: logits_chunk(2 * cc + 1, None, logits_chunk(2 * cc, None, ms)), ms)
    ms = lax.cond(nfar % 2 == 1, lambda ms: logits_chunk(nfar - 1, None, ms), lambda ms: ms, ms)
    ms = lax.cond(nck >= 2,
                  lambda ms: logits_chunk(nck - 1, 1, logits_chunk(nck - 2, 0, ms)),
                  lambda ms: logits_chunk(nck - 1, 1, ms), ms)
    mrow = [jnp.max(m, axis=0, keepdims=True) for m in ms]

    o_ref[...] = jnp.zeros(o_ref.shape, f32)

    def pv_chunks(chunks):
        for i in range(N_PAIRS):
            acc = None
            for c in chunks:
                p = jnp.exp2((lg_ref[i, c] - mrow[i]).astype(bf16))
                d = jnp.dot(vt_ref[0, c, i], p, preferred_element_type=f32)
                acc = d if acc is None else acc + d
            o_ref[i] += acc
    for_chunk_pairs(nck, pv_chunks)

    pieces = []
    for i in range(N_PAIRS):
        o = o_ref[i]
        l = o[2 * HEAD_DIM:2 * HEAD_DIM + 1, :]
        pieces.append(o[0:HEAD_DIM, 0:Q_TILE] / l[:, 0:Q_TILE])
        pieces.append(o[HEAD_DIM:2 * HEAD_DIM, Q_TILE:] / l[:, Q_TILE:])
    out_ref[0] = (jnp.concatenate(pieces, axis=0).T * ga_ref[0]).astype(bf16)


def _prompt_attention(qt, qit, wit, kb, ki2, vt, ga, btab):
    B, S, _ = kb.shape
    assert S % K_CHUNK == 0
    nq = S // Q_TILE
    nc = S // K_CHUNK
    ksel = min(TOPK_MAX, S // 4)
    qcol = lambda b, j: (b, 0, j)
    perb3 = lambda b, j: (b, 0, 0)
    return pl.pallas_call(
        functools.partial(_attn_kernel, ksel=ksel, idx_bits=max(1, (S - 1).bit_length())),
        grid=(B, nq),
        in_specs=[
            pl.BlockSpec((1, D_ATTN, Q_TILE), qcol),
            pl.BlockSpec((1, N_IDX_HEADS * IDX_DIM, Q_TILE), qcol),
            pl.BlockSpec((1, N_IDX_HEADS, Q_TILE), qcol),
            pl.BlockSpec((1, S, D_ATTN), perb3),
            pl.BlockSpec((1, S, LANES), perb3),
            pl.BlockSpec((1, nc, N_PAIRS, PV_ROWS, K_CHUNK), lambda b, j: (b, 0, 0, 0, 0)),
            pl.BlockSpec((1, Q_TILE, D_ATTN), lambda b, j: (b, j, 0)),
            pl.BlockSpec((N_PAR, 2, N_PAIRS, K_CHUNK, 2 * Q_TILE), lambda b, j: (0, 0, 0, 0, 0)),
        ],
        out_specs=pl.BlockSpec((1, Q_TILE, D_ATTN), lambda b, j: (b, j, 0)),
        out_shape=jax.ShapeDtypeStruct((B, S, D_ATTN), bf16),
        scratch_shapes=[
            pltpu.VMEM((nc, K_CHUNK, Q_TILE), f32),
            pltpu.VMEM((nc, K_CHUNK, Q_TILE), jnp.int32),
            pltpu.VMEM((nc, 32, SUB, Q_TILE), jnp.int32),
            pltpu.VMEM((N_PAIRS, nc, K_CHUNK, 2 * Q_TILE), f32),
            pltpu.VMEM((N_PAIRS, 2 * HEAD_DIM, 2 * Q_TILE), bf16),
            pltpu.VMEM((N_PAIRS, 2 * IDX_DIM, 2 * Q_TILE), bf16),
            pltpu.VMEM((N_PAIRS, PV_ROWS, 2 * Q_TILE), f32),
        ],
        compiler_params=pltpu.CompilerParams(
            dimension_semantics=("parallel", "arbitrary"), vmem_limit_bytes=VMEM_LIMIT),
        name="prompt_attention",
    )(qt, qit, wit, kb, ki2, vt, ga, btab)


CONV_HALO = 32
CONV_ROWS = 32


def _layer_norm_swish(c, lng_ref, lnb_ref):
    mu = jnp.mean(c, axis=-1, keepdims=True)
    d = c - mu
    var = jnp.mean(d * d, axis=-1, keepdims=True)
    return _silu(d * lax.rsqrt(var + EPS) * lng_ref[...] + lnb_ref[...])


def _merge_kernel(mixa_ref, ucur_ref, uprev_ref, gc_ref, x_ref, wo_ref, dw_ref, dwb_ref, lng_ref,
                  lnb_ref, y_ref, ext_ref, sh_ref, cg_ref, *, tq):
    i = pl.program_id(1)
    n = CONV_HALO + tq
    ext_ref[0:CONV_HALO] = jnp.where(i > 0, uprev_ref[0], 0.0)
    ext_ref[CONV_HALO:n] = ucur_ref[0]
    ext_ref[n:n + SUB] = jnp.zeros((SUB, D_CONV), f32)
    ext = ext_ref[...]
    for r in range(1, SUB):
        sh_ref[r - 1] = pltpu.roll(ext, n + SUB - r, 0)[0:n, :]
    first = CONV_HALO - (CONV_W - 1)
    for c in range(tq // CONV_ROWS):
        acc = jnp.broadcast_to(dwb_ref[...], (CONV_ROWS, D_CONV))
        for w in range(CONV_W):
            a, r = divmod(first + w, SUB)
            start = c * CONV_ROWS + a * SUB
            src = ext_ref if r == 0 else sh_ref.at[r - 1]
            acc = acc + src[start:start + CONV_ROWS, :] * jnp.tile(dw_ref[w], (CONV_ROWS // SUB, 1))
        rows = slice(c * CONV_ROWS, (c + 1) * CONV_ROWS)
        cg_ref[rows] = (_layer_norm_swish(acc, lng_ref, lnb_ref) * gc_ref[0, rows]).astype(bf16)
    y_ref[0] = (x_ref[0]
                + jnp.dot(mixa_ref[0], wo_ref[0:D_ATTN], preferred_element_type=f32)
                + jnp.dot(cg_ref[...], wo_ref[D_ATTN:D_ATTN + D_CONV], preferred_element_type=f32))


def _prompt_merge(mixa, u, gc, x, wo, dw, dwb, lng, lnb):
    B, S, D = x.shape
    tq = 256
    assert S % tq == 0 and tq % CONV_HALO == 0
    row = lambda b, i: (b, i, 0)
    const2 = lambda b, i: (0, 0)
    halo = lambda b, i: (b, jnp.maximum(i * (tq // CONV_HALO) - 1, 0), 0)
    return pl.pallas_call(
        functools.partial(_merge_kernel, tq=tq),
        grid=(B, S // tq),
        in_specs=[
            pl.BlockSpec((1, tq, D_ATTN), row),
            pl.BlockSpec((1, tq, D_CONV), row),
            pl.BlockSpec((1, CONV_HALO, D_CONV), halo),
            pl.BlockSpec((1, tq, D_CONV), row),
            pl.BlockSpec((1, tq, D), row),
            pl.BlockSpec((D_ATTN + D_CONV, D), const2),
            pl.BlockSpec((CONV_W, SUB, D_CONV), lambda b, i: (0, 0, 0)),
            pl.BlockSpec((1, D_CONV), const2),
            pl.BlockSpec((1, D_CONV), const2),
            pl.BlockSpec((1, D_CONV), const2),
        ],
        out_specs=pl.BlockSpec((1, tq, D), row),
        out_shape=jax.ShapeDtypeStruct((B, S, D), f32),
        scratch_shapes=[
            pltpu.VMEM((CONV_HALO + tq + SUB, D_CONV), f32),
            pltpu.VMEM((SUB - 1, CONV_HALO + tq, D_CONV), f32),
            pltpu.VMEM((tq, D_CONV), bf16),
        ],
        compiler_params=pltpu.CompilerParams(
            dimension_semantics=("parallel", "arbitrary"), vmem_limit_bytes=VMEM_LIMIT),
        name="prompt_merge",
    )(mixa, u, u, gc, x, wo, dw, dwb, lng, lnb)


SCORE_UNROLL = 16


def _sample_score_kernel(pt_ref, qi_ref, w_ref, cki_ref, out_ref, buf_ref, sem_ref, *, npg):
    b = pl.program_id(0)
    nb = pl.num_programs(0)
    slot = b % 2

    def page_copy(bb, p, sl):
        return pltpu.make_async_copy(cki_ref.at[pt_ref[bb, p]], buf_ref.at[sl, p], sem_ref.at[sl])

    def start_all(bb, sl):
        def body(pp, carry):
            page_copy(bb, 2 * pp, sl).start(priority=0)
            page_copy(bb, 2 * pp + 1, sl).start(priority=1)
            return carry
        lax.fori_loop(0, npg // 2, body, 0)

    @pl.when(b == 0)
    def _():
        start_all(0, 0)

    @pl.when(b + 1 < nb)
    def _():
        start_all(b + 1, 1 - slot)

    def wait_body(p, carry):
        page_copy(b, p, slot).wait()
        return carry
    lax.fori_loop(0, npg, wait_body, 0)

    qs = (qi_ref[0] * (IDX_DIM ** -0.5)).astype(bf16)
    wcol = w_ref[0] * (N_IDX_HEADS ** -0.5)
    wcol2 = jnp.concatenate([wcol, wcol], axis=1)

    def body(i, carry):
        for u in range(0, SCORE_UNROLL, 2):
            p = i * SCORE_UNROLL + u
            pages = jnp.concatenate([buf_ref[slot, p], buf_ref[slot, p + 1]], axis=1).astype(bf16)
            s = jnp.dot(qs, pages, preferred_element_type=f32)
            r = jnp.sum(jnp.maximum(s, 0.0) * wcol2, axis=0, keepdims=True)
            out_ref[0, pl.ds(p, 1), :] = r[:, :LANES]
            out_ref[0, pl.ds(p + 1, 1), :] = r[:, LANES:]
        return carry
    lax.fori_loop(0, npg // SCORE_UNROLL, body, 0)


def _sample_scores(page_table, qi3, wrow, cki_t):
    Bd, npg = page_table.shape
    page = cki_t.shape[2]
    assert page == LANES and npg % SCORE_UNROLL == 0
    grid_spec = pltpu.PrefetchScalarGridSpec(
        num_scalar_prefetch=1,
        grid=(Bd,),
        in_specs=[
            pl.BlockSpec((1, N_IDX_HEADS, IDX_DIM), lambda b, pt: (b, 0, 0)),
            pl.BlockSpec((1, N_IDX_HEADS, LANES), lambda b, pt: (b, 0, 0)),
            pl.BlockSpec(memory_space=pl.ANY),
        ],
        out_specs=pl.BlockSpec((1, npg, page), lambda b, pt: (b, 0, 0)),
        scratch_shapes=[
            pltpu.VMEM((2, npg, IDX_DIM, page), f32),
            pltpu.SemaphoreType.DMA((2,)),
        ],
    )
    return pl.pallas_call(
        functools.partial(_sample_score_kernel, npg=npg),
        grid_spec=grid_spec,
        out_shape=jax.ShapeDtypeStruct((Bd, npg, page), f32),
        compiler_params=pltpu.CompilerParams(
            dimension_semantics=("arbitrary",), vmem_limit_bytes=VMEM_LIMIT),
        name="sample_scores",
    )(page_table, qi3, wrow, cki_t)


def _sample_select_kernel(sc_ref, qi_ref, kit_ref, kw_ref, mask_ref, mnew_ref, pany_ref, key_ref, *,
                          npg, ksel, past, idx_bits):
    Bd = sc_ref.shape[0]
    shape = (Bd, npg, LANES)

    qb = (qi_ref[...] * (IDX_DIM ** -0.5)).astype(bf16).astype(f32)
    kb = kit_ref[...].astype(bf16).astype(f32)
    prod = qb * kb
    grp = lax.broadcasted_iota(jnp.int32, prod.shape, 1) // IDX_DIM
    kwv = kw_ref[...]
    s_new = jnp.zeros((Bd, 1), f32)
    for h in range(N_IDX_HEADS):
        sh = jnp.sum(jnp.where(grp == h, prod, 0.0), axis=1, keepdims=True)
        s_new = s_new + jnp.maximum(sh, 0.0) * (kwv[:, IDX_DIM + h:IDX_DIM + h + 1] * (N_IDX_HEADS ** -0.5))
    key_new = _to_key(s_new).reshape(Bd, 1, 1)

    key_ref[...] = _to_key(sc_ref[...])
    pos = (lax.broadcasted_iota(jnp.int32, shape, 1) * LANES
           + lax.broadcasted_iota(jnp.int32, shape, 2))

    def count(pred, pred_new):
        x = jnp.where(pred, 1.0, 0.0)
        s = jnp.sum(jnp.sum(x, axis=1, keepdims=True), axis=2, keepdims=True)
        return s + jnp.where(pred_new, 1.0, 0.0)

    def bit_body(i, lo):
        cand = lo + jnp.left_shift(jnp.int32(1), 31 - i)
        tot = count(key_ref[...] >= cand, key_new >= cand)
        return jnp.where(tot >= ksel, cand, lo)

    thr = lax.fori_loop(0, 32, bit_body, jnp.full((Bd, 1, 1), INT_MIN, jnp.int32))
    keys = key_ref[...]
    need = ksel - count(keys > thr, key_new > thr)
    big = jnp.int32(2 ** 30)
    eq_pos = jnp.where(keys == thr, pos, big)
    eq_pos_new = jnp.where(key_new == thr, jnp.int32(past), big)

    def cut_body(i, x):
        cand = x + jnp.left_shift(jnp.int32(1), idx_bits - 1 - i)
        g = count(eq_pos < cand, eq_pos_new < cand)
        return jnp.where(g < need, cand, x)

    cut = lax.fori_loop(0, idx_bits, cut_body, jnp.zeros((Bd, 1, 1), jnp.int32))
    sel = jnp.where(keys > thr, 0, eq_pos) <= cut
    mask_ref[...] = jnp.where(sel, 0.0, NEG)
    pany_ref[...] = jnp.max(jnp.where(sel, 1.0, 0.0), axis=2, keepdims=True).astype(jnp.int32)
    sel_new = jnp.where(key_new > thr, 0, eq_pos_new) <= cut
    mnew_ref[...] = jnp.broadcast_to(jnp.where(sel_new, 0.0, NEG), mnew_ref.shape)


def _sample_select(scores, qi, ki_tiled, kw, *, past):
    Bd, npg, _ = scores.shape
    ksel = min(TOPK_MAX, (past + 1) // 4)
    vm = pl.BlockSpec(memory_space=pltpu.VMEM)
    return pl.pallas_call(
        functools.partial(_sample_select_kernel, npg=npg, ksel=ksel, past=past,
                          idx_bits=past.bit_length()),
        in_specs=[vm, vm, vm, vm],
        out_specs=(vm, vm, vm),
        out_shape=(jax.ShapeDtypeStruct((Bd, npg, LANES), f32),
                   jax.ShapeDtypeStruct((Bd, N_HEADS, LANES), f32),
                   jax.ShapeDtypeStruct((Bd, npg, 1), jnp.int32)),
        scratch_shapes=[pltpu.VMEM((Bd, npg, LANES), jnp.int32)],
        compiler_params=pltpu.CompilerParams(vmem_limit_bytes=VMEM_LIMIT),
        name="sample_select",
    )(scores, qi, ki_tiled, kw)


PAGE_BUFS = 32
PAGE_GROUP = 4


def _sample_attn_kernel(pt_ref, pany_ref, qbd_ref, knew_ref, vnew_ref, mask_ref, mnew_ref, rbt_ref, ck_ref, cv_ref,
                        out_ref, buf_ref, lg_ref, sem_ref, *, npg, past):
    b = pl.program_id(0)
    nb = pl.num_programs(0)
    per_b = 2 * npg
    total = nb * per_b

    def page_copy(g, src_ref):
        phys = pt_ref[g // per_b, g % npg]
        slot = g % PAGE_BUFS
        return pltpu.make_async_copy(src_ref.at[phys], buf_ref.at[slot], sem_ref.at[slot])

    def needed(g):
        return pany_ref[g // per_b, g % npg] != 0

    def start(g):
        is_k = g % per_b < npg

        @pl.when(jnp.logical_and(needed(g), is_k))
        def _():
            page_copy(g, ck_ref).start()

        @pl.when(jnp.logical_and(needed(g), jnp.logical_not(is_k)))
        def _():
            page_copy(g, cv_ref).start()

    def wait(g):
        @pl.when(needed(g))
        def _():
            page_copy(g, ck_ref).wait()

    @pl.when(b == 0)
    def _():
        for slot in range(PAGE_BUFS):
            buf_ref[slot] = jnp.zeros(buf_ref.shape[1:], f32)
        for g in range(PAGE_BUFS):
            start(g)

    g0 = b * per_b
    qbd = (qbd_ref[0] * (HEAD_DIM ** -0.5 * LOG2E)).astype(bf16)
    blk = (lax.broadcasted_iota(jnp.int32, (N_HEADS, D_ATTN), 0)
           == lax.broadcasted_iota(jnp.int32, (N_HEADS, D_ATTN), 1) // HEAD_DIM)
    tok = lax.broadcasted_iota(jnp.int32, (N_HEADS, LANES), 1)

    def bias_of(dist):
        far = rbt_ref[N_BUCKETS - 1]
        acc = jnp.broadcast_to(rbt_ref[0] - far, dist.shape)
        for i, t in enumerate(BUCKET_THRESH):
            acc = jnp.where(dist >= t, rbt_ref[i + 1] - far, acc)
        return acc * LOG2E

    def refill(gs):
        for g in gs:
            @pl.when(g + PAGE_BUFS < total)
            def _():
                start(g + PAGE_BUFS)

    def k_body(it, m):
        gs = [g0 + it * PAGE_GROUP + u for u in range(PAGE_GROUP)]
        for g in gs:
            wait(g)
        for u, g in enumerate(gs):
            p = it * PAGE_GROUP + u
            kp = buf_ref[g % PAGE_BUFS].astype(bf16)
            s = jnp.dot(qbd, kp, preferred_element_type=f32) + mask_ref[0, pl.ds(p, 1), :]
            lg_ref[p] = s
            m = jnp.maximum(m, s)
        refill(gs)
        return m

    m = lax.fori_loop(0, npg // PAGE_GROUP, k_body, jnp.full((N_HEADS, LANES), NEG, f32))
    last = lg_ref[npg - 1] + bias_of(past - ((npg - 1) * LANES + tok))
    lg_ref[npg - 1] = last
    m = jnp.maximum(m, last)

    s_new = (jnp.sum(qbd_ref[0] * knew_ref[0], axis=1, keepdims=True) * (HEAD_DIM ** -0.5 * LOG2E)
             + bias_of(jnp.zeros((N_HEADS, LANES), jnp.int32))[:, 0:1] + mnew_ref[0][:, 0:1])
    m_row = jnp.maximum(jnp.max(m, axis=1, keepdims=True), s_new)
    mb = jnp.broadcast_to(m_row, (N_HEADS, LANES))

    def v_body(it, carry):
        acc, l = carry
        gs = [g0 + npg + it * PAGE_GROUP + u for u in range(PAGE_GROUP)]
        for g in gs:
            wait(g)
        for u, g in enumerate(gs):
            vp = buf_ref[g % PAGE_BUFS].astype(bf16)
            pr = jnp.exp2(lg_ref[it * PAGE_GROUP + u] - mb)
            acc = acc + lax.dot_general(pr.astype(bf16), vp, (((1,), (1,)), ((), ())),
                                        preferred_element_type=f32)
            l = l + pr
        refill(gs)
        return acc, l

    acc, l = lax.fori_loop(0, npg // PAGE_GROUP, v_body, (jnp.zeros((N_HEADS, D_ATTN), f32),
                                                          jnp.zeros((N_HEADS, LANES), f32)))
    p_new = jnp.exp2(s_new - m_row)
    l_tot = jnp.sum(l, axis=1, keepdims=True) + p_new
    num = jnp.where(blk, acc + p_new * vnew_ref[0], 0.0) / l_tot
    out_ref[0] = jnp.sum(num, axis=0, keepdims=True)


def _sample_attention(page_table, pany, qbd, k_new_flat, v_new_flat, mask, mnew, rbt, ck_t, cv_t, *, past):
    Bd, npg = page_table.shape
    rows, page = ck_t.shape[1], ck_t.shape[2]
    assert rows == D_ATTN and page == LANES
    assert npg % PAGE_GROUP == 0 and PAGE_BUFS % PAGE_GROUP == 0 and 2 * npg >= PAGE_BUFS
    grid_spec = pltpu.PrefetchScalarGridSpec(
        num_scalar_prefetch=2,
        grid=(Bd,),
        in_specs=[
            pl.BlockSpec((1, N_HEADS, D_ATTN), lambda b, pt, pa: (b, 0, 0)),
            pl.BlockSpec((1, 1, D_ATTN), lambda b, pt, pa: (b, 0, 0)),
            pl.BlockSpec((1, 1, D_ATTN), lambda b, pt, pa: (b, 0, 0)),
            pl.BlockSpec((1, npg, LANES), lambda b, pt, pa: (b, 0, 0)),
            pl.BlockSpec((1, N_HEADS, LANES), lambda b, pt, pa: (b, 0, 0)),
            pl.BlockSpec((N_BUCKETS, N_HEADS, LANES), lambda b, pt, pa: (0, 0, 0)),
            pl.BlockSpec(memory_space=pl.ANY),
            pl.BlockSpec(memory_space=pl.ANY),
        ],
        out_specs=pl.BlockSpec((1, 1, D_ATTN), lambda b, pt, pa: (b, 0, 0)),
        scratch_shapes=[
            pltpu.VMEM((PAGE_BUFS, rows, page), f32),
            pltpu.VMEM((npg, N_HEADS, LANES), f32),
            pltpu.SemaphoreType.DMA((PAGE_BUFS,)),
        ],
    )
    return pl.pallas_call(
        functools.partial(_sample_attn_kernel, npg=npg, past=past),
        grid_spec=grid_spec,
        out_shape=jax.ShapeDtypeStruct((Bd, 1, D_ATTN), f32),
        compiler_params=pltpu.CompilerParams(
            dimension_semantics=("arbitrary",), vmem_limit_bytes=VMEM_LIMIT),
        name="sample_attention",
    )(page_table, pany, qbd, k_new_flat, v_new_flat, mask, mnew, rbt, ck_t, cv_t)


def _sample_merge_kernel(attn_ref, ga_ref, u_ref, gc_ref, st_ref, x_ref, wo_ref, dw_ref, dwb_ref,
                         lng_ref, lnb_ref, y_ref):
    acc = dwb_ref[...] + u_ref[...] * dw_ref[CONV_W - 1:CONV_W, :]
    for w in range(CONV_W - 1):
        acc = acc + st_ref[w] * dw_ref[w:w + 1, :]
    cg = (_layer_norm_swish(acc, lng_ref, lnb_ref) * gc_ref[...]).astype(bf16)
    mixa = (attn_ref[...] * ga_ref[...]).astype(bf16)
    y_ref[...] = (x_ref[...]
                  + jnp.dot(mixa, wo_ref[0:D_ATTN], preferred_element_type=f32)
                  + jnp.dot(cg, wo_ref[D_ATTN:D_ATTN + D_CONV], preferred_element_type=f32))


def _sample_merge(attn, ga, u, gc, state_t, x, wo, dw, dwb, lng, lnb):
    vm = pl.BlockSpec(memory_space=pltpu.VMEM)
    return pl.pallas_call(
        _sample_merge_kernel,
        in_specs=[vm] * 11,
        out_specs=vm,
        out_shape=jax.ShapeDtypeStruct(x.shape, f32),
        compiler_params=pltpu.CompilerParams(vmem_limit_bytes=VMEM_LIMIT),
        name="sample_merge",
    )(attn, ga, u, gc, state_t, x, wo, dw, dwb, lng, lnb)


def kernel(x_prompt, x_sample, cache_k, cache_v, cache_k_idx, state_conv, page_table, rel_bias, norm_g,
           w_in, q_norm_g, k_norm_g, dw_w, dw_b, ln_g, ln_b, w_out):
    depth = norm_g.shape[0]
    assert depth == 1, "single-layer step"
    B, S, D = x_prompt.shape
    Bd, T, _ = x_sample.shape
    assert T == 1
    n_phys, page = cache_k.shape[1], cache_k.shape[2]
    past = page_table.shape[1] * page

    w = w_in[0]
    c_ki = 2 * D_ATTN + 2 * N_HEADS * HEAD_DIM + N_IDX_HEADS * IDX_DIM
    c_conv = c_ki + IDX_DIM + N_IDX_HEADS
    w_groups = (w[:, :c_ki].astype(bf16),
                jnp.pad(w[:, c_ki:c_conv], ((0, 0), (0, LANES - IDX_DIM - N_IDX_HEADS))).astype(bf16),
                w[:, c_conv:].astype(bf16))
    assert [g.shape[1] for g in w_groups] == [_C_KW, _C_A - _C_KW, _C_END - _C_A]
    wo = w_out[0].astype(bf16)
    g = norm_g[0][None]
    qg = jnp.tile(q_norm_g[0], N_HEADS)[None]
    kg = jnp.tile(k_norm_g[0], N_HEADS)[None]
    hid = np.arange(D_ATTN) // HEAD_DIM
    bd = jnp.asarray((hid[:, None] == hid[None, :]) / HEAD_DIM, dtype=bf16)
    dwb, lng, lnb = dw_b[0][None], ln_g[0][None], ln_b[0][None]

    (qt, k_p, kb, v_p, vt, ga_p, qit, kw_p, wit, ki2, u_p, gc_p) = _inproj(
        x_prompt, g, w_groups, qg, kg, bd, prompt=True)
    btab = _bias_tables(rel_bias)
    mixa = _prompt_attention(qt, qit, wit, kb, ki2, vt, ga_p, btab)
    dw8 = jnp.broadcast_to(dw_w[0][:, None, :], (CONV_W, SUB, D_CONV))
    y_prompt = _prompt_merge(mixa, u_p, gc_p, x_prompt, wo, dw8, dwb, lng, lnb)

    xs = x_sample.reshape(1, Bd, D)
    q_s, k_s, v_s, ga_s, qi_s, kw_s, u_s, gc_s = (a[0] for a in _inproj(
        xs, g, w_groups, qg, kg, bd, prompt=False))
    ki_s = kw_s[:, :IDX_DIM]
    wrow = jnp.broadcast_to(kw_s[:, IDX_DIM:IDX_DIM + N_IDX_HEADS, None], (Bd, N_IDX_HEADS, LANES))
    cki_t = jnp.transpose(cache_k_idx[0], (0, 2, 1))
    ck_t = jnp.transpose(cache_k[0], (0, 2, 3, 1)).reshape(n_phys, D_ATTN, page)
    cv_t = jnp.transpose(cache_v[0], (0, 2, 3, 1)).reshape(n_phys, D_ATTN, page)
    scores = _sample_scores(page_table, qi_s.reshape(Bd, N_IDX_HEADS, IDX_DIM), wrow, cki_t)
    mask, mnew, pany = _sample_select(scores, qi_s, jnp.tile(ki_s, (1, N_IDX_HEADS)), kw_s, past=past)
    rbt = jnp.broadcast_to(rel_bias[:, :, None], (N_BUCKETS, N_HEADS, LANES))
    eye = jnp.asarray(np.arange(N_HEADS)[:, None] == hid[None, :], dtype=f32)
    qbd = q_s[:, None, :] * eye[None]
    attn_s = _sample_attention(page_table, pany[:, :, 0], qbd, k_s[:, None, :], v_s[:, None, :],
                               mask, mnew, rbt, ck_t, cv_t, past=past)
    state_t = jnp.transpose(state_conv[0], (1, 0, 2))
    y_sample = _sample_merge(attn_s[:, 0], ga_s, u_s, gc_s, state_t, x_sample[:, 0], wo,
                             dw_w[0], dwb, lng, lnb)

    heads = (N_HEADS, HEAD_DIM)
    return (
        y_prompt,
        y_sample[:, None],
        k_p.reshape((1, B, S) + heads),
        v_p.reshape((1, B, S) + heads),
        kw_p[None, :, :, :IDX_DIM],
        u_p[None, :, S - (CONV_W - 1):],
        k_s.reshape((1, Bd, 1) + heads),
        v_s.reshape((1, Bd, 1) + heads),
        ki_s.reshape(1, Bd, 1, IDX_DIM),
        jnp.concatenate([state_conv[0][:, 1:], u_s[:, None]], axis=1)[None],
    )
```

```python
import functools
import math

import numpy as np
import jax
import jax.numpy as jnp
from jax import lax
from jax.experimental import pallas as pl
from jax.experimental.pallas import tpu as pltpu

N_HEADS = 8
HEAD_DIM = 64
N_IDX_HEADS = 8
IDX_DIM = 64
D_ATTN = N_HEADS * HEAD_DIM
D_CONV = 512
CONV_W = 31
TOPK_MAX = 256
N_BUCKETS = 32
MAX_DISTANCE = 128
EPS = 1e-6
NEG = -1e30
INT_MIN = -2 ** 31
LOG2E = math.log2(math.e)

LANES = 128
Q_TILE = 256
K_CHUNK = 256
PV_ROWS = 144
VMEM_LIMIT = 56 * 1024 * 1024

f32 = jnp.float32
bf16 = jnp.bfloat16


def _bucket_thresholds():
    n = np.arange(0, 4 * MAX_DISTANCE)
    max_exact = N_BUCKETS // 2
    nf = np.maximum(n, 1).astype(np.float32)
    large = max_exact + (np.log(nf / np.float32(max_exact)) / np.float32(math.log(MAX_DISTANCE / max_exact))
                         * np.float32(N_BUCKETS - max_exact)).astype(np.int32)
    bucket = np.where(n < max_exact, n, np.minimum(large, N_BUCKETS - 1))
    return [int(n[bucket >= i].min()) for i in range(1, N_BUCKETS)]


BUCKET_THRESH = _bucket_thresholds()
assert BUCKET_THRESH[-1] <= K_CHUNK + 1 and K_CHUNK % Q_TILE == 0
N_PAR = K_CHUNK // Q_TILE


def _silu(x):
    return x * jax.nn.sigmoid(x)


def _to_key(x):
    b = pltpu.bitcast(x, jnp.int32)
    return jnp.where(b < 0, b ^ jnp.int32(0x7FFFFFFF), b)


_C_Q, _C_K, _C_V, _C_GA, _C_QI, _C_KW, _C_A, _C_B, _C_GC, _C_END = (
    0, 512, 1024, 1536, 2048, 2560, 2688, 3200, 3712, 4224)


def _inproj_kernel(x_ref, g_ref, wa_ref, wb_ref, wc_ref, qg_ref, kg_ref, bd_ref, *outs, prompt, tm):
    xf = x_ref[0]
    ms = jnp.mean(xf * xf, axis=-1, keepdims=True)
    xn = (xf * lax.rsqrt(ms + EPS) * g_ref[...]).astype(bf16)

    def proj(a, b):
        if b <= _C_KW:
            w = wa_ref[:, a:b]
        elif b <= _C_A:
            w = wb_ref[:, a - _C_KW:b - _C_KW]
        else:
            w = wc_ref[:, a - _C_A:b - _C_A]
        return jnp.dot(xn, w, preferred_element_type=f32)

    def head_norm(z, gain_ref):
        msq = jnp.dot((z * z).astype(bf16), bd_ref[...], preferred_element_type=f32)
        return z * lax.rsqrt(msq + EPS) * gain_ref[...]

    q = head_norm(proj(_C_Q, _C_K), qg_ref)
    k = head_norm(proj(_C_K, _C_V), kg_ref)
    v = proj(_C_V, _C_GA)
    ga = _silu(proj(_C_GA, _C_QI))
    qi = proj(_C_QI, _C_KW)
    kw = proj(_C_KW, _C_A)
    u = proj(_C_A, _C_B) * jax.nn.sigmoid(proj(_C_B, _C_GC))
    gc = _silu(proj(_C_GC, _C_END))

    if prompt:
        q_o, k_o, kb_o, v_o, vt_o, ga_o, qi_o, kw_o, wi_o, ki2_o, u_o, gc_o = outs
        q_o[0] = (q * (HEAD_DIM ** -0.5 * LOG2E)).T.astype(bf16)
        qi_o[0] = (qi * (IDX_DIM ** -0.5)).T.astype(bf16)
        wi_o[0] = kw.T[IDX_DIM:IDX_DIM + N_IDX_HEADS, :]
        kb_o[0] = k.astype(bf16)
        lane = lax.broadcasted_iota(jnp.int32, kw.shape, 1)
        ki2_o[0] = jnp.where(lane < IDX_DIM, kw, pltpu.roll(kw, IDX_DIM, 1)).astype(bf16)
        vt = v.T.astype(bf16)
        tail = jnp.where(lax.broadcasted_iota(jnp.int32, (PV_ROWS - 2 * HEAD_DIM, K_CHUNK), 0) < 8,
                         1.0, 0.0).astype(bf16)
        for c in range(tm // K_CHUNK):
            for i in range(N_HEADS // 2):
                blk = vt[i * 2 * HEAD_DIM:(i + 1) * 2 * HEAD_DIM, c * K_CHUNK:(c + 1) * K_CHUNK]
                vt_o[0, c, i] = jnp.concatenate([blk, tail], axis=0)
    else:
        q_o, k_o, v_o, ga_o, qi_o, kw_o, u_o, gc_o = outs
        q_o[0] = q
        qi_o[0] = qi
    k_o[0] = k
    v_o[0] = v
    ga_o[0] = ga
    kw_o[0] = kw
    u_o[0] = u
    gc_o[0] = gc


def _inproj(x, norm_g, w_groups, qg, kg, bd, *, prompt):
    B, S, D = x.shape
    tm = 512 if prompt else S
    assert S % tm == 0 and (not prompt or tm % K_CHUNK == 0)
    grid = (B, S // tm)
    row = lambda b, i: (b, i, 0)
    const2 = lambda b, i: (0, 0)
    col = lambda b, i: (b, 0, i)
    nck = tm // K_CHUNK
    if prompt:
        out_shape = (
            jax.ShapeDtypeStruct((B, D_ATTN, S), bf16),
            jax.ShapeDtypeStruct((B, S, D_ATTN), f32),
            jax.ShapeDtypeStruct((B, S, D_ATTN), bf16),
            jax.ShapeDtypeStruct((B, S, D_ATTN), f32),
            jax.ShapeDtypeStruct((B, S // K_CHUNK, N_HEADS // 2, PV_ROWS, K_CHUNK), bf16),
            jax.ShapeDtypeStruct((B, S, D_ATTN), f32),
            jax.ShapeDtypeStruct((B, N_IDX_HEADS * IDX_DIM, S), bf16),
            jax.ShapeDtypeStruct((B, S, LANES), f32),
            jax.ShapeDtypeStruct((B, N_IDX_HEADS, S), f32),
            jax.ShapeDtypeStruct((B, S, LANES), bf16),
            jax.ShapeDtypeStruct((B, S, D_CONV), f32),
            jax.ShapeDtypeStruct((B, S, D_CONV), f32),
        )
        out_specs = (
            pl.BlockSpec((1, D_ATTN, tm), col),
            pl.BlockSpec((1, tm, D_ATTN), row),
            pl.BlockSpec((1, tm, D_ATTN), row),
            pl.BlockSpec((1, tm, D_ATTN), row),
            pl.BlockSpec((1, nck, N_HEADS // 2, PV_ROWS, K_CHUNK), lambda b, i: (b, i, 0, 0, 0)),
            pl.BlockSpec((1, tm, D_ATTN), row),
            pl.BlockSpec((1, N_IDX_HEADS * IDX_DIM, tm), col),
            pl.BlockSpec((1, tm, LANES), row),
            pl.BlockSpec((1, N_IDX_HEADS, tm), col),
            pl.BlockSpec((1, tm, LANES), row),
            pl.BlockSpec((1, tm, D_CONV), row),
            pl.BlockSpec((1, tm, D_CONV), row),
        )
    else:
        out_shape = (
            jax.ShapeDtypeStruct((B, S, D_ATTN), f32),
            jax.ShapeDtypeStruct((B, S, D_ATTN), f32),
            jax.ShapeDtypeStruct((B, S, D_ATTN), f32),
            jax.ShapeDtypeStruct((B, S, D_ATTN), f32),
            jax.ShapeDtypeStruct((B, S, D_ATTN), f32),
            jax.ShapeDtypeStruct((B, S, LANES), f32),
            jax.ShapeDtypeStruct((B, S, D_CONV), f32),
            jax.ShapeDtypeStruct((B, S, D_CONV), f32),
        )
        out_specs = tuple(pl.BlockSpec((1, tm, s.shape[-1]), row) for s in out_shape)
    return pl.pallas_call(
        functools.partial(_inproj_kernel, prompt=prompt, tm=tm),
        grid=grid,
        in_specs=[
            pl.BlockSpec((1, tm, D), row),
            pl.BlockSpec((1, D), const2),
            pl.BlockSpec((D, _C_KW), const2),
            pl.BlockSpec((D, _C_A - _C_KW), const2),
            pl.BlockSpec((D, _C_END - _C_A), const2),
            pl.BlockSpec((1, D_ATTN), const2),
            pl.BlockSpec((1, D_ATTN), const2),
            pl.BlockSpec((D_ATTN, D_ATTN), const2),
        ],
        out_specs=out_specs,
        out_shape=out_shape,
        compiler_params=pltpu.CompilerParams(
            dimension_semantics=("parallel", "parallel"), vmem_limit_bytes=VMEM_LIMIT),
        name="inproj_prompt" if prompt else "inproj_sample",
    )(x, norm_g, *w_groups, qg, kg, bd)


def _bias_table_kernel(rb_ref, out_ref):
    shape = (K_CHUNK, Q_TILE)
    krow = lax.broadcasted_iota(jnp.int32, shape, 0)
    qcol = lax.broadcasted_iota(jnp.int32, shape, 1)
    for par in range(N_PAR):
        for which in range(2):
            dist = Q_TILE * par + K_CHUNK * (1 - which) + qcol - krow
            for h in range(N_HEADS):
                far = rb_ref[N_BUCKETS - 1, h]
                acc = jnp.full(shape, (rb_ref[0, h] - far) * LOG2E, f32)
                for i, t in enumerate(BUCKET_THRESH):
                    acc = jnp.where(dist >= t, (rb_ref[i + 1, h] - far) * LOG2E, acc)
                out_ref[par, which, h // 2, :, (h % 2) * Q_TILE:(h % 2 + 1) * Q_TILE] = (
                    jnp.where(dist >= 0, acc, NEG))


def _bias_tables(rel_bias):
    return pl.pallas_call(
        _bias_table_kernel,
        in_specs=[pl.BlockSpec(memory_space=pltpu.SMEM)],
        out_specs=pl.BlockSpec(memory_space=pltpu.VMEM),
        out_shape=jax.ShapeDtypeStruct((N_PAR, 2, N_HEADS // 2, K_CHUNK, 2 * Q_TILE), f32),
        name="bias_tables",
    )(rel_bias)


N_PAIRS = N_HEADS // 2
SUB = 8


COUNT_ROWS = 32
MAX_ROWS = 16


def _bit_planes(words):
    a = list(words)
    j, m = 16, 0x0000FFFF
    while j:
        k = 0
        while k < 32:
            t = (a[k] ^ lax.shift_right_logical(a[k + j], jnp.int32(j))) & jnp.int32(m)
            a[k] = a[k] ^ t
            a[k + j] = a[k + j] ^ jnp.left_shift(t, jnp.int32(j))
            k = (k + j + 1) & ~j
        j >>= 1
        m = (m ^ (m << j)) & 0xFFFFFFFF
    return a


def _fold(x, op, rows):
    return op(x.reshape(x.shape[0] // rows, rows, x.shape[1]), axis=0)


def _attn_kernel(qt_ref, qit_ref, wit_ref, k_ref, ki_ref, vt_ref, ga_ref, bt_ref, out_ref,
                 sc_ref, key_ref, pl_ref, lg_ref, qbd_ref, qibd_ref, o_ref, *, ksel, idx_bits):
    j = pl.program_id(1)
    n_chunks = sc_ref.shape[0]
    nck = j // N_PAR + 1
    par = j % N_PAR
    shape = (K_CHUNK, Q_TILE)
    key_pos = lax.broadcasted_iota(jnp.int32, shape, 0)
    q_pos = j * Q_TILE + lax.broadcasted_iota(jnp.int32, shape, 1)

    top = lax.broadcasted_iota(jnp.int32, (2 * HEAD_DIM, Q_TILE), 0) < HEAD_DIM
    for i in range(N_PAIRS):
        rows = slice(i * 2 * HEAD_DIM, (i + 1) * 2 * HEAD_DIM)
        for src, dst in ((qt_ref, qbd_ref), (qit_ref, qibd_ref)):
            x = src[0, rows, :]
            zero = jnp.zeros_like(x)
            dst[i] = jnp.concatenate([jnp.where(top, x, zero), jnp.where(top, zero, x)], axis=1)
    wi = wit_ref[0] * (N_IDX_HEADS ** -0.5)
    wpair = [jnp.concatenate([wi[2 * i:2 * i + 1], wi[2 * i + 1:2 * i + 2]], axis=1)
             for i in range(N_PAIRS)]

    def chunk_rows(c):
        return pl.ds(pl.multiple_of(c * K_CHUNK, K_CHUNK), K_CHUNK)

    def for_chunk_pairs(n, body):
        def pair(cc, carry):
            body([2 * cc, 2 * cc + 1])
            return carry
        lax.fori_loop(0, n // 2, pair, 0)

        @pl.when(n % 2 == 1)
        def _():
            body([n - 1])

    def score_chunks(chunks):
        for c in chunks:
            ki2 = ki_ref[0, chunk_rows(c), :]
            acc = jnp.zeros(shape, f32)
            for i in range(N_PAIRS):
                r = jnp.maximum(jnp.dot(ki2, qibd_ref[i], preferred_element_type=f32), 0.0) * wpair[i]
                acc = acc + r[:, :Q_TILE] + r[:, Q_TILE:]
            keys = _to_key(jnp.where(c * K_CHUNK + key_pos <= q_pos, acc, -jnp.inf))
            key_ref[c] = keys
            ukeys = keys ^ jnp.int32(INT_MIN)
            planes = _bit_planes([ukeys[t * SUB:(t + 1) * SUB, :] for t in range(32)])
            for p in range(32):
                pl_ref[c, p] = planes[p]

    for_chunk_pairs(nck, score_chunks)

    def count(pred):
        def body(c, cnt):
            hit = pred(c)
            for g in range(K_CHUNK // COUNT_ROWS):
                rows = slice(g * COUNT_ROWS, (g + 1) * COUNT_ROWS)
                cnt = jnp.where(hit[rows], cnt + 1.0, cnt)
            return cnt
        cnt = lax.fori_loop(0, nck, body, jnp.zeros((COUNT_ROWS, Q_TILE), f32))
        return jnp.sum(cnt, axis=0, keepdims=True)

    def write_mask(pred):
        def body(c, carry):
            sc_ref[c] = jnp.where(pred(c), 0.0, NEG)
            return carry
        lax.fori_loop(0, nck, body, 0)

    need_search = (j + 1) * Q_TILE > ksel

    @pl.when(jnp.logical_not(need_search))
    def _():
        write_mask(lambda c: c * K_CHUNK + key_pos < (j + 1) * Q_TILE)

    @pl.when(need_search)
    def _():
        def clear_body(c, carry):
            pl_ref[c] = jnp.zeros(pl_ref.shape[1:], jnp.int32)
            return carry
        lax.fori_loop(nck, n_chunks, clear_body, 0)

        def bit_body(p, carry):
            n_above, thr_u, alive = carry
            ones = [alive[c] & pl_ref[c, p] for c in range(n_chunks)]
            acc = lax.population_count(ones[0])
            for c in range(1, n_chunks):
                acc = acc + lax.population_count(ones[c])
            n_hi = n_above + jnp.sum(acc.astype(f32), axis=0, keepdims=True)
            take = n_hi >= ksel
            takeb = jnp.broadcast_to(take, (SUB, Q_TILE))
            alive = tuple(jnp.where(takeb, ones[c], alive[c] ^ ones[c]) for c in range(n_chunks))
            bit = jnp.left_shift(jnp.int32(1), 31 - p)
            return jnp.where(take, n_above, n_hi), jnp.where(take, thr_u | bit, thr_u), alive

        everything = jnp.full((SUB, Q_TILE), -1, jnp.int32)
        n_above, thr_u, alive = lax.fori_loop(
            0, 32, bit_body,
            (jnp.zeros((1, Q_TILE), f32), jnp.zeros((1, Q_TILE), jnp.int32),
             tuple(jnp.where(c < nck, everything, 0) for c in range(n_chunks))))
        n_eq = lax.population_count(alive[0])
        for c in range(1, n_chunks):
            n_eq = n_eq + lax.population_count(alive[c])
        n_ge = n_above + jnp.sum(n_eq.astype(f32), axis=0, keepdims=True)
        thr = thr_u ^ jnp.int32(INT_MIN)
        thrb = jnp.broadcast_to(thr, shape)
        has_tie = jnp.max(n_ge) > ksel

        @pl.when(jnp.logical_not(has_tie))
        def _():
            write_mask(lambda c: key_ref[c] >= thrb)

        @pl.when(has_tie)
        def _():
            need = ksel - count(lambda c: key_ref[c] > thrb)

            def cut_body(i, x):
                cand = x + jnp.left_shift(jnp.int32(1), idx_bits - 1 - i)
                candb = jnp.broadcast_to(cand, shape)
                g = count(lambda c: jnp.where(key_ref[c] == thrb, c * K_CHUNK + key_pos, 2 ** 30) < candb)
                return jnp.where(g < need, cand, x)

            cut = lax.fori_loop(0, idx_bits, cut_body, jnp.zeros((1, Q_TILE), jnp.int32))
            cutb = jnp.broadcast_to(cut, shape)

            def sel(c):
                kc = key_ref[c]
                eq_idx = jnp.where(kc == thrb, c * K_CHUNK + key_pos, 2 ** 30)
                return jnp.where(kc > thrb, 0, eq_idx) <= cutb
            write_mask(sel)

    def logits_chunk(c, which, ms):
        mask = sc_ref[c]
        mask2 = jnp.concatenate([mask, mask], axis=1)
        out = []
        for i in range(N_PAIRS):
            kc = k_ref[0, chunk_rows(c), i * 2 * HEAD_DIM:(i + 1) * 2 * HEAD_DIM]
            s = jnp.dot(kc, qbd_ref[i], preferred_element_type=f32) + mask2
            if which is not None:
                s = s + bt_ref[par, which, i]
            lg_ref[i, c] = s
            out.append(jnp.maximum(ms[i], _fold(s, jnp.max, MAX_ROWS)))
        return tuple(out)

    ms = tuple(jnp.full((MAX_ROWS, 2 * Q_TILE), NEG, f32) for _ in range(N_PAIRS))
    nfar = jnp.maximum(nck - 2, 0)
    ms = lax.fori_loop(
        0, nfar // 2, lambda cc, ms: logits_chunk(2 * cc + 1, None, logits_chunk(2 * cc, None, ms)), ms)
    ms = lax.cond(nfar % 2 == 1, lambda ms: logits_chunk(nfar - 1, None, ms), lambda ms: ms, ms)
    ms = lax.cond(nck >= 2,
                  lambda ms: logits_chunk(nck - 1, 1, logits_chunk(nck - 2, 0, ms)),
                  lambda ms: logits_chunk(nck - 1, 1, ms), ms)
    mrow = [jnp.max(m, axis=0, keepdims=True) for m in ms]

    o_ref[...] = jnp.zeros(o_ref.shape, f32)

    def pv_chunks(chunks):
        for i in range(N_PAIRS):
            acc = None
            for c in chunks:
                p = jnp.exp2((lg_ref[i, c] - mrow[i]).astype(bf16))
                d = jnp.dot(vt_ref[0, c, i], p, preferred_element_type=f32)
                acc = d if acc is None else acc + d
            o_ref[i] += acc
    for_chunk_pairs(nck, pv_chunks)

    pieces = []
    for i in range(N_PAIRS):
        o = o_ref[i]
        l = o[2 * HEAD_DIM:2 * HEAD_DIM + 1, :]
        pieces.append(o[0:HEAD_DIM, 0:Q_TILE] / l[:, 0:Q_TILE])
        pieces.append(o[HEAD_DIM:2 * HEAD_DIM, Q_TILE:] / l[:, Q_TILE:])
    out_ref[0] = (jnp.concatenate(pieces, axis=0).T * ga_ref[0]).astype(bf16)


def _prompt_attention(qt, qit, wit, kb, ki2, vt, ga, btab):
    B, S, _ = kb.shape
    assert S % K_CHUNK == 0
    nq = S // Q_TILE
    nc = S // K_CHUNK
    ksel = min(TOPK_MAX, S // 4)
    qcol = lambda b, j: (b, 0, j)
    perb3 = lambda b, j: (b, 0, 0)
    return pl.pallas_call(
        functools.partial(_attn_kernel, ksel=ksel, idx_bits=max(1, (S - 1).bit_length())),
        grid=(B, nq),
        in_specs=[
            pl.BlockSpec((1, D_ATTN, Q_TILE), qcol),
            pl.BlockSpec((1, N_IDX_HEADS * IDX_DIM, Q_TILE), qcol),
            pl.BlockSpec((1, N_IDX_HEADS, Q_TILE), qcol),
            pl.BlockSpec((1, S, D_ATTN), perb3),
            pl.BlockSpec((1, S, LANES), perb3),
            pl.BlockSpec((1, nc, N_PAIRS, PV_ROWS, K_CHUNK), lambda b, j: (b, 0, 0, 0, 0)),
            pl.BlockSpec((1, Q_TILE, D_ATTN), lambda b, j: (b, j, 0)),
            pl.BlockSpec((N_PAR, 2, N_PAIRS, K_CHUNK, 2 * Q_TILE), lambda b, j: (0, 0, 0, 0, 0)),
        ],
        out_specs=pl.BlockSpec((1, Q_TILE, D_ATTN), lambda b, j: (b, j, 0)),
        out_shape=jax.ShapeDtypeStruct((B, S, D_ATTN), bf16),
        scratch_shapes=[
            pltpu.VMEM((nc, K_CHUNK, Q_TILE), f32),
            pltpu.VMEM((nc, K_CHUNK, Q_TILE), jnp.int32),
            pltpu.VMEM((nc, 32, SUB, Q_TILE), jnp.int32),
            pltpu.VMEM((N_PAIRS, nc, K_CHUNK, 2 * Q_TILE), f32),
            pltpu.VMEM((N_PAIRS, 2 * HEAD_DIM, 2 * Q_TILE), bf16),
            pltpu.VMEM((N_PAIRS, 2 * IDX_DIM, 2 * Q_TILE), bf16),
            pltpu.VMEM((N_PAIRS, PV_ROWS, 2 * Q_TILE), f32),
        ],
        compiler_params=pltpu.CompilerParams(
            dimension_semantics=("parallel", "arbitrary"), vmem_limit_bytes=VMEM_LIMIT),
        name="prompt_attention",
    )(qt, qit, wit, kb, ki2, vt, ga, btab)


CONV_HALO = 32
CONV_ROWS = 32


def _layer_norm_swish(c, lng_ref, lnb_ref):
    mu = jnp.mean(c, axis=-1, keepdims=True)
    d = c - mu
    var = jnp.mean(d * d, axis=-1, keepdims=True)
    return _silu(d * lax.rsqrt(var + EPS) * lng_ref[...] + lnb_ref[...])


def _merge_kernel(mixa_ref, ucur_ref, uprev_ref, gc_ref, x_ref, wo_ref, dw_ref, dwb_ref, lng_ref,
                  lnb_ref, y_ref, ext_ref, sh_ref, cg_ref, *, tq):
    i = pl.program_id(1)
    n = CONV_HALO + tq
    ext_ref[0:CONV_HALO] = jnp.where(i > 0, uprev_ref[0], 0.0)
    ext_ref[CONV_HALO:n] = ucur_ref[0]
    ext_ref[n:n + SUB] = jnp.zeros((SUB, D_CONV), f32)
    ext = ext_ref[...]
    for r in range(1, SUB):
        sh_ref[r - 1] = pltpu.roll(ext, n + SUB - r, 0)[0:n, :]
    first = CONV_HALO - (CONV_W - 1)
    for c in range(tq // CONV_ROWS):
        acc = jnp.broadcast_to(dwb_ref[...], (CONV_ROWS, D_CONV))
        for w in range(CONV_W):
            a, r = divmod(first + w, SUB)
            start = c * CONV_ROWS + a * SUB
            src = ext_ref if r == 0 else sh_ref.at[r - 1]
            acc = acc + src[start:start + CONV_ROWS, :] * jnp.tile(dw_ref[w], (CONV_ROWS // SUB, 1))
        rows = slice(c * CONV_ROWS, (c + 1) * CONV_ROWS)
        cg_ref[rows] = (_layer_norm_swish(acc, lng_ref, lnb_ref) * gc_ref[0, rows]).astype(bf16)
    y_ref[0] = (x_ref[0]
                + jnp.dot(mixa_ref[0], wo_ref[0:D_ATTN], preferred_element_type=f32)
                + jnp.dot(cg_ref[...], wo_ref[D_ATTN:D_ATTN + D_CONV], preferred_element_type=f32))


def _prompt_merge(mixa, u, gc, x, wo, dw, dwb, lng, lnb):
    B, S, D = x.shape
    tq = 512
    assert S % tq == 0 and tq % CONV_HALO == 0
    row = lambda b, i: (b, i, 0)
    const2 = lambda b, i: (0, 0)
    halo = lambda b, i: (b, jnp.maximum(i * (tq // CONV_HALO) - 1, 0), 0)
    return pl.pallas_call(
        functools.partial(_merge_kernel, tq=tq),
        grid=(B, S // tq),
        in_specs=[
            pl.BlockSpec((1, tq, D_ATTN), row),
            pl.BlockSpec((1, tq, D_CONV), row),
            pl.BlockSpec((1, CONV_HALO, D_CONV), halo),
            pl.BlockSpec((1, tq, D_CONV), row),
            pl.BlockSpec((1, tq, D), row),
            pl.BlockSpec((D_ATTN + D_CONV, D), const2),
            pl.BlockSpec((CONV_W, SUB, D_CONV), lambda b, i: (0, 0, 0)),
            pl.BlockSpec((1, D_CONV), const2),
            pl.BlockSpec((1, D_CONV), const2),
            pl.BlockSpec((1, D_CONV), const2),
        ],
        out_specs=pl.BlockSpec((1, tq, D), row),
        out_shape=jax.ShapeDtypeStruct((B, S, D), f32),
        scratch_shapes=[
            pltpu.VMEM((CONV_HALO + tq + SUB, D_CONV), f32),
            pltpu.VMEM((SUB - 1, CONV_HALO + tq, D_CONV), f32),
            pltpu.VMEM((tq, D_CONV), bf16),
        ],
        compiler_params=pltpu.CompilerParams(
            dimension_semantics=("parallel", "arbitrary"), vmem_limit_bytes=VMEM_LIMIT),
        name="prompt_merge",
    )(mixa, u, u, gc, x, wo, dw, dwb, lng, lnb)


SCORE_UNROLL = 16


def _sample_score_kernel(pt_ref, qi_ref, w_ref, cki_ref, out_ref, buf_ref, sem_ref, *, npg):
    b = pl.program_id(0)
    nb = pl.num_programs(0)
    slot = b % 2

    def page_copy(bb, p, sl):
        return pltpu.make_async_copy(cki_ref.at[pt_ref[bb, p]], buf_ref.at[sl, p], sem_ref.at[sl])

    def start_all(bb, sl):
        def body(pp, carry):
            page_copy(bb, 2 * pp, sl).start(priority=0)
            page_copy(bb, 2 * pp + 1, sl).start(priority=1)
            return carry
        lax.fori_loop(0, npg // 2, body, 0)

    @pl.when(b == 0)
    def _():
        start_all(0, 0)

    @pl.when(b + 1 < nb)
    def _():
        start_all(b + 1, 1 - slot)

    def wait_body(p, carry):
        page_copy(b, p, slot).wait()
        return carry
    lax.fori_loop(0, npg, wait_body, 0)

    qs = (qi_ref[0] * (IDX_DIM ** -0.5)).astype(bf16)
    wcol = w_ref[0] * (N_IDX_HEADS ** -0.5)
    wcol2 = jnp.concatenate([wcol, wcol], axis=1)

    def body(i, carry):
        for u in range(0, SCORE_UNROLL, 2):
            p = i * SCORE_UNROLL + u
            pages = jnp.concatenate([buf_ref[slot, p], buf_ref[slot, p + 1]], axis=1).astype(bf16)
            s = jnp.dot(qs, pages, preferred_element_type=f32)
            r = jnp.sum(jnp.maximum(s, 0.0) * wcol2, axis=0, keepdims=True)
            out_ref[0, pl.ds(p, 1), :] = r[:, :LANES]
            out_ref[0, pl.ds(p + 1, 1), :] = r[:, LANES:]
        return carry
    lax.fori_loop(0, npg // SCORE_UNROLL, body, 0)


def _sample_scores(page_table, qi3, wrow, cki_t):
    Bd, npg = page_table.shape
    page = cki_t.shape[2]
    assert page == LANES and npg % SCORE_UNROLL == 0
    grid_spec = pltpu.PrefetchScalarGridSpec(
        num_scalar_prefetch=1,
        grid=(Bd,),
        in_specs=[
            pl.BlockSpec((1, N_IDX_HEADS, IDX_DIM), lambda b, pt: (b, 0, 0)),
            pl.BlockSpec((1, N_IDX_HEADS, LANES), lambda b, pt: (b, 0, 0)),
            pl.BlockSpec(memory_space=pl.ANY),
        ],
        out_specs=pl.BlockSpec((1, npg, page), lambda b, pt: (b, 0, 0)),
        scratch_shapes=[
            pltpu.VMEM((2, npg, IDX_DIM, page), f32),
            pltpu.SemaphoreType.DMA((2,)),
        ],
    )
    return pl.pallas_call(
        functools.partial(_sample_score_kernel, npg=npg),
        grid_spec=grid_spec,
        out_shape=jax.ShapeDtypeStruct((Bd, npg, page), f32),
        compiler_params=pltpu.CompilerParams(
            dimension_semantics=("arbitrary",), vmem_limit_bytes=VMEM_LIMIT),
        name="sample_scores",
    )(page_table, qi3, wrow, cki_t)


def _sample_select_kernel(sc_ref, qi_ref, kit_ref, kw_ref, mask_ref, mnew_ref, pany_ref, key_ref, *,
                          npg, ksel, past, idx_bits):
    Bd = sc_ref.shape[0]
    shape = (Bd, npg, LANES)

    qb = (qi_ref[...] * (IDX_DIM ** -0.5)).astype(bf16).astype(f32)
    kb = kit_ref[...].astype(bf16).astype(f32)
    prod = qb * kb
    grp = lax.broadcasted_iota(jnp.int32, prod.shape, 1) // IDX_DIM
    kwv = kw_ref[...]
    s_new = jnp.zeros((Bd, 1), f32)
    for h in range(N_IDX_HEADS):
        sh = jnp.sum(jnp.where(grp == h, prod, 0.0), axis=1, keepdims=True)
        s_new = s_new + jnp.maximum(sh, 0.0) * (kwv[:, IDX_DIM + h:IDX_DIM + h + 1] * (N_IDX_HEADS ** -0.5))
    key_new = _to_key(s_new).reshape(Bd, 1, 1)

    key_ref[...] = _to_key(sc_ref[...])
    pos = (lax.broadcasted_iota(jnp.int32, shape, 1) * LANES
           + lax.broadcasted_iota(jnp.int32, shape, 2))

    def count(pred, pred_new):
        x = jnp.where(pred, 1.0, 0.0)
        s = jnp.sum(jnp.sum(x, axis=1, keepdims=True), axis=2, keepdims=True)
        return s + jnp.where(pred_new, 1.0, 0.0)

    def bit_body(i, lo):
        cand = lo + jnp.left_shift(jnp.int32(1), 31 - i)
        tot = count(key_ref[...] >= cand, key_new >= cand)
        return jnp.where(tot >= ksel, cand, lo)

    thr = lax.fori_loop(0, 32, bit_body, jnp.full((Bd, 1, 1), INT_MIN, jnp.int32))
    keys = key_ref[...]
    need = ksel - count(keys > thr, key_new > thr)
    big = jnp.int32(2 ** 30)
    eq_pos = jnp.where(keys == thr, pos, big)
    eq_pos_new = jnp.where(key_new == thr, jnp.int32(past), big)

    def cut_body(i, x):
        cand = x + jnp.left_shift(jnp.int32(1), idx_bits - 1 - i)
        g = count(eq_pos < cand, eq_pos_new < cand)
        return jnp.where(g < need, cand, x)

    cut = lax.fori_loop(0, idx_bits, cut_body, jnp.zeros((Bd, 1, 1), jnp.int32))
    sel = jnp.where(keys > thr, 0, eq_pos) <= cut
    mask_ref[...] = jnp.where(sel, 0.0, NEG)
    pany_ref[...] = jnp.max(jnp.where(sel, 1.0, 0.0), axis=2, keepdims=True).astype(jnp.int32)
    sel_new = jnp.where(key_new > thr, 0, eq_pos_new) <= cut
    mnew_ref[...] = jnp.broadcast_to(jnp.where(sel_new, 0.0, NEG), mnew_ref.shape)


def _sample_select(scores, qi, ki_tiled, kw, *, past):
    Bd, npg, _ = scores.shape
    ksel = min(TOPK_MAX, (past + 1) // 4)
    vm = pl.BlockSpec(memory_space=pltpu.VMEM)
    return pl.pallas_call(
        functools.partial(_sample_select_kernel, npg=npg, ksel=ksel, past=past,
                          idx_bits=past.bit_length()),
        in_specs=[vm, vm, vm, vm],
        out_specs=(vm, vm, vm),
        out_shape=(jax.ShapeDtypeStruct((Bd, npg, LANES), f32),
                   jax.ShapeDtypeStruct((Bd, N_HEADS, LANES), f32),
                   jax.ShapeDtypeStruct((Bd, npg, 1), jnp.int32)),
        scratch_shapes=[pltpu.VMEM((Bd, npg, LANES), jnp.int32)],
        compiler_params=pltpu.CompilerParams(vmem_limit_bytes=VMEM_LIMIT),
        name="sample_select",
    )(scores, qi, ki_tiled, kw)


PAGE_BUFS = 32
PAGE_GROUP = 4


def _sample_attn_kernel(pt_ref, pany_ref, qbd_ref, knew_ref, vnew_ref, mask_ref, mnew_ref, rbt_ref, ck_ref, cv_ref,
                        out_ref, buf_ref, lg_ref, sem_ref, *, npg, past):
    b = pl.program_id(0)
    nb = pl.num_programs(0)
    per_b = 2 * npg
    total = nb * per_b

    def page_copy(g, src_ref):
        phys = pt_ref[g // per_b, g % npg]
        slot = g % PAGE_BUFS
        return pltpu.make_async_copy(src_ref.at[phys], buf_ref.at[slot], sem_ref.at[slot])

    def needed(g):
        return pany_ref[g // per_b, g % npg] != 0

    def start(g):
        is_k = g % per_b < npg

        @pl.when(jnp.logical_and(needed(g), is_k))
        def _():
            page_copy(g, ck_ref).start()

        @pl.when(jnp.logical_and(needed(g), jnp.logical_not(is_k)))
        def _():
            page_copy(g, cv_ref).start()

    def wait(g):
        @pl.when(needed(g))
        def _():
            page_copy(g, ck_ref).wait()

    @pl.when(b == 0)
    def _():
        for slot in range(PAGE_BUFS):
            buf_ref[slot] = jnp.zeros(buf_ref.shape[1:], f32)
        for g in range(PAGE_BUFS):
            start(g)

    g0 = b * per_b
    qbd = (qbd_ref[0] * (HEAD_DIM ** -0.5 * LOG2E)).astype(bf16)
    blk = (lax.broadcasted_iota(jnp.int32, (N_HEADS, D_ATTN), 0)
           == lax.broadcasted_iota(jnp.int32, (N_HEADS, D_ATTN), 1) // HEAD_DIM)
    tok = lax.broadcasted_iota(jnp.int32, (N_HEADS, LANES), 1)

    def bias_of(dist):
        far = rbt_ref[N_BUCKETS - 1]
        acc = jnp.broadcast_to(rbt_ref[0] - far, dist.shape)
        for i, t in enumerate(BUCKET_THRESH):
            acc = jnp.where(dist >= t, rbt_ref[i + 1] - far, acc)
        return acc * LOG2E

    def refill(gs):
        for g in gs:
            @pl.when(g + PAGE_BUFS < total)
            def _():
                start(g + PAGE_BUFS)

    def k_body(it, m):
        gs = [g0 + it * PAGE_GROUP + u for u in range(PAGE_GROUP)]
        for g in gs:
            wait(g)
        for u, g in enumerate(gs):
            p = it * PAGE_GROUP + u
            kp = buf_ref[g % PAGE_BUFS].astype(bf16)
            s = jnp.dot(qbd, kp, preferred_element_type=f32) + mask_ref[0, pl.ds(p, 1), :]
            lg_ref[p] = s
            m = jnp.maximum(m, s)
        refill(gs)
        return m

    m = lax.fori_loop(0, npg // PAGE_GROUP, k_body, jnp.full((N_HEADS, LANES), NEG, f32))
    last = lg_ref[npg - 1] + bias_of(past - ((npg - 1) * LANES + tok))
    lg_ref[npg - 1] = last
    m = jnp.maximum(m, last)

    s_new = (jnp.sum(qbd_ref[0] * knew_ref[0], axis=1, keepdims=True) * (HEAD_DIM ** -0.5 * LOG2E)
             + bias_of(jnp.zeros((N_HEADS, LANES), jnp.int32))[:, 0:1] + mnew_ref[0][:, 0:1])
    m_row = jnp.maximum(jnp.max(m, axis=1, keepdims=True), s_new)
    mb = jnp.broadcast_to(m_row, (N_HEADS, LANES))

    def v_body(it, carry):
        acc, l = carry
        gs = [g0 + npg + it * PAGE_GROUP + u for u in range(PAGE_GROUP)]
        for g in gs:
            wait(g)
        for u, g in enumerate(gs):
            vp = buf_ref[g % PAGE_BUFS].astype(bf16)
            pr = jnp.exp2(lg_ref[it * PAGE_GROUP + u] - mb)
            acc = acc + lax.dot_general(pr.astype(bf16), vp, (((1,), (1,)), ((), ())),
                                        preferred_element_type=f32)
            l = l + pr
        refill(gs)
        return acc, l

    acc, l = lax.fori_loop(0, npg // PAGE_GROUP, v_body, (jnp.zeros((N_HEADS, D_ATTN), f32),
                                                          jnp.zeros((N_HEADS, LANES), f32)))
    p_new = jnp.exp2(s_new - m_row)
    l_tot = jnp.sum(l, axis=1, keepdims=True) + p_new
    num = jnp.where(blk, acc + p_new * vnew_ref[0], 0.0) / l_tot
    out_ref[0] = jnp.sum(num, axis=0, keepdims=True)


def _sample_attention(page_table, pany, qbd, k_new_flat, v_new_flat, mask, mnew, rbt, ck_t, cv_t, *, past):
    Bd, npg = page_table.shape
    rows, page = ck_t.shape[1], ck_t.shape[2]
    assert rows == D_ATTN and page == LANES
    assert npg % PAGE_GROUP == 0 and PAGE_BUFS % PAGE_GROUP == 0 and 2 * npg >= PAGE_BUFS
    grid_spec = pltpu.PrefetchScalarGridSpec(
        num_scalar_prefetch=2,
        grid=(Bd,),
        in_specs=[
            pl.BlockSpec((1, N_HEADS, D_ATTN), lambda b, pt, pa: (b, 0, 0)),
            pl.BlockSpec((1, 1, D_ATTN), lambda b, pt, pa: (b, 0, 0)),
            pl.BlockSpec((1, 1, D_ATTN), lambda b, pt, pa: (b, 0, 0)),
            pl.BlockSpec((1, npg, LANES), lambda b, pt, pa: (b, 0, 0)),
            pl.BlockSpec((1, N_HEADS, LANES), lambda b, pt, pa: (b, 0, 0)),
            pl.BlockSpec((N_BUCKETS, N_HEADS, LANES), lambda b, pt, pa: (0, 0, 0)),
            pl.BlockSpec(memory_space=pl.ANY),
            pl.BlockSpec(memory_space=pl.ANY),
        ],
        out_specs=pl.BlockSpec((1, 1, D_ATTN), lambda b, pt, pa: (b, 0, 0)),
        scratch_shapes=[
            pltpu.VMEM((PAGE_BUFS, rows, page), f32),
            pltpu.VMEM((npg, N_HEADS, LANES), f32),
            pltpu.SemaphoreType.DMA((PAGE_BUFS,)),
        ],
    )
    return pl.pallas_call(
        functools.partial(_sample_attn_kernel, npg=npg, past=past),
        grid_spec=grid_spec,
        out_shape=jax.ShapeDtypeStruct((Bd, 1, D_ATTN), f32),
        compiler_params=pltpu.CompilerParams(
            dimension_semantics=("arbitrary",), vmem_limit_bytes=VMEM_LIMIT),
        name="sample_attention",
    )(page_table, pany, qbd, k_new_flat, v_new_flat, mask, mnew, rbt, ck_t, cv_t)


def _sample_merge_kernel(attn_ref, ga_ref, u_ref, gc_ref, st_ref, x_ref, wo_ref, dw_ref, dwb_ref,
                         lng_ref, lnb_ref, y_ref):
    acc = dwb_ref[...] + u_ref[...] * dw_ref[CONV_W - 1:CONV_W, :]
    for w in range(CONV_W - 1):
        acc = acc + st_ref[w] * dw_ref[w:w + 1, :]
    cg = (_layer_norm_swish(acc, lng_ref, lnb_ref) * gc_ref[...]).astype(bf16)
    mixa = (attn_ref[...] * ga_ref[...]).astype(bf16)
    y_ref[...] = (x_ref[...]
                  + jnp.dot(mixa, wo_ref[0:D_ATTN], preferred_element_type=f32)
                  + jnp.dot(cg, wo_ref[D_ATTN:D_ATTN + D_CONV], preferred_element_type=f32))


def _sample_merge(attn, ga, u, gc, state_t, x, wo, dw, dwb, lng, lnb):
    vm = pl.BlockSpec(memory_space=pltpu.VMEM)
    return pl.pallas_call(
        _sample_merge_kernel,
        in_specs=[vm] * 11,
        out_specs=vm,
        out_shape=jax.ShapeDtypeStruct(x.shape, f32),
        compiler_params=pltpu.CompilerParams(vmem_limit_bytes=VMEM_LIMIT),
        name="sample_merge",
    )(attn, ga, u, gc, state_t, x, wo, dw, dwb, lng, lnb)


def kernel(x_prompt, x_sample, cache_k, cache_v, cache_k_idx, state_conv, page_table, rel_bias, norm_g,
           w_in, q_norm_g, k_norm_g, dw_w, dw_b, ln_g, ln_b, w_out):
    depth = norm_g.shape[0]
    assert depth == 1, "single-layer step"
    B, S, D = x_prompt.shape
    Bd, T, _ = x_sample.shape
    assert T == 1
    n_phys, page = cache_k.shape[1], cache_k.shape[2]
    past = page_table.shape[1] * page

    w = w_in[0]
    c_ki = 2 * D_ATTN + 2 * N_HEADS * HEAD_DIM + N_IDX_HEADS * IDX_DIM
    c_conv = c_ki + IDX_DIM + N_IDX_HEADS
    w_groups = (w[:, :c_ki].astype(bf16),
                jnp.pad(w[:, c_ki:c_conv], ((0, 0), (0, LANES - IDX_DIM - N_IDX_HEADS))).astype(bf16),
                w[:, c_conv:].astype(bf16))
    assert [g.shape[1] for g in w_groups] == [_C_KW, _C_A - _C_KW, _C_END - _C_A]
    wo = w_out[0].astype(bf16)
    g = norm_g[0][None]
    qg = jnp.tile(q_norm_g[0], N_HEADS)[None]
    kg = jnp.tile(k_norm_g[0], N_HEADS)[None]
    hid = np.arange(D_ATTN) // HEAD_DIM
    bd = jnp.asarray((hid[:, None] == hid[None, :]) / HEAD_DIM, dtype=bf16)
    dwb, lng, lnb = dw_b[0][None], ln_g[0][None], ln_b[0][None]

    (qt, k_p, kb, v_p, vt, ga_p, qit, kw_p, wit, ki2, u_p, gc_p) = _inproj(
        x_prompt, g, w_groups, qg, kg, bd, prompt=True)
    btab = _bias_tables(rel_bias)
    mixa = _prompt_attention(qt, qit, wit, kb, ki2, vt, ga_p, btab)
    dw8 = jnp.broadcast_to(dw_w[0][:, None, :], (CONV_W, SUB, D_CONV))
    y_prompt = _prompt_merge(mixa, u_p, gc_p, x_prompt, wo, dw8, dwb, lng, lnb)

    xs = x_sample.reshape(1, Bd, D)
    q_s, k_s, v_s, ga_s, qi_s, kw_s, u_s, gc_s = (a[0] for a in _inproj(
        xs, g, w_groups, qg, kg, bd, prompt=False))
    ki_s = kw_s[:, :IDX_DIM]
    wrow = jnp.broadcast_to(kw_s[:, IDX_DIM:IDX_DIM + N_IDX_HEADS, None], (Bd, N_IDX_HEADS, LANES))
    cki_t = jnp.transpose(cache_k_idx[0], (0, 2, 1))
    ck_t = jnp.transpose(cache_k[0], (0, 2, 3, 1)).reshape(n_phys, D_ATTN, page)
    cv_t = jnp.transpose(cache_v[0], (0, 2, 3, 1)).reshape(n_phys, D_ATTN, page)
    scores = _sample_scores(page_table, qi_s.reshape(Bd, N_IDX_HEADS, IDX_DIM), wrow, cki_t)
    mask, mnew, pany = _sample_select(scores, qi_s, jnp.tile(ki_s, (1, N_IDX_HEADS)), kw_s, past=past)
    rbt = jnp.broadcast_to(rel_bias[:, :, None], (N_BUCKETS, N_HEADS, LANES))
    eye = jnp.asarray(np.arange(N_HEADS)[:, None] == hid[None, :], dtype=f32)
    qbd = q_s[:, None, :] * eye[None]
    attn_s = _sample_attention(page_table, pany[:, :, 0], qbd, k_s[:, None, :], v_s[:, None, :],
                               mask, mnew, rbt, ck_t, cv_t, past=past)
    state_t = jnp.transpose(state_conv[0], (1, 0, 2))
    y_sample = _sample_merge(attn_s[:, 0], ga_s, u_s, gc_s, state_t, x_sample[:, 0], wo,
                             dw_w[0], dwb, lng, lnb)

    heads = (N_HEADS, HEAD_DIM)
    return (
        y_prompt,
        y_sample[:, None],
        k_p.reshape((1, B, S) + heads),
        v_p.reshape((1, B, S) + heads),
        kw_p[None, :, :, :IDX_DIM],
        u_p[None, :, S - (CONV_W - 1):],
        k_s.reshape((1, Bd, 1) + heads),
        v_s.reshape((1, Bd, 1) + heads),
        ki_s.reshape(1, Bd, 1, IDX_DIM),
        jnp.concatenate([state_conv[0][:, 1:], u_s[:, None]], axis=1)[None],
    )
```

```python
import functools
import math

import numpy as np
import jax
import jax.numpy as jnp
from jax import lax
from jax.experimental import pallas as pl
from jax.experimental.pallas import tpu as pltpu

N_HEADS = 8
HEAD_DIM = 64
N_IDX_HEADS = 8
IDX_DIM = 64
D_ATTN = N_HEADS * HEAD_DIM
D_CONV = 512
CONV_W = 31
TOPK_MAX = 256
N_BUCKETS = 32
MAX_DISTANCE = 128
EPS = 1e-6
NEG = -1e30
INT_MIN = -2 ** 31
LOG2E = math.log2(math.e)

LANES = 128
Q_TILE = 256
K_CHUNK = 256
PV_ROWS = 144
VMEM_LIMIT = 56 * 1024 * 1024

f32 = jnp.float32
bf16 = jnp.bfloat16


def _bucket_thresholds():
    n = np.arange(0, 4 * MAX_DISTANCE)
    max_exact = N_BUCKETS // 2
    nf = np.maximum(n, 1).astype(np.float32)
    large = max_exact + (np.log(nf / np.float32(max_exact)) / np.float32(math.log(MAX_DISTANCE / max_exact))
                         * np.float32(N_BUCKETS - max_exact)).astype(np.int32)
    bucket = np.where(n < max_exact, n, np.minimum(large, N_BUCKETS - 1))
    return [int(n[bucket >= i].min()) for i in range(1, N_BUCKETS)]


BUCKET_THRESH = _bucket_thresholds()
assert BUCKET_THRESH[-1] <= K_CHUNK + 1 and K_CHUNK % Q_TILE == 0
N_PAR = K_CHUNK // Q_TILE


def _silu(x):
    return x * jax.nn.sigmoid(x)


def _to_key(x):
    b = pltpu.bitcast(x, jnp.int32)
    return jnp.where(b < 0, b ^ jnp.int32(0x7FFFFFFF), b)


_C_Q, _C_K, _C_V, _C_GA, _C_QI, _C_KW, _C_A, _C_B, _C_GC, _C_END = (
    0, 512, 1024, 1536, 2048, 2560, 2688, 3200, 3712, 4224)


def _inproj_kernel(x_ref, g_ref, wa_ref, wb_ref, wc_ref, qg_ref, kg_ref, bd_ref, *outs, prompt, tm):
    xf = x_ref[0]
    ms = jnp.mean(xf * xf, axis=-1, keepdims=True)
    xn = (xf * lax.rsqrt(ms + EPS) * g_ref[...]).astype(bf16)

    def proj(a, b):
        if b <= _C_KW:
            w = wa_ref[:, a:b]
        elif b <= _C_A:
            w = wb_ref[:, a - _C_KW:b - _C_KW]
        else:
            w = wc_ref[:, a - _C_A:b - _C_A]
        return jnp.dot(xn, w, preferred_element_type=f32)

    def head_norm(z, gain_ref):
        msq = jnp.dot((z * z).astype(bf16), bd_ref[...], preferred_element_type=f32)
        return z * lax.rsqrt(msq + EPS) * gain_ref[...]

    q = head_norm(proj(_C_Q, _C_K), qg_ref)
    k = head_norm(proj(_C_K, _C_V), kg_ref)
    v = proj(_C_V, _C_GA)
    ga = _silu(proj(_C_GA, _C_QI))
    qi = proj(_C_QI, _C_KW)
    kw = proj(_C_KW, _C_A)
    u = proj(_C_A, _C_B) * jax.nn.sigmoid(proj(_C_B, _C_GC))
    gc = _silu(proj(_C_GC, _C_END))

    if prompt:
        q_o, k_o, kb_o, v_o, vt_o, ga_o, qi_o, kw_o, wi_o, ki2_o, u_o, gc_o = outs
        q_o[0] = (q * (HEAD_DIM ** -0.5 * LOG2E)).T.astype(bf16)
        qi_o[0] = (qi * (IDX_DIM ** -0.5)).T.astype(bf16)
        wi_o[0] = kw.T[IDX_DIM:IDX_DIM + N_IDX_HEADS, :]
        kb_o[0] = k.astype(bf16)
        lane = lax.broadcasted_iota(jnp.int32, kw.shape, 1)
        ki2_o[0] = jnp.where(lane < IDX_DIM, kw, pltpu.roll(kw, IDX_DIM, 1)).astype(bf16)
        vt = v.T.astype(bf16)
        tail = jnp.where(lax.broadcasted_iota(jnp.int32, (PV_ROWS - 2 * HEAD_DIM, K_CHUNK), 0) < 8,
                         1.0, 0.0).astype(bf16)
        for c in range(tm // K_CHUNK):
            for i in range(N_HEADS // 2):
                blk = vt[i * 2 * HEAD_DIM:(i + 1) * 2 * HEAD_DIM, c * K_CHUNK:(c + 1) * K_CHUNK]
                vt_o[0, c, i] = jnp.concatenate([blk, tail], axis=0)
    else:
        q_o, k_o, v_o, ga_o, qi_o, kw_o, u_o, gc_o = outs
        q_o[0] = q
        qi_o[0] = qi
    k_o[0] = k
    v_o[0] = v
    ga_o[0] = ga
    kw_o[0] = kw
    u_o[0] = u
    gc_o[0] = gc


def _inproj(x, norm_g, w_groups, qg, kg, bd, *, prompt):
    B, S, D = x.shape
    tm = 1024 if prompt else S
    assert S % tm == 0 and (not prompt or tm % K_CHUNK == 0)
    grid = (B, S // tm)
    row = lambda b, i: (b, i, 0)
    const2 = lambda b, i: (0, 0)
    col = lambda b, i: (b, 0, i)
    nck = tm // K_CHUNK
    if prompt:
        out_shape = (
            jax.ShapeDtypeStruct((B, D_ATTN, S), bf16),
            jax.ShapeDtypeStruct((B, S, D_ATTN), f32),
            jax.ShapeDtypeStruct((B, S, D_ATTN), bf16),
            jax.ShapeDtypeStruct((B, S, D_ATTN), f32),
            jax.ShapeDtypeStruct((B, S // K_CHUNK, N_HEADS // 2, PV_ROWS, K_CHUNK), bf16),
            jax.ShapeDtypeStruct((B, S, D_ATTN), f32),
            jax.ShapeDtypeStruct((B, N_IDX_HEADS * IDX_DIM, S), bf16),
            jax.ShapeDtypeStruct((B, S, LANES), f32),
            jax.ShapeDtypeStruct((B, N_IDX_HEADS, S), f32),
            jax.ShapeDtypeStruct((B, S, LANES), bf16),
            jax.ShapeDtypeStruct((B, S, D_CONV), f32),
            jax.ShapeDtypeStruct((B, S, D_CONV), f32),
        )
        out_specs = (
            pl.BlockSpec((1, D_ATTN, tm), col),
            pl.BlockSpec((1, tm, D_ATTN), row),
            pl.BlockSpec((1, tm, D_ATTN), row),
            pl.BlockSpec((1, tm, D_ATTN), row),
            pl.BlockSpec((1, nck, N_HEADS // 2, PV_ROWS, K_CHUNK), lambda b, i: (b, i, 0, 0, 0)),
            pl.BlockSpec((1, tm, D_ATTN), row),
            pl.BlockSpec((1, N_IDX_HEADS * IDX_DIM, tm), col),
            pl.BlockSpec((1, tm, LANES), row),
            pl.BlockSpec((1, N_IDX_HEADS, tm), col),
            pl.BlockSpec((1, tm, LANES), row),
            pl.BlockSpec((1, tm, D_CONV), row),
            pl.BlockSpec((1, tm, D_CONV), row),
        )
    else:
        out_shape = (
            jax.ShapeDtypeStruct((B, S, D_ATTN), f32),
            jax.ShapeDtypeStruct((B, S, D_ATTN), f32),
            jax.ShapeDtypeStruct((B, S, D_ATTN), f32),
            jax.ShapeDtypeStruct((B, S, D_ATTN), f32),
            jax.ShapeDtypeStruct((B, S, D_ATTN), f32),
            jax.ShapeDtypeStruct((B, S, LANES), f32),
            jax.ShapeDtypeStruct((B, S, D_CONV), f32),
            jax.ShapeDtypeStruct((B, S, D_CONV), f32),
        )
        out_specs = tuple(pl.BlockSpec((1, tm, s.shape[-1]), row) for s in out_shape)
    return pl.pallas_call(
        functools.partial(_inproj_kernel, prompt=prompt, tm=tm),
        grid=grid,
        in_specs=[
            pl.BlockSpec((1, tm, D), row),
            pl.BlockSpec((1, D), const2),
            pl.BlockSpec((D, _C_KW), const2, pipeline_mode=pl.Buffered(1)),
            pl.BlockSpec((D, _C_A - _C_KW), const2, pipeline_mode=pl.Buffered(1)),
            pl.BlockSpec((D, _C_END - _C_A), const2, pipeline_mode=pl.Buffered(1)),
            pl.BlockSpec((1, D_ATTN), const2),
            pl.BlockSpec((1, D_ATTN), const2),
            pl.BlockSpec((D_ATTN, D_ATTN), const2),
        ],
        out_specs=out_specs,
        out_shape=out_shape,
        compiler_params=pltpu.CompilerParams(
            dimension_semantics=("parallel", "parallel"), vmem_limit_bytes=VMEM_LIMIT),
        name="inproj_prompt" if prompt else "inproj_sample",
    )(x, norm_g, *w_groups, qg, kg, bd)


def _bias_table_kernel(rb_ref, out_ref):
    shape = (K_CHUNK, Q_TILE)
    krow = lax.broadcasted_iota(jnp.int32, shape, 0)
    qcol = lax.broadcasted_iota(jnp.int32, shape, 1)
    for par in range(N_PAR):
        for which in range(2):
            dist = Q_TILE * par + K_CHUNK * (1 - which) + qcol - krow
            for h in range(N_HEADS):
                far = rb_ref[N_BUCKETS - 1, h]
                acc = jnp.full(shape, (rb_ref[0, h] - far) * LOG2E, f32)
                for i, t in enumerate(BUCKET_THRESH):
                    acc = jnp.where(dist >= t, (rb_ref[i + 1, h] - far) * LOG2E, acc)
                out_ref[par, which, h // 2, :, (h % 2) * Q_TILE:(h % 2 + 1) * Q_TILE] = (
                    jnp.where(dist >= 0, acc, NEG))


def _bias_tables(rel_bias):
    return pl.pallas_call(
        _bias_table_kernel,
        in_specs=[pl.BlockSpec(memory_space=pltpu.SMEM)],
        out_specs=pl.BlockSpec(memory_space=pltpu.VMEM),
        out_shape=jax.ShapeDtypeStruct((N_PAR, 2, N_HEADS // 2, K_CHUNK, 2 * Q_TILE), f32),
        name="bias_tables",
    )(rel_bias)


N_PAIRS = N_HEADS // 2
SUB = 8


COUNT_ROWS = 32
MAX_ROWS = 16


def _bit_planes(words):
    a = list(words)
    j, m = 16, 0x0000FFFF
    while j:
        k = 0
        while k < 32:
            t = (a[k] ^ lax.shift_right_logical(a[k + j], jnp.int32(j))) & jnp.int32(m)
            a[k] = a[k] ^ t
            a[k + j] = a[k + j] ^ jnp.left_shift(t, jnp.int32(j))
            k = (k + j + 1) & ~j
        j >>= 1
        m = (m ^ (m << j)) & 0xFFFFFFFF
    return a


def _fold(x, op, rows):
    return op(x.reshape(x.shape[0] // rows, rows, x.shape[1]), axis=0)


def _attn_kernel(qt_ref, qit_ref, wit_ref, k_ref, ki_ref, vt_ref, ga_ref, bt_ref, out_ref,
                 sc_ref, key_ref, pl_ref, lg_ref, qbd_ref, qibd_ref, o_ref, *, ksel, idx_bits):
    j = pl.program_id(1)
    n_chunks = sc_ref.shape[0]
    nck = j // N_PAR + 1
    par = j % N_PAR
    shape = (K_CHUNK, Q_TILE)
    key_pos = lax.broadcasted_iota(jnp.int32, shape, 0)
    q_pos = j * Q_TILE + lax.broadcasted_iota(jnp.int32, shape, 1)

    top = lax.broadcasted_iota(jnp.int32, (2 * HEAD_DIM, Q_TILE), 0) < HEAD_DIM
    for i in range(N_PAIRS):
        rows = slice(i * 2 * HEAD_DIM, (i + 1) * 2 * HEAD_DIM)
        for src, dst in ((qt_ref, qbd_ref), (qit_ref, qibd_ref)):
            x = src[0, rows, :]
            zero = jnp.zeros_like(x)
            dst[i] = jnp.concatenate([jnp.where(top, x, zero), jnp.where(top, zero, x)], axis=1)
    wi = wit_ref[0] * (N_IDX_HEADS ** -0.5)
    wpair = [jnp.concatenate([wi[2 * i:2 * i + 1], wi[2 * i + 1:2 * i + 2]], axis=1)
             for i in range(N_PAIRS)]

    def chunk_rows(c):
        return pl.ds(pl.multiple_of(c * K_CHUNK, K_CHUNK), K_CHUNK)

    def for_chunk_pairs(n, body):
        def pair(cc, carry):
            body([2 * cc, 2 * cc + 1])
            return carry
        lax.fori_loop(0, n // 2, pair, 0)

        @pl.when(n % 2 == 1)
        def _():
            body([n - 1])

    def score_chunks(chunks):
        for c in chunks:
            ki2 = ki_ref[0, chunk_rows(c), :]
            acc = jnp.zeros(shape, f32)
            for i in range(N_PAIRS):
                r = jnp.maximum(jnp.dot(ki2, qibd_ref[i], preferred_element_type=f32), 0.0) * wpair[i]
                acc = acc + r[:, :Q_TILE] + r[:, Q_TILE:]
            keys = _to_key(jnp.where(c * K_CHUNK + key_pos <= q_pos, acc, -jnp.inf))
            key_ref[c] = keys
            ukeys = keys ^ jnp.int32(INT_MIN)
            planes = _bit_planes([ukeys[t * SUB:(t + 1) * SUB, :] for t in range(32)])
            for p in range(32):
                pl_ref[c, p] = planes[p]

    for_chunk_pairs(nck, score_chunks)

    def count(pred):
        def body(c, cnt):
            hit = pred(c)
            for g in range(K_CHUNK // COUNT_ROWS):
                rows = slice(g * COUNT_ROWS, (g + 1) * COUNT_ROWS)
                cnt = jnp.where(hit[rows], cnt + 1.0, cnt)
            return cnt
        cnt = lax.fori_loop(0, nck, body, jnp.zeros((COUNT_ROWS, Q_TILE), f32))
        return jnp.sum(cnt, axis=0, keepdims=True)

    def write_mask(pred):
        def body(c, carry):
            sc_ref[c] = jnp.where(pred(c), 0.0, NEG)
            return carry
        lax.fori_loop(0, nck, body, 0)

    need_search = (j + 1) * Q_TILE > ksel

    @pl.when(jnp.logical_not(need_search))
    def _():
        write_mask(lambda c: c * K_CHUNK + key_pos < (j + 1) * Q_TILE)

    @pl.when(need_search)
    def _():
        def clear_body(c, carry):
            pl_ref[c] = jnp.zeros(pl_ref.shape[1:], jnp.int32)
            return carry
        lax.fori_loop(nck, n_chunks, clear_body, 0)

        def bit_body(p, carry):
            n_above, thr_u, alive = carry
            ones = [alive[c] & pl_ref[c, p] for c in range(n_chunks)]
            acc = lax.population_count(ones[0])
            for c in range(1, n_chunks):
                acc = acc + lax.population_count(ones[c])
            n_hi = n_above + jnp.sum(acc.astype(f32), axis=0, keepdims=True)
            take = n_hi >= ksel
            takeb = jnp.broadcast_to(take, (SUB, Q_TILE))
            alive = tuple(jnp.where(takeb, ones[c], alive[c] ^ ones[c]) for c in range(n_chunks))
            bit = jnp.left_shift(jnp.int32(1), 31 - p)
            return jnp.where(take, n_above, n_hi), jnp.where(take, thr_u | bit, thr_u), alive

        everything = jnp.full((SUB, Q_TILE), -1, jnp.int32)
        n_above, thr_u, alive = lax.fori_loop(
            0, 32, bit_body,
            (jnp.zeros((1, Q_TILE), f32), jnp.zeros((1, Q_TILE), jnp.int32),
             tuple(jnp.where(c < nck, everything, 0) for c in range(n_chunks))))
        n_eq = lax.population_count(alive[0])
        for c in range(1, n_chunks):
            n_eq = n_eq + lax.population_count(alive[c])
        n_ge = n_above + jnp.sum(n_eq.astype(f32), axis=0, keepdims=True)
        thr = thr_u ^ jnp.int32(INT_MIN)
        thrb = jnp.broadcast_to(thr, shape)
        has_tie = jnp.max(n_ge) > ksel

        @pl.when(jnp.logical_not(has_tie))
        def _():
            write_mask(lambda c: key_ref[c] >= thrb)

        @pl.when(has_tie)
        def _():
            need = ksel - count(lambda c: key_ref[c] > thrb)

            def cut_body(i, x):
                cand = x + jnp.left_shift(jnp.int32(1), idx_bits - 1 - i)
                candb = jnp.broadcast_to(cand, shape)
                g = count(lambda c: jnp.where(key_ref[c] == thrb, c * K_CHUNK + key_pos, 2 ** 30) < candb)
                return jnp.where(g < need, cand, x)

            cut = lax.fori_loop(0, idx_bits, cut_body, jnp.zeros((1, Q_TILE), jnp.int32))
            cutb = jnp.broadcast_to(cut, shape)

            def sel(c):
                kc = key_ref[c]
                eq_idx = jnp.where(kc == thrb, c * K_CHUNK + key_pos, 2 ** 30)
                return jnp.where(kc > thrb, 0, eq_idx) <= cutb
            write_mask(sel)

    def logits_chunk(c, which, ms):
        mask = sc_ref[c]
        mask2 = jnp.concatenate([mask, mask], axis=1)
        out = []
        for i in range(N_PAIRS):
            kc = k_ref[0, chunk_rows(c), i * 2 * HEAD_DIM:(i + 1) * 2 * HEAD_DIM]
            s = jnp.dot(kc, qbd_ref[i], preferred_element_type=f32) + mask2
            if which is not None:
                s = s + bt_ref[par, which, i]
            lg_ref[i, c] = s
            out.append(jnp.maximum(ms[i], _fold(s, jnp.max, MAX_ROWS)))
        return tuple(out)

    ms = tuple(jnp.full((MAX_ROWS, 2 * Q_TILE), NEG, f32) for _ in range(N_PAIRS))
    nfar = jnp.maximum(nck - 2, 0)
    ms = lax.fori_loop(
        0, nfar // 2, lambda cc, ms: logits_chunk(2 * cc + 1, None, logits_chunk(2 * cc, None, ms)), ms)
    ms = lax.cond(nfar % 2 == 1, lambda ms: logits_chunk(nfar - 1, None, ms), lambda ms: ms, ms)
    ms = lax.cond(nck >= 2,
                  lambda ms: logits_chunk(nck - 1, 1, logits_chunk(nck - 2, 0, ms)),
                  lambda ms: logits_chunk(nck - 1, 1, ms), ms)
    mrow = [jnp.max(m, axis=0, keepdims=True) for m in ms]

    o_ref[...] = jnp.zeros(o_ref.shape, f32)

    def pv_chunks(chunks):
        for i in range(N_PAIRS):
            acc = None
            for c in chunks:
                p = jnp.exp2((lg_ref[i, c] - mrow[i]).astype(bf16))
                d = jnp.dot(vt_ref[0, c, i], p, preferred_element_type=f32)
                acc = d if acc is None else acc + d
            o_ref[i] += acc
    for_chunk_pairs(nck, pv_chunks)

    pieces = []
    for i in range(N_PAIRS):
        o = o_ref[i]
        l = o[2 * HEAD_DIM:2 * HEAD_DIM + 1, :]
        pieces.append(o[0:HEAD_DIM, 0:Q_TILE] / l[:, 0:Q_TILE])
        pieces.append(o[HEAD_DIM:2 * HEAD_DIM, Q_TILE:] / l[:, Q_TILE:])
    out_ref[0] = (jnp.concatenate(pieces, axis=0).T * ga_ref[0]).astype(bf16)


def _prompt_attention(qt, qit, wit, kb, ki2, vt, ga, btab):
    B, S, _ = kb.shape
    assert S % K_CHUNK == 0
    nq = S // Q_TILE
    nc = S // K_CHUNK
    ksel = min(TOPK_MAX, S // 4)
    qcol = lambda b, j: (b, 0, j)
    perb3 = lambda b, j: (b, 0, 0)
    return pl.pallas_call(
        functools.partial(_attn_kernel, ksel=ksel, idx_bits=max(1, (S - 1).bit_length())),
        grid=(B, nq),
        in_specs=[
            pl.BlockSpec((1, D_ATTN, Q_TILE), qcol),
            pl.BlockSpec((1, N_IDX_HEADS * IDX_DIM, Q_TILE), qcol),
            pl.BlockSpec((1, N_IDX_HEADS, Q_TILE), qcol),
            pl.BlockSpec((1, S, D_ATTN), perb3),
            pl.BlockSpec((1, S, LANES), perb3),
            pl.BlockSpec((1, nc, N_PAIRS, PV_ROWS, K_CHUNK), lambda b, j: (b, 0, 0, 0, 0)),
            pl.BlockSpec((1, Q_TILE, D_ATTN), lambda b, j: (b, j, 0)),
            pl.BlockSpec((N_PAR, 2, N_PAIRS, K_CHUNK, 2 * Q_TILE), lambda b, j: (0, 0, 0, 0, 0)),
        ],
        out_specs=pl.BlockSpec((1, Q_TILE, D_ATTN), lambda b, j: (b, j, 0)),
        out_shape=jax.ShapeDtypeStruct((B, S, D_ATTN), bf16),
        scratch_shapes=[
            pltpu.VMEM((nc, K_CHUNK, Q_TILE), f32),
            pltpu.VMEM((nc, K_CHUNK, Q_TILE), jnp.int32),
            pltpu.VMEM((nc, 32, SUB, Q_TILE), jnp.int32),
            pltpu.VMEM((N_PAIRS, nc, K_CHUNK, 2 * Q_TILE), f32),
            pltpu.VMEM((N_PAIRS, 2 * HEAD_DIM, 2 * Q_TILE), bf16),
            pltpu.VMEM((N_PAIRS, 2 * IDX_DIM, 2 * Q_TILE), bf16),
            pltpu.VMEM((N_PAIRS, PV_ROWS, 2 * Q_TILE), f32),
        ],
        compiler_params=pltpu.CompilerParams(
            dimension_semantics=("parallel", "arbitrary"), vmem_limit_bytes=VMEM_LIMIT),
        name="prompt_attention",
    )(qt, qit, wit, kb, ki2, vt, ga, btab)


CONV_HALO = 32
CONV_ROWS = 32


def _layer_norm_swish(c, lng_ref, lnb_ref):
    mu = jnp.mean(c, axis=-1, keepdims=True)
    d = c - mu
    var = jnp.mean(d * d, axis=-1, keepdims=True)
    return _silu(d * lax.rsqrt(var + EPS) * lng_ref[...] + lnb_ref[...])


def _merge_kernel(mixa_ref, ucur_ref, uprev_ref, gc_ref, x_ref, wo_ref, dw_ref, dwb_ref, lng_ref,
                  lnb_ref, y_ref, ext_ref, sh_ref, cg_ref, *, tq):
    i = pl.program_id(1)
    n = CONV_HALO + tq
    ext_ref[0:CONV_HALO] = jnp.where(i > 0, uprev_ref[0], 0.0)
    ext_ref[CONV_HALO:n] = ucur_ref[0]
    ext_ref[n:n + SUB] = jnp.zeros((SUB, D_CONV), f32)
    ext = ext_ref[...]
    for r in range(1, SUB):
        sh_ref[r - 1] = pltpu.roll(ext, n + SUB - r, 0)[0:n, :]
    first = CONV_HALO - (CONV_W - 1)
    for c in range(tq // CONV_ROWS):
        acc = jnp.broadcast_to(dwb_ref[...], (CONV_ROWS, D_CONV))
        for w in range(CONV_W):
            a, r = divmod(first + w, SUB)
            start = c * CONV_ROWS + a * SUB
            src = ext_ref if r == 0 else sh_ref.at[r - 1]
            acc = acc + src[start:start + CONV_ROWS, :] * jnp.tile(dw_ref[w], (CONV_ROWS // SUB, 1))
        rows = slice(c * CONV_ROWS, (c + 1) * CONV_ROWS)
        cg_ref[rows] = (_layer_norm_swish(acc, lng_ref, lnb_ref) * gc_ref[0, rows]).astype(bf16)
    y_ref[0] = (x_ref[0]
                + jnp.dot(mixa_ref[0], wo_ref[0:D_ATTN], preferred_element_type=f32)
                + jnp.dot(cg_ref[...], wo_ref[D_ATTN:D_ATTN + D_CONV], preferred_element_type=f32))


def _prompt_merge(mixa, u, gc, x, wo, dw, dwb, lng, lnb):
    B, S, D = x.shape
    tq = 512
    assert S % tq == 0 and tq % CONV_HALO == 0
    row = lambda b, i: (b, i, 0)
    const2 = lambda b, i: (0, 0)
    halo = lambda b, i: (b, jnp.maximum(i * (tq // CONV_HALO) - 1, 0), 0)
    return pl.pallas_call(
        functools.partial(_merge_kernel, tq=tq),
        grid=(B, S // tq),
        in_specs=[
            pl.BlockSpec((1, tq, D_ATTN), row),
            pl.BlockSpec((1, tq, D_CONV), row),
            pl.BlockSpec((1, CONV_HALO, D_CONV), halo),
            pl.BlockSpec((1, tq, D_CONV), row),
            pl.BlockSpec((1, tq, D), row),
            pl.BlockSpec((D_ATTN + D_CONV, D), const2),
            pl.BlockSpec((CONV_W, SUB, D_CONV), lambda b, i: (0, 0, 0)),
            pl.BlockSpec((1, D_CONV), const2),
            pl.BlockSpec((1, D_CONV), const2),
            pl.BlockSpec((1, D_CONV), const2),
        ],
        out_specs=pl.BlockSpec((1, tq, D), row),
        out_shape=jax.ShapeDtypeStruct((B, S, D), f32),
        scratch_shapes=[
            pltpu.VMEM((CONV_HALO + tq + SUB, D_CONV), f32),
            pltpu.VMEM((SUB - 1, CONV_HALO + tq, D_CONV), f32),
            pltpu.VMEM((tq, D_CONV), bf16),
        ],
        compiler_params=pltpu.CompilerParams(
            dimension_semantics=("parallel", "arbitrary"), vmem_limit_bytes=VMEM_LIMIT),
        name="prompt_merge",
    )(mixa, u, u, gc, x, wo, dw, dwb, lng, lnb)


SCORE_UNROLL = 16


def _sample_score_kernel(pt_ref, qi_ref, w_ref, cki_ref, out_ref, buf_ref, sem_ref, *, npg):
    b = pl.program_id(0)
    nb = pl.num_programs(0)
    slot = b % 2

    def page_copy(bb, p, sl):
        return pltpu.make_async_copy(cki_ref.at[pt_ref[bb, p]], buf_ref.at[sl, p], sem_ref.at[sl])

    def start_all(bb, sl):
        def body(pp, carry):
            page_copy(bb, 2 * pp, sl).start(priority=0)
            page_copy(bb, 2 * pp + 1, sl).start(priority=1)
            return carry
        lax.fori_loop(0, npg // 2, body, 0)

    @pl.when(b == 0)
    def _():
        start_all(0, 0)

    @pl.when(b + 1 < nb)
    def _():
        start_all(b + 1, 1 - slot)

    def wait_body(p, carry):
        page_copy(b, p, slot).wait()
        return carry
    lax.fori_loop(0, npg, wait_body, 0)

    qs = (qi_ref[0] * (IDX_DIM ** -0.5)).astype(bf16)
    wcol = w_ref[0] * (N_IDX_HEADS ** -0.5)
    wcol2 = jnp.concatenate([wcol, wcol], axis=1)

    def body(i, carry):
        for u in range(0, SCORE_UNROLL, 2):
            p = i * SCORE_UNROLL + u
            pages = jnp.concatenate([buf_ref[slot, p], buf_ref[slot, p + 1]], axis=1).astype(bf16)
            s = jnp.dot(qs, pages, preferred_element_type=f32)
            r = jnp.sum(jnp.maximum(s, 0.0) * wcol2, axis=0, keepdims=True)
            out_ref[0, pl.ds(p, 1), :] = r[:, :LANES]
            out_ref[0, pl.ds(p + 1, 1), :] = r[:, LANES:]
        return carry
    lax.fori_loop(0, npg // SCORE_UNROLL, body, 0)


def _sample_scores(page_table, qi3, wrow, cki_t):
    Bd, npg = page_table.shape
    page = cki_t.shape[2]
    assert page == LANES and npg % SCORE_UNROLL == 0
    grid_spec = pltpu.PrefetchScalarGridSpec(
        num_scalar_prefetch=1,
        grid=(Bd,),
        in_specs=[
            pl.BlockSpec((1, N_IDX_HEADS, IDX_DIM), lambda b, pt: (b, 0, 0)),
            pl.BlockSpec((1, N_IDX_HEADS, LANES), lambda b, pt: (b, 0, 0)),
            pl.BlockSpec(memory_space=pl.ANY),
        ],
        out_specs=pl.BlockSpec((1, npg, page), lambda b, pt: (b, 0, 0)),
        scratch_shapes=[
            pltpu.VMEM((2, npg, IDX_DIM, page), f32),
            pltpu.SemaphoreType.DMA((2,)),
        ],
    )
    return pl.pallas_call(
        functools.partial(_sample_score_kernel, npg=npg),
        grid_spec=grid_spec,
        out_shape=jax.ShapeDtypeStruct((Bd, npg, page), f32),
        compiler_params=pltpu.CompilerParams(
            dimension_semantics=("arbitrary",), vmem_limit_bytes=VMEM_LIMIT),
        name="sample_scores",
    )(page_table, qi3, wrow, cki_t)


def _sample_select_kernel(sc_ref, qi_ref, kit_ref, kw_ref, mask_ref, mnew_ref, pany_ref, key_ref, *,
                          npg, ksel, past, idx_bits):
    Bd = sc_ref.shape[0]
    shape = (Bd, npg, LANES)

    qb = (qi_ref[...] * (IDX_DIM ** -0.5)).astype(bf16).astype(f32)
    kb = kit_ref[...].astype(bf16).astype(f32)
    prod = qb * kb
    grp = lax.broadcasted_iota(jnp.int32, prod.shape, 1) // IDX_DIM
    kwv = kw_ref[...]
    s_new = jnp.zeros((Bd, 1), f32)
    for h in range(N_IDX_HEADS):
        sh = jnp.sum(jnp.where(grp == h, prod, 0.0), axis=1, keepdims=True)
        s_new = s_new + jnp.maximum(sh, 0.0) * (kwv[:, IDX_DIM + h:IDX_DIM + h + 1] * (N_IDX_HEADS ** -0.5))
    key_new = _to_key(s_new).reshape(Bd, 1, 1)

    key_ref[...] = _to_key(sc_ref[...])
    pos = (lax.broadcasted_iota(jnp.int32, shape, 1) * LANES
           + lax.broadcasted_iota(jnp.int32, shape, 2))

    def count(pred, pred_new):
        x = jnp.where(pred, 1.0, 0.0)
        s = jnp.sum(jnp.sum(x, axis=1, keepdims=True), axis=2, keepdims=True)
        return s + jnp.where(pred_new, 1.0, 0.0)

    def bit_body(i, lo):
        cand = lo + jnp.left_shift(jnp.int32(1), 31 - i)
        tot = count(key_ref[...] >= cand, key_new >= cand)
        return jnp.where(tot >= ksel, cand, lo)

    thr = lax.fori_loop(0, 32, bit_body, jnp.full((Bd, 1, 1), INT_MIN, jnp.int32))
    keys = key_ref[...]
    need = ksel - count(keys > thr, key_new > thr)
    big = jnp.int32(2 ** 30)
    eq_pos = jnp.where(keys == thr, pos, big)
    eq_pos_new = jnp.where(key_new == thr, jnp.int32(past), big)

    def cut_body(i, x):
        cand = x + jnp.left_shift(jnp.int32(1), idx_bits - 1 - i)
        g = count(eq_pos < cand, eq_pos_new < cand)
        return jnp.where(g < need, cand, x)

    cut = lax.fori_loop(0, idx_bits, cut_body, jnp.zeros((Bd, 1, 1), jnp.int32))
    sel = jnp.where(keys > thr, 0, eq_pos) <= cut
    mask_ref[...] = jnp.where(sel, 0.0, NEG)
    pany_ref[...] = jnp.max(jnp.where(sel, 1.0, 0.0), axis=2, keepdims=True).astype(jnp.int32)
    sel_new = jnp.where(key_new > thr, 0, eq_pos_new) <= cut
    mnew_ref[...] = jnp.broadcast_to(jnp.where(sel_new, 0.0, NEG), mnew_ref.shape)


def _sample_select(scores, qi, ki_tiled, kw, *, past):
    Bd, npg, _ = scores.shape
    ksel = min(TOPK_MAX, (past + 1) // 4)
    vm = pl.BlockSpec(memory_space=pltpu.VMEM)
    return pl.pallas_call(
        functools.partial(_sample_select_kernel, npg=npg, ksel=ksel, past=past,
                          idx_bits=past.bit_length()),
        in_specs=[vm, vm, vm, vm],
        out_specs=(vm, vm, vm),
        out_shape=(jax.ShapeDtypeStruct((Bd, npg, LANES), f32),
                   jax.ShapeDtypeStruct((Bd, N_HEADS, LANES), f32),
                   jax.ShapeDtypeStruct((Bd, npg, 1), jnp.int32)),
        scratch_shapes=[pltpu.VMEM((Bd, npg, LANES), jnp.int32)],
        compiler_params=pltpu.CompilerParams(vmem_limit_bytes=VMEM_LIMIT),
        name="sample_select",
    )(scores, qi, ki_tiled, kw)


PAGE_BUFS = 32
PAGE_GROUP = 4


def _sample_attn_kernel(pt_ref, pany_ref, qbd_ref, knew_ref, vnew_ref, mask_ref, mnew_ref, rbt_ref, ck_ref, cv_ref,
                        out_ref, buf_ref, lg_ref, sem_ref, *, npg, past):
    b = pl.program_id(0)
    nb = pl.num_programs(0)
    per_b = 2 * npg
    total = nb * per_b

    def page_copy(g, src_ref):
        phys = pt_ref[g // per_b, g % npg]
        slot = g % PAGE_BUFS
        return pltpu.make_async_copy(src_ref.at[phys], buf_ref.at[slot], sem_ref.at[slot])

    def needed(g):
        return pany_ref[g // per_b, g % npg] != 0

    def start(g):
        is_k = g % per_b < npg

        @pl.when(jnp.logical_and(needed(g), is_k))
        def _():
            page_copy(g, ck_ref).start()

        @pl.when(jnp.logical_and(needed(g), jnp.logical_not(is_k)))
        def _():
            page_copy(g, cv_ref).start()

    def wait(g):
        @pl.when(needed(g))
        def _():
            page_copy(g, ck_ref).wait()

    @pl.when(b == 0)
    def _():
        for slot in range(PAGE_BUFS):
            buf_ref[slot] = jnp.zeros(buf_ref.shape[1:], f32)
        for g in range(PAGE_BUFS):
            start(g)

    g0 = b * per_b
    qbd = (qbd_ref[0] * (HEAD_DIM ** -0.5 * LOG2E)).astype(bf16)
    blk = (lax.broadcasted_iota(jnp.int32, (N_HEADS, D_ATTN), 0)
           == lax.broadcasted_iota(jnp.int32, (N_HEADS, D_ATTN), 1) // HEAD_DIM)
    tok = lax.broadcasted_iota(jnp.int32, (N_HEADS, LANES), 1)

    def bias_of(dist):
        far = rbt_ref[N_BUCKETS - 1]
        acc = jnp.broadcast_to(rbt_ref[0] - far, dist.shape)
        for i, t in enumerate(BUCKET_THRESH):
            acc = jnp.where(dist >= t, rbt_ref[i + 1] - far, acc)
        return acc * LOG2E

    def refill(gs):
        for g in gs:
            @pl.when(g + PAGE_BUFS < total)
            def _():
                start(g + PAGE_BUFS)

    def k_body(it, m):
        gs = [g0 + it * PAGE_GROUP + u for u in range(PAGE_GROUP)]
        for g in gs:
            wait(g)
        for u, g in enumerate(gs):
            p = it * PAGE_GROUP + u
            kp = buf_ref[g % PAGE_BUFS].astype(bf16)
            s = jnp.dot(qbd, kp, preferred_element_type=f32) + mask_ref[0, pl.ds(p, 1), :]
            lg_ref[p] = s
            m = jnp.maximum(m, s)
        refill(gs)
        return m

    m = lax.fori_loop(0, npg // PAGE_GROUP, k_body, jnp.full((N_HEADS, LANES), NEG, f32))
    last = lg_ref[npg - 1] + bias_of(past - ((npg - 1) * LANES + tok))
    lg_ref[npg - 1] = last
    m = jnp.maximum(m, last)

    s_new = (jnp.sum(qbd_ref[0] * knew_ref[0], axis=1, keepdims=True) * (HEAD_DIM ** -0.5 * LOG2E)
             + bias_of(jnp.zeros((N_HEADS, LANES), jnp.int32))[:, 0:1] + mnew_ref[0][:, 0:1])
    m_row = jnp.maximum(jnp.max(m, axis=1, keepdims=True), s_new)
    mb = jnp.broadcast_to(m_row, (N_HEADS, LANES))

    def v_body(it, carry):
        acc, l = carry
        gs = [g0 + npg + it * PAGE_GROUP + u for u in range(PAGE_GROUP)]
        for g in gs:
            wait(g)
        for u, g in enumerate(gs):
            vp = buf_ref[g % PAGE_BUFS].astype(bf16)
            pr = jnp.exp2(lg_ref[it * PAGE_GROUP + u] - mb)
            acc = acc + lax.dot_general(pr.astype(bf16), vp, (((1,), (1,)), ((), ())),
                                        preferred_element_type=f32)
            l = l + pr
        refill(gs)
        return acc, l

    acc, l = lax.fori_loop(0, npg // PAGE_GROUP, v_body, (jnp.zeros((N_HEADS, D_ATTN), f32),
                                                          jnp.zeros((N_HEADS, LANES), f32)))
    p_new = jnp.exp2(s_new - m_row)
    l_tot = jnp.sum(l, axis=1, keepdims=True) + p_new
    num = jnp.where(blk, acc + p_new * vnew_ref[0], 0.0) / l_tot
    out_ref[0] = jnp.sum(num, axis=0, keepdims=True)


def _sample_attention(page_table, pany, qbd, k_new_flat, v_new_flat, mask, mnew, rbt, ck_t, cv_t, *, past):
    Bd, npg = page_table.shape
    rows, page = ck_t.shape[1], ck_t.shape[2]
    assert rows == D_ATTN and page == LANES
    assert npg % PAGE_GROUP == 0 and PAGE_BUFS % PAGE_GROUP == 0 and 2 * npg >= PAGE_BUFS
    grid_spec = pltpu.PrefetchScalarGridSpec(
        num_scalar_prefetch=2,
        grid=(Bd,),
        in_specs=[
            pl.BlockSpec((1, N_HEADS, D_ATTN), lambda b, pt, pa: (b, 0, 0)),
            pl.BlockSpec((1, 1, D_ATTN), lambda b, pt, pa: (b, 0, 0)),
            pl.BlockSpec((1, 1, D_ATTN), lambda b, pt, pa: (b, 0, 0)),
            pl.BlockSpec((1, npg, LANES), lambda b, pt, pa: (b, 0, 0)),
            pl.BlockSpec((1, N_HEADS, LANES), lambda b, pt, pa: (b, 0, 0)),
            pl.BlockSpec((N_BUCKETS, N_HEADS, LANES), lambda b, pt, pa: (0, 0, 0)),
            pl.BlockSpec(memory_space=pl.ANY),
            pl.BlockSpec(memory_space=pl.ANY),
        ],
        out_specs=pl.BlockSpec((1, 1, D_ATTN), lambda b, pt, pa: (b, 0, 0)),
        scratch_shapes=[
            pltpu.VMEM((PAGE_BUFS, rows, page), f32),
            pltpu.VMEM((npg, N_HEADS, LANES), f32),
            pltpu.SemaphoreType.DMA((PAGE_BUFS,)),
        ],
    )
    return pl.pallas_call(
        functools.partial(_sample_attn_kernel, npg=npg, past=past),
        grid_spec=grid_spec,
        out_shape=jax.ShapeDtypeStruct((Bd, 1, D_ATTN), f32),
        compiler_params=pltpu.CompilerParams(
            dimension_semantics=("arbitrary",), vmem_limit_bytes=VMEM_LIMIT),
        name="sample_attention",
    )(page_table, pany, qbd, k_new_flat, v_new_flat, mask, mnew, rbt, ck_t, cv_t)


def _sample_merge_kernel(attn_ref, ga_ref, u_ref, gc_ref, st_ref, x_ref, wo_ref, dw_ref, dwb_ref,
                         lng_ref, lnb_ref, y_ref):
    acc = dwb_ref[...] + u_ref[...] * dw_ref[CONV_W - 1:CONV_W, :]
    for w in range(CONV_W - 1):
        acc = acc + st_ref[w] * dw_ref[w:w + 1, :]
    cg = (_layer_norm_swish(acc, lng_ref, lnb_ref) * gc_ref[...]).astype(bf16)
    mixa = (attn_ref[...] * ga_ref[...]).astype(bf16)
    y_ref[...] = (x_ref[...]
                  + jnp.dot(mixa, wo_ref[0:D_ATTN], preferred_element_type=f32)
                  + jnp.dot(cg, wo_ref[D_ATTN:D_ATTN + D_CONV], preferred_element_type=f32))


def _sample_merge(attn, ga, u, gc, state_t, x, wo, dw, dwb, lng, lnb):
    vm = pl.BlockSpec(memory_space=pltpu.VMEM)
    return pl.pallas_call(
        _sample_merge_kernel,
        in_specs=[vm] * 11,
        out_specs=vm,
        out_shape=jax.ShapeDtypeStruct(x.shape, f32),
        compiler_params=pltpu.CompilerParams(vmem_limit_bytes=VMEM_LIMIT),
        name="sample_merge",
    )(attn, ga, u, gc, state_t, x, wo, dw, dwb, lng, lnb)


def kernel(x_prompt, x_sample, cache_k, cache_v, cache_k_idx, state_conv, page_table, rel_bias, norm_g,
           w_in, q_norm_g, k_norm_g, dw_w, dw_b, ln_g, ln_b, w_out):
    depth = norm_g.shape[0]
    assert depth == 1, "single-layer step"
    B, S, D = x_prompt.shape
    Bd, T, _ = x_sample.shape
    assert T == 1
    n_phys, page = cache_k.shape[1], cache_k.shape[2]
    past = page_table.shape[1] * page

    w = w_in[0]
    c_ki = 2 * D_ATTN + 2 * N_HEADS * HEAD_DIM + N_IDX_HEADS * IDX_DIM
    c_conv = c_ki + IDX_DIM + N_IDX_HEADS
    w_groups = (w[:, :c_ki].astype(bf16),
                jnp.pad(w[:, c_ki:c_conv], ((0, 0), (0, LANES - IDX_DIM - N_IDX_HEADS))).astype(bf16),
                w[:, c_conv:].astype(bf16))
    assert [g.shape[1] for g in w_groups] == [_C_KW, _C_A - _C_KW, _C_END - _C_A]
    wo = w_out[0].astype(bf16)
    g = norm_g[0][None]
    qg = jnp.tile(q_norm_g[0], N_HEADS)[None]
    kg = jnp.tile(k_norm_g[0], N_HEADS)[None]
    hid = np.arange(D_ATTN) // HEAD_DIM
    bd = jnp.asarray((hid[:, None] == hid[None, :]) / HEAD_DIM, dtype=bf16)
    dwb, lng, lnb = dw_b[0][None], ln_g[0][None], ln_b[0][None]

    (qt, k_p, kb, v_p, vt, ga_p, qit, kw_p, wit, ki2, u_p, gc_p) = _inproj(
        x_prompt, g, w_groups, qg, kg, bd, prompt=True)
    btab = _bias_tables(rel_bias)
    mixa = _prompt_attention(qt, qit, wit, kb, ki2, vt, ga_p, btab)
    dw8 = jnp.broadcast_to(dw_w[0][:, None, :], (CONV_W, SUB, D_CONV))
    y_prompt = _prompt_merge(mixa, u_p, gc_p, x_prompt, wo, dw8, dwb, lng, lnb)

    xs = x_sample.reshape(1, Bd, D)
    q_s, k_s, v_s, ga_s, qi_s, kw_s, u_s, gc_s = (a[0] for a in _inproj(
        xs, g, w_groups, qg, kg, bd, prompt=False))
    ki_s = kw_s[:, :IDX_DIM]
    wrow = jnp.broadcast_to(kw_s[:, IDX_DIM:IDX_DIM + N_IDX_HEADS, None], (Bd, N_IDX_HEADS, LANES))
    cki_t = jnp.transpose(cache_k_idx[0], (0, 2, 1))
    ck_t = jnp.transpose(cache_k[0], (0, 2, 3, 1)).reshape(n_phys, D_ATTN, page)
    cv_t = jnp.transpose(cache_v[0], (0, 2, 3, 1)).reshape(n_phys, D_ATTN, page)
    scores = _sample_scores(page_table, qi_s.reshape(Bd, N_IDX_HEADS, IDX_DIM), wrow, cki_t)
    mask, mnew, pany = _sample_select(scores, qi_s, jnp.tile(ki_s, (1, N_IDX_HEADS)), kw_s, past=past)
    rbt = jnp.broadcast_to(rel_bias[:, :, None], (N_BUCKETS, N_HEADS, LANES))
    eye = jnp.asarray(np.arange(N_HEADS)[:, None] == hid[None, :], dtype=f32)
    qbd = q_s[:, None, :] * eye[None]
    attn_s = _sample_attention(page_table, pany[:, :, 0], qbd, k_s[:, None, :], v_s[:, None, :],
                               mask, mnew, rbt, ck_t, cv_t, past=past)
    state_t = jnp.transpose(state_conv[0], (1, 0, 2))
    y_sample = _sample_merge(attn_s[:, 0], ga_s, u_s, gc_s, state_t, x_sample[:, 0], wo,
                             dw_w[0], dwb, lng, lnb)

    heads = (N_HEADS, HEAD_DIM)
    return (
        y_prompt,
        y_sample[:, None],
        k_p.reshape((1, B, S) + heads),
        v_p.reshape((1, B, S) + heads),
        kw_p[None, :, :, :IDX_DIM],
        u_p[None, :, S - (CONV_W - 1):],
        k_s.reshape((1, Bd, 1) + heads),
        v_s.reshape((1, Bd, 1) + heads),
        ki_s.reshape(1, Bd, 1, IDX_DIM),
        jnp.concatenate([state_conv[0][:, 1:], u_s[:, None]], axis=1)[None],
    )
```
